```python
import jax
import jax.numpy as jnp
from jax import lax
import numpy as np

D_MODEL = 1024
BATCH = 16
SEQ = 256
DEPTH = 2
DEC_BATCH = 8
DEC_SEQ = 1024
PAST_LEN = 512

GRID_W = 64
N_BRANCH = 4
BR_W = 1024
LRU_W = BR_W
LRU_BLOCKS = 16
LRU_BW = LRU_W // LRU_BLOCKS
LRU_CONV = 4
LRU_C = 8.0
CONV_W = BR_W
CONV_K = 31
N_HEADS = 16
N_KV_HEADS = 4
HEAD_DIM = 64
Q_PER_KV = N_HEADS // N_KV_HEADS
WINDOW = 128
ATT_BLOCK = 128
ROPE_BASE = 10000.0
POOL_W = BR_W
POOL_SIZES = (2, 4, 8, 16)
POOL_G = POOL_W // len(POOL_SIZES)
IN_SPLITS = (LRU_W, LRU_W, 2 * CONV_W, N_HEADS * HEAD_DIM, N_KV_HEADS * HEAD_DIM, N_KV_HEADS * HEAD_DIM, POOL_W)
IN_W = sum(IN_SPLITS)
N_EXPERTS = 32
TOP_K = 4
D_EXPERT = D_MODEL
SWIGLU_LIMIT = 7.0
SWIGLU_ALPHA = 1.702
MOE_BLOCK = 128
EPS = 1e-6
NEG_INF = -1e30

kernel_name = 'hybrid_flow_rglru_conv_swa_pool_moe_step'


def rms_norm(x, g):
    x32 = x.astype(jnp.float32)
    y = x32 * lax.rsqrt(jnp.mean(x32 * x32, axis=-1, keepdims=True) + EPS)
    return (y * g.astype(jnp.float32)).astype(x.dtype)


def layer_norm(x, g, b):
    x32 = x.astype(jnp.float32)
    xc = x32 - jnp.mean(x32, axis=-1, keepdims=True)
    y = xc * lax.rsqrt(jnp.mean(xc * xc, axis=-1, keepdims=True) + EPS)
    return (y * g.astype(jnp.float32) + b.astype(jnp.float32)).astype(x.dtype)


def split_columns(u):
    cuts = [int(v) for v in np.cumsum(IN_SPLITS)[:-1]]
    return jnp.split(u, cuts, axis=-1)


def dw_conv(x, w, b, left, right):
    y = lax.conv_general_dilated(x, w[:, None, :].astype(x.dtype), window_strides=(1,), padding=[(left, right)],
                                 dimension_numbers=('NWC', 'WIO', 'NWC'), feature_group_count=x.shape[-1])
    return y + b


def affine_combine(e1, e2):
    a1, b1 = e1
    a2, b2 = e2
    return a1 * a2, a2 * b1 + b2


def rglru_direction(x, w_r, b_r, w_i, b_i, lam, h0):
    b, n, _ = x.shape
    xb = x.reshape(b, n, LRU_BLOCKS, LRU_BW)
    r = jax.nn.sigmoid(jnp.einsum('bnhi,hij->bnhj', xb, w_r.astype(jnp.float32)).reshape(b, n, LRU_W) + b_r.astype(jnp.float32))
    i = jax.nn.sigmoid(jnp.einsum('bnhi,hij->bnhj', xb, w_i.astype(jnp.float32)).reshape(b, n, LRU_W) + b_i.astype(jnp.float32))
    log_a = -LRU_C * r * jax.nn.softplus(-lam.astype(jnp.float32))
    a = jnp.exp(log_a)
    bx = jnp.sqrt(-jnp.expm1(2.0 * log_a)) * (i * x)
    a_cum, b_cum = lax.associative_scan(affine_combine, (a, bx), axis=1)
    return a_cum * h0.astype(jnp.float32)[:, None, :] + b_cum


def rglru_branch(ux, uy, p, h0_f, h0_b):
    xc = dw_conv(ux, p['lru_conv_w'], p['lru_conv_b'], LRU_CONV // 2, LRU_CONV - 1 - LRU_CONV // 2).astype(jnp.float32)
    hf = rglru_direction(xc, p['lru_w_r'][0], p['lru_b_r'][0], p['lru_w_i'][0], p['lru_b_i'][0], p['lru_lambda'][0], h0_f)
    hb = jnp.flip(rglru_direction(jnp.flip(xc, axis=1), p['lru_w_r'][1], p['lru_b_r'][1], p['lru_w_i'][1],
                                  p['lru_b_i'][1], p['lru_lambda'][1], h0_b), axis=1)
    out = ((hf + hb) * jax.nn.gelu(uy.astype(jnp.float32))).astype(ux.dtype)
    return out, hf, hb


def conformer_branch(u, p):
    g = u[..., :CONV_W] * jax.nn.sigmoid(u[..., CONV_W:])
    h = dw_conv(g, p['conv_dw_w'], p['conv_dw_b'], CONV_K // 2, CONV_K // 2)
    return jax.nn.silu(layer_norm(h, p['conv_ln_g'], p['conv_ln_b']))


def centred_mean(x, w):
    n = x.shape[1]
    cs = jnp.pad(jnp.cumsum(x, axis=1), ((0, 0), (1, 0), (0, 0)))
    t = jnp.arange(n)
    lo = jnp.clip(t - w // 2, 0, n)
    hi = jnp.clip(t + w - w // 2, 0, n)
    return (cs[:, hi] - cs[:, lo]) / (hi - lo).astype(jnp.float32)[None, :, None]


def pool_branch(u, p):
    b, n, _ = u.shape
    groups = u.astype(jnp.float32).reshape(b, n, len(POOL_SIZES), POOL_G)
    pooled = jnp.stack([centred_mean(groups[:, :, gi], POOL_SIZES[gi]) for gi in range(len(POOL_SIZES))], axis=2) - groups
    mixed = jnp.einsum('bngc,gcd->bngd', pooled, p['pool_w'].astype(jnp.float32)).reshape(b, n, POOL_W)
    return (mixed * p['pool_scale'].astype(jnp.float32)).astype(u.dtype)


def rope_2d(x):
    n = x.shape[1]
    n_rows = n // GRID_W
    rows = jnp.repeat(jnp.arange(n_rows), GRID_W).astype(jnp.float32)
    cols = jnp.tile(jnp.arange(GRID_W), n_rows).astype(jnp.float32)
    half = HEAD_DIM // 2
    freqs = ROPE_BASE ** (-jnp.arange(0, half, 2, dtype=jnp.float32) / half)

    def rot(xa, pos):
        ang = pos[:, None] * freqs[None, :]
        cos = jnp.cos(ang)[None, :, None, :]
        sin = jnp.sin(ang)[None, :, None, :]
        x1, x2 = xa[..., :half // 2], xa[..., half // 2:]
        return jnp.concatenate([x1 * cos - x2 * sin, x2 * cos + x1 * sin], axis=-1)

    x32 = x.astype(jnp.float32)
    return jnp.concatenate([rot(x32[..., :half], rows), rot(x32[..., half:], cols)], axis=-1).astype(x.dtype)


def attn_softmax(q, k, v, sink, mask):
    s = jnp.einsum('bqhgd,bkhd->bhgqk', q, k, preferred_element_type=jnp.float32) * (HEAD_DIM ** -0.5)
    if mask is not None:
        s = jnp.where(mask, s, NEG_INF)
    sk = jnp.broadcast_to(sink.astype(jnp.float32).reshape(N_KV_HEADS, Q_PER_KV)[None, :, :, None, None], s.shape[:-1] + (1,))
    pr = jax.nn.softmax(jnp.concatenate([s, sk], axis=-1), axis=-1)[..., :-1]
    return jnp.einsum('bhgqk,bkhd->bqhgd', pr.astype(v.dtype), v)


def context_attention(q, k, v, sink):
    b, n = q.shape[:2]
    nb = n // ATT_BLOCK
    qb = jnp.swapaxes(q.reshape(b, nb, ATT_BLOCK, N_KV_HEADS, Q_PER_KV, HEAD_DIM), 0, 1)
    out = lax.map(lambda qi: attn_softmax(qi, k, v, sink, None), qb)
    return jnp.swapaxes(out, 0, 1).reshape(b, n, N_HEADS * HEAD_DIM)


def latent_attention(q, k, v, k_ctx, v_ctx, sink):
    b, n = q.shape[:2]
    nb = n // ATT_BLOCK
    qb = jnp.swapaxes(q.reshape(b, nb, ATT_BLOCK, N_KV_HEADS, Q_PER_KV, HEAD_DIM), 0, 1)
    pad = ((0, 0), (ATT_BLOCK, ATT_BLOCK), (0, 0), (0, 0))
    kp = jnp.pad(k, pad)
    vp = jnp.pad(v, pad)
    r = jnp.arange(ATT_BLOCK)[:, None]
    cidx = jnp.arange(3 * ATT_BLOCK)[None, :]
    band = jnp.abs(r + ATT_BLOCK - cidx) <= WINDOW
    ctx_ok = jnp.ones((ATT_BLOCK, k_ctx.shape[1]), dtype=bool)

    def one_block(args):
        i, qi = args
        kpos = (i - 1) * ATT_BLOCK + cidx
        mask = jnp.concatenate([band & (kpos >= 0) & (kpos < n), ctx_ok], axis=1)
        start = i * ATT_BLOCK
        kk = jnp.concatenate([lax.dynamic_slice_in_dim(kp, start, 3 * ATT_BLOCK, axis=1), k_ctx], axis=1)
        vv = jnp.concatenate([lax.dynamic_slice_in_dim(vp, start, 3 * ATT_BLOCK, axis=1), v_ctx], axis=1)
        return attn_softmax(qi, kk, vv, sink, mask)

    out = lax.map(one_block, (jnp.arange(nb), qb))
    return jnp.swapaxes(out, 0, 1).reshape(b, n, N_HEADS * HEAD_DIM)


def token_mixers(xn, p, ctx):
    b, n, _ = xn.shape
    u = jnp.dot(xn, p['w_in'])
    ux, uy, uc, uq, uk, uv, up = split_columns(u)
    q = rms_norm(uq.reshape(b, n, N_HEADS, HEAD_DIM), p['q_norm_g'])
    k = rms_norm(uk.reshape(b, n, N_KV_HEADS, HEAD_DIM), p['k_norm_g'])
    v = uv.reshape(b, n, N_KV_HEADS, HEAD_DIM)
    if ctx is None:
        zero = jnp.zeros((b, LRU_W), jnp.float32)
        lru, hf, hb = rglru_branch(ux, uy, p, zero, zero)
        att = context_attention(q.reshape(b, n, N_KV_HEADS, Q_PER_KV, HEAD_DIM), k, v, p['attn_sink'])
        new_cache = (k, v, jnp.stack([hf[:, -1], hb[:, 0]], axis=1).astype(xn.dtype))
    else:
        k_ctx, v_ctx, st = ctx
        lru, _, _ = rglru_branch(ux, uy, p, st[:, 0], st[:, 1])
        qr = rope_2d(q).reshape(b, n, N_KV_HEADS, Q_PER_KV, HEAD_DIM)
        att = latent_attention(qr, rope_2d(k), v, k_ctx, v_ctx, p['attn_sink'])
        new_cache = None
    conv = conformer_branch(uc, p)
    pool = pool_branch(up, p)
    branches = jnp.stack([lru, conv, att, pool], axis=2)
    proj = jnp.einsum('bnjc,jcd->bnjd', branches, p['w_branch'])
    gates = jax.nn.sigmoid(jnp.dot(xn, p['w_gate']) + p['b_gate']).reshape(b, n, N_BRANCH, D_MODEL)
    merged = jnp.sum(gates * proj, axis=2)
    return jnp.dot(merged, p['w_out']), new_cache


def moe(x, p):
    shp = x.shape
    xt = x.reshape(-1, D_MODEL)
    t = xt.shape[0]
    logits = jnp.dot(xt, p['router_w'], preferred_element_type=jnp.float32) + p['router_b'].astype(jnp.float32)
    top_v, top_i = lax.top_k(logits, TOP_K)
    top_w = jax.nn.softmax(top_v, axis=-1)
    n_assign = t * TOP_K
    flat_e = top_i.reshape(-1)
    order = jnp.argsort(flat_e)
    sorted_e = flat_e[order]
    sorted_tok = (order // TOP_K).astype(jnp.int32)
    sorted_w = top_w.reshape(-1)[order]
    counts = jnp.bincount(flat_e, length=N_EXPERTS)
    padded = (counts + MOE_BLOCK - 1) // MOE_BLOCK * MOE_BLOCK
    start = jnp.cumsum(counts) - counts
    pend = jnp.cumsum(padded)
    pstart = pend - padded
    dest = pstart[sorted_e] + jnp.arange(n_assign) - start[sorted_e]
    n_blocks = -(-n_assign // MOE_BLOCK) + N_EXPERTS
    cap = n_blocks * MOE_BLOCK
    buf_tok = jnp.full((cap,), t, jnp.int32).at[dest].set(sorted_tok)
    buf_w = jnp.zeros((cap,), jnp.float32).at[dest].set(sorted_w)
    block_e = jnp.clip(jnp.searchsorted(pend, jnp.arange(n_blocks) * MOE_BLOCK, side='right'), 0, N_EXPERTS - 1)
    x_pad = jnp.concatenate([xt, jnp.zeros((1, D_MODEL), xt.dtype)], axis=0)
    xb = x_pad[buf_tok].reshape(n_blocks, MOE_BLOCK, D_MODEL)
    w_gu, b_gu, w_dn, b_dn = p['exp_w_gu'], p['exp_b_gu'], p['exp_w_down'], p['exp_b_down']

    def expert_block(args):
        xi, e = args
        h = jnp.dot(xi, w_gu[e]) + b_gu[e]
        gate = jnp.minimum(h[:, :D_EXPERT], SWIGLU_LIMIT)
        lin = jnp.clip(h[:, D_EXPERT:], -SWIGLU_LIMIT, SWIGLU_LIMIT)
        act = (lin + 1.0) * (gate * jax.nn.sigmoid(SWIGLU_ALPHA * gate))
        return jnp.dot(act, w_dn[e]) + b_dn[e]

    yb = lax.map(expert_block, (xb, block_e)).reshape(cap, D_MODEL)
    y = jnp.zeros((t + 1, D_MODEL), jnp.float32).at[buf_tok].add(yb.astype(jnp.float32) * buf_w[:, None])
    return y[:t].astype(x.dtype).reshape(shp)


def modulation(cond, w_mod, b_mod):
    m = jnp.dot(jax.nn.silu(cond), w_mod) + b_mod
    return [m_i[:, None, :] for m_i in jnp.split(m, 6, axis=-1)]


def trunk_layer(h, cond, p, ctx):
    sh1, sc1, g1, sh2, sc2, g2 = modulation(cond, p['w_mod'], p['b_mod'])
    xn = rms_norm(h, p['norm1_g']) * (1.0 + sc1) + sh1
    mix, new_cache = token_mixers(xn, p, ctx)
    h = h + g1 * mix
    xn = rms_norm(h, p['norm2_g']) * (1.0 + sc2) + sh2
    h = h + g2 * moe(xn, p)
    return h, new_cache


def setup_inputs(seed: int = 0) -> dict:
    key = jax.random.key(seed)
    keys = iter(jax.random.split(key, 48))

    def nrm(shape, scale):
        return jax.random.normal(next(keys), shape, jnp.float32) * scale

    def gain(shape):
        return 1.0 + nrm(shape, 0.1)

    L = DEPTH
    a_init = jax.random.uniform(next(keys), (L, 2, LRU_W), jnp.float32, 0.9, 0.999) ** (1.0 / LRU_C)
    return {
        'x_prompt': nrm((BATCH, SEQ, D_MODEL), 1.0),
        'x_sample': nrm((DEC_BATCH, DEC_SEQ, D_MODEL), 1.0),
        'cache_k': nrm((DEC_BATCH, L, PAST_LEN, N_KV_HEADS, HEAD_DIM), 1.0),
        'cache_v': nrm((DEC_BATCH, L, PAST_LEN, N_KV_HEADS, HEAD_DIM), 1.0),
        'state_lru': nrm((DEC_BATCH, L, 2, LRU_W), 0.5),
        'c': nrm((DEC_BATCH, D_MODEL), 1.0),
        'c_ctx': nrm((D_MODEL,), 1.0),
        'norm1_g': gain((L, D_MODEL)),
        'norm2_g': gain((L, D_MODEL)),
        'w_mod': nrm((L, D_MODEL, 6 * D_MODEL), 0.5 * D_MODEL ** -0.5),
        'b_mod': nrm((L, 6 * D_MODEL), 0.02),
        'w_in': nrm((L, D_MODEL, IN_W), D_MODEL ** -0.5),
        'lru_conv_w': nrm((L, LRU_CONV, LRU_W), LRU_CONV ** -0.5),
        'lru_conv_b': nrm((L, LRU_W), 0.02),
        'lru_w_r': nrm((L, 2, LRU_BLOCKS, LRU_BW, LRU_BW), LRU_BW ** -0.5),
        'lru_b_r': nrm((L, 2, LRU_W), 0.02),
        'lru_w_i': nrm((L, 2, LRU_BLOCKS, LRU_BW, LRU_BW), LRU_BW ** -0.5),
        'lru_b_i': nrm((L, 2, LRU_W), 0.02),
        'lru_lambda': jnp.log(a_init) - jnp.log1p(-a_init),
        'conv_dw_w': nrm((L, CONV_K, CONV_W), CONV_K ** -0.5),
        'conv_dw_b': nrm((L, CONV_W), 0.02),
        'conv_ln_g': gain((L, CONV_W)),
        'conv_ln_b': nrm((L, CONV_W), 0.02),
        'q_norm_g': gain((L, HEAD_DIM)),
        'k_norm_g': gain((L, HEAD_DIM)),
        'attn_sink': nrm((L, N_HEADS), 0.5),
        'pool_w': nrm((L, len(POOL_SIZES), POOL_G, POOL_G), POOL_G ** -0.5),
        'pool_scale': gain((L, POOL_W)),
        'w_branch': nrm((L, N_BRANCH, BR_W, D_MODEL), BR_W ** -0.5),
        'w_gate': nrm((L, D_MODEL, N_BRANCH * D_MODEL), D_MODEL ** -0.5),
        'b_gate': nrm((L, N_BRANCH * D_MODEL), 0.02),
        'w_out': nrm((L, D_MODEL, D_MODEL), D_MODEL ** -0.5),
        'router_w': nrm((L, D_MODEL, N_EXPERTS), D_MODEL ** -0.5),
        'router_b': nrm((L, N_EXPERTS), 0.01),
        'exp_w_gu': nrm((L, N_EXPERTS, D_MODEL, 2 * D_EXPERT), D_MODEL ** -0.5),
        'exp_b_gu': nrm((L, N_EXPERTS, 2 * D_EXPERT), 0.02),
        'exp_w_down': nrm((L, N_EXPERTS, D_EXPERT, D_MODEL), D_EXPERT ** -0.5),
        'exp_b_down': nrm((L, N_EXPERTS, D_MODEL), 0.02),
    }


def reference(x_prompt, x_sample, cache_k, cache_v, state_lru, c, c_ctx, norm1_g, norm2_g, w_mod, b_mod, w_in,
              lru_conv_w, lru_conv_b, lru_w_r, lru_b_r, lru_w_i, lru_b_i, lru_lambda, conv_dw_w, conv_dw_b,
              conv_ln_g, conv_ln_b, q_norm_g, k_norm_g, attn_sink, pool_w, pool_scale, w_branch, w_gate, b_gate,
              w_out, router_w, router_b, exp_w_gu, exp_b_gu, exp_w_down, exp_b_down):
    def layer_params(l):
        return {
            'norm1_g': norm1_g[l], 'norm2_g': norm2_g[l], 'w_mod': w_mod[l], 'b_mod': b_mod[l], 'w_in': w_in[l],
            'lru_conv_w': lru_conv_w[l], 'lru_conv_b': lru_conv_b[l], 'lru_w_r': lru_w_r[l], 'lru_b_r': lru_b_r[l],
            'lru_w_i': lru_w_i[l], 'lru_b_i': lru_b_i[l], 'lru_lambda': lru_lambda[l],
            'conv_dw_w': conv_dw_w[l], 'conv_dw_b': conv_dw_b[l], 'conv_ln_g': conv_ln_g[l], 'conv_ln_b': conv_ln_b[l],
            'q_norm_g': q_norm_g[l], 'k_norm_g': k_norm_g[l], 'attn_sink': attn_sink[l],
            'pool_w': pool_w[l], 'pool_scale': pool_scale[l], 'w_branch': w_branch[l], 'w_gate': w_gate[l],
            'b_gate': b_gate[l], 'w_out': w_out[l], 'router_w': router_w[l], 'router_b': router_b[l],
            'exp_w_gu': exp_w_gu[l], 'exp_b_gu': exp_b_gu[l], 'exp_w_down': exp_w_down[l], 'exp_b_down': exp_b_down[l],
        }

    h = x_prompt
    cond_ctx = c_ctx[None, :]
    ks, vs, ss = [], [], []
    for l in range(DEPTH):
        h, (k_l, v_l, s_l) = trunk_layer(h, cond_ctx, layer_params(l), None)
        ks.append(k_l)
        vs.append(v_l)
        ss.append(s_l)
    y_prompt = h
    new_k = jnp.stack(ks, axis=1)
    new_v = jnp.stack(vs, axis=1)
    new_state_lru = jnp.stack(ss, axis=1)

    h = x_sample
    for l in range(DEPTH):
        h, _ = trunk_layer(h, c, layer_params(l), (cache_k[:, l], cache_v[:, l], state_lru[:, l]))
    y_sample = h
    return (y_prompt, y_sample, new_k, new_v, new_state_lru)
```

```python
import functools

import jax
import jax.numpy as jnp
from jax import lax
from jax.experimental import pallas as pl
from jax.experimental.pallas import tpu as pltpu

f32 = jnp.float32
bf16 = jnp.bfloat16
i32 = jnp.int32

D = 1024
N_CTX_B, N_CTX = 16, 256
N_LAT_B, N_LAT = 8, 1024
T_CTX = N_CTX_B * N_CTX
T_LAT = N_LAT_B * N_LAT
T = T_CTX + T_LAT
DEPTH = 2
PAST = 512
GRID_W = 64
IN_W = 6656
COL_UX, COL_UY, COL_UC, COL_UQ, COL_UK, COL_UV, COL_UP = 0, 1024, 2048, 4096, 5120, 5376, 5632
LRU_BW = 64
LRU_CONV = 4
LRU_C = 8.0
CONV_K = 31
N_HEADS, N_KV, HEAD_DIM = 16, 4, 64
POOL_SIZES = (2, 4, 8, 16)
N_EXPERTS, TOP_K = 32, 4
SWIGLU_LIMIT, SWIGLU_ALPHA = 7.0, 1.702
EPS = 1e-6
NEG_INF = -1e30
ROPE_BASE = 10000.0

LANES = 128
SUBLANES = 8
VMEM_LIMIT = 56 * 1024 * 1024

CW = 256
RC = 128
TQ = 128
TM = 256
MOE_BM = 256
N_ASSIGN = T * TOP_K
MOE_NBLK = N_ASSIGN // MOE_BM + N_EXPERTS
MOE_CAP = MOE_NBLK * MOE_BM


def _sds(shape, dt=f32):
    return jax.ShapeDtypeStruct(shape, dt)


def _cparams(sem, vmem=VMEM_LIMIT):
    return pltpu.CompilerParams(dimension_semantics=sem, vmem_limit_bytes=vmem)


def _sigmoid(x):
    return 1.0 / (1.0 + jnp.exp(-x))


def _log1p(z):
    u = 1.0 + z
    d = u - 1.0
    return jnp.where(d == 0.0, z, jnp.log(u) * (z / jnp.where(d == 0.0, 1.0, d)))


def _cast_kernel(x_ref, o_ref):
    o_ref[...] = x_ref[...].astype(o_ref.dtype)


def cast_bf16(w):
    r, c = w.shape
    tr = 256 if r % 256 == 0 else r
    return pl.pallas_call(
        _cast_kernel, out_shape=_sds((r, c), bf16), grid=(r // tr,),
        in_specs=[pl.BlockSpec((tr, c), lambda i: (i, 0))],
        out_specs=pl.BlockSpec((tr, c), lambda i: (i, 0)),
        compiler_params=_cparams(("parallel",)), name="cast_bf16")(w)


def _mod_kernel(c_ref, w_ref, b_ref, o_ref):
    x = c_ref[...]
    s = x * _sigmoid(x)
    o_ref[0] = jnp.dot(s.astype(bf16), w_ref[0].astype(bf16), preferred_element_type=f32) + b_ref[0]


def modulation(cond, w_mod, b_mod):
    r = cond.shape[0]
    tn = 1536
    return pl.pallas_call(
        _mod_kernel, out_shape=_sds((DEPTH, r, 6 * D)), grid=(DEPTH, 6 * D // tn),
        in_specs=[pl.BlockSpec((r, D), lambda l, j: (0, 0)),
                  pl.BlockSpec((1, D, tn), lambda l, j: (l, 0, j)),
                  pl.BlockSpec((1, 1, tn), lambda l, j: (l, 0, j))],
        out_specs=pl.BlockSpec((1, r, tn), lambda l, j: (l, 0, j)),
        compiler_params=_cparams(("parallel", "parallel")), name="modulation")(
            cond, w_mod, b_mod.reshape(DEPTH, 1, 6 * D))


def _rms_mod(x, g, shift, scale):
    ms = jnp.mean(x * x, axis=-1, keepdims=True)
    return (x * lax.rsqrt(ms + EPS) * g) * (1.0 + scale) + shift


def _norm1_kernel(h_ref, g_ref, mod_ref, o_ref):
    m = mod_ref[0]
    o_ref[...] = _rms_mod(h_ref[...], g_ref[...], m[0:1, :], m[1:2, :]).astype(bf16)


def norm1(h, g, mod_tiles):
    nt = h.shape[0] // TM
    return pl.pallas_call(
        _norm1_kernel, out_shape=_sds(h.shape, bf16), grid=(nt,),
        in_specs=[pl.BlockSpec((TM, D), lambda i: (i, 0)),
                  pl.BlockSpec((1, D), lambda i: (0, 0)),
                  pl.BlockSpec((1, 6, D), lambda i: (i, 0, 0))],
        out_specs=pl.BlockSpec((TM, D), lambda i: (i, 0)),
        compiler_params=_cparams(("parallel",)), name="norm1")(h, g.reshape(1, D), mod_tiles)


def _matmul_kernel(x_ref, w_ref, o_ref):
    o_ref[...] = jnp.dot(x_ref[...], w_ref[...], preferred_element_type=f32)


def in_proj(xn, w):
    t = xn.shape[0]
    tm, tn = 1024, 1664
    return pl.pallas_call(
        _matmul_kernel, out_shape=_sds((t, IN_W)), grid=(t // tm, IN_W // tn),
        in_specs=[pl.BlockSpec((tm, D), lambda i, j: (i, 0)),
                  pl.BlockSpec((D, tn), lambda i, j: (0, j))],
        out_specs=pl.BlockSpec((tm, tn), lambda i, j: (i, j)),
        compiler_params=_cparams(("parallel", "parallel")), name="in_proj")(xn, w)


def _gelu_tanh(x):
    return 0.5 * x * (1.0 + jnp.tanh(0.7978845608028654 * (x + 0.044715 * (x * x * x))))


def _lru_kernel(ux_ref, uy_ref, cw_ref, cb_ref, wr_ref, wi_ref, br_ref, bi_ref, lam_ref, h0_ref,
                out_ref, st_ref, xp_ref, a_ref, b_ref, *, n):
    pad = SUBLANES
    xp_ref[pl.ds(0, pad), :] = jnp.zeros((pad, CW), f32)
    xp_ref[pl.ds(n + pad, pad), :] = jnp.zeros((pad, CW), f32)
    xp_ref[pl.ds(pad, n), :] = ux_ref[...]
    left = LRU_CONV // 2
    lam = lam_ref[...]
    neg_c_sp = -LRU_C * (jnp.maximum(-lam, 0.0) + _log1p(jnp.exp(-jnp.abs(lam))))

    def coef_body(i, c):
        r0 = pl.multiple_of(i * RC, RC)
        w = xp_ref[pl.ds(r0, RC + 2 * pad), :]
        xc = jnp.zeros((RC, CW), f32) + cb_ref[...]
        for k in range(LRU_CONV):
            off = pad - left + k
            xc = xc + cw_ref[k:k + 1, :] * w[off:off + RC, :]
        xcb = xc.astype(bf16)
        for d in range(2):
            r = _sigmoid(jnp.dot(xcb, wr_ref[d, 0], preferred_element_type=f32) + br_ref[d:d + 1, :])
            g = _sigmoid(jnp.dot(xcb, wi_ref[d, 0], preferred_element_type=f32) + bi_ref[d:d + 1, :])
            log_a = r * neg_c_sp[d:d + 1, :]
            a = jnp.exp(log_a)
            th = jnp.tanh(log_a)
            one_m_a2 = (-2.0 * th) / (1.0 - th)
            a_ref[d, pl.ds(r0, RC), :] = a
            b_ref[d, pl.ds(r0, RC), :] = jnp.sqrt(one_m_a2) * (g * xc)
        return c

    lax.fori_loop(0, n // RC, coef_body, 0)

    row = lax.broadcasted_iota(i32, (SUBLANES, CW), 0)
    nchunk = n // SUBLANES

    def scan_body(j, carry):
        hf, hb = carry
        rf = pl.multiple_of(j * SUBLANES, SUBLANES)
        a = a_ref[0, pl.ds(rf, SUBLANES), :]
        b = b_ref[0, pl.ds(rf, SUBLANES), :]
        for sh in (1, 2, 4):
            a_s = jnp.where(row >= sh, pltpu.roll(a, sh, axis=0), 1.0)
            b_s = jnp.where(row >= sh, pltpu.roll(b, sh, axis=0), 0.0)
            b = a * b_s + b
            a = a * a_s
        h = a * hf + b
        b_ref[0, pl.ds(rf, SUBLANES), :] = h
        hf = h[SUBLANES - 1:SUBLANES, :]
        rb = pl.multiple_of((nchunk - 1 - j) * SUBLANES, SUBLANES)
        a = a_ref[1, pl.ds(rb, SUBLANES), :]
        b = b_ref[1, pl.ds(rb, SUBLANES), :]
        for sh in (1, 2, 4):
            keep = row < SUBLANES - sh
            a_s = jnp.where(keep, pltpu.roll(a, SUBLANES - sh, axis=0), 1.0)
            b_s = jnp.where(keep, pltpu.roll(b, SUBLANES - sh, axis=0), 0.0)
            b = a * b_s + b
            a = a * a_s
        h = a * hb + b
        b_ref[1, pl.ds(rb, SUBLANES), :] = h
        hb = h[0:1, :]
        return hf, hb

    h0 = h0_ref[0]
    hf, hb = lax.fori_loop(0, nchunk, scan_body, (h0[0:1, :], h0[1:2, :]))
    st_ref[0, 0:1, :] = hf
    st_ref[0, 1:2, :] = hb

    def out_body(i, c):
        r0 = pl.multiple_of(i * RC, RC)
        hsum = b_ref[0, pl.ds(r0, RC), :] + b_ref[1, pl.ds(r0, RC), :]
        out_ref[pl.ds(r0, RC), :] = hsum * _gelu_tanh(uy_ref[pl.ds(r0, RC), :])
        return c

    lax.fori_loop(0, n // RC, out_body, 0)


def lru_branch(u, p, h0, nb, n, row_blk0, t_out):
    nct = D // CW
    cu = COL_UY // CW
    kern = functools.partial(_lru_kernel, n=n)
    vec = lambda a: pl.BlockSpec((a, CW), lambda b, c: (0, c))
    return pl.pallas_call(
        kern, out_shape=(_sds((t_out, D)), _sds((nb, 2, D))), grid=(nb, nct),
        in_specs=[pl.BlockSpec((n, CW), lambda b, c: (row_blk0 + b, c)),
                  pl.BlockSpec((n, CW), lambda b, c: (row_blk0 + b, cu + c)),
                  vec(LRU_CONV), vec(1),
                  pl.BlockSpec((2, 1, CW, CW), lambda b, c: (0, c, 0, 0)),
                  pl.BlockSpec((2, 1, CW, CW), lambda b, c: (0, c, 0, 0)),
                  vec(2), vec(2), vec(2),
                  pl.BlockSpec((1, 2, CW), lambda b, c: (b, 0, c))],
        out_specs=(pl.BlockSpec((n, CW), lambda b, c: (b, c)),
                   pl.BlockSpec((1, 2, CW), lambda b, c: (b, 0, c))),
        scratch_shapes=[pltpu.VMEM((n + 2 * SUBLANES, CW), f32),
                        pltpu.VMEM((2, n, CW), f32), pltpu.VMEM((2, n, CW), f32)],
        compiler_params=_cparams(("parallel", "parallel")), name="lru_branch")(
            u, u, p["lru_conv_w"], p["lru_conv_b"].reshape(1, D), p["lru_wr_bd"], p["lru_wi_bd"],
            p["lru_b_r"], p["lru_b_i"], p["lru_lambda"], h0)


def _conf_kernel(ua_ref, ug_ref, w_ref, b_ref, o_ref, gp_ref, *, n):
    pad = 2 * SUBLANES
    left = CONV_K // 2
    gp_ref[pl.ds(0, pad), :] = jnp.zeros((pad, CW), f32)
    gp_ref[pl.ds(n + pad, pad), :] = jnp.zeros((pad, CW), f32)
    gp_ref[pl.ds(pad, n), :] = ua_ref[...] * _sigmoid(ug_ref[...])

    def body(i, c):
        r0 = pl.multiple_of(i * RC, RC)
        w = gp_ref[pl.ds(r0, RC + 2 * pad), :]
        acc = jnp.zeros((RC, CW), f32) + b_ref[...]
        for k in range(CONV_K):
            off = pad - left + k
            acc = acc + w_ref[k:k + 1, :] * w[off:off + RC, :]
        o_ref[pl.ds(r0, RC), :] = acc
        return c

    lax.fori_loop(0, n // RC, body, 0)


def conf_branch(u, p, nb, n, row_blk0, t_out):
    nct = D // CW
    ca, cg = COL_UC // CW, (COL_UC + D) // CW
    kern = functools.partial(_conf_kernel, n=n)
    return pl.pallas_call(
        kern, out_shape=_sds((t_out, D)), grid=(nb, nct),
        in_specs=[pl.BlockSpec((n, CW), lambda b, c: (row_blk0 + b, ca + c)),
                  pl.BlockSpec((n, CW), lambda b, c: (row_blk0 + b, cg + c)),
                  pl.BlockSpec((CONV_K, CW), lambda b, c: (0, c)),
                  pl.BlockSpec((1, CW), lambda b, c: (0, c))],
        out_specs=pl.BlockSpec((n, CW), lambda b, c: (b, c)),
        scratch_shapes=[pltpu.VMEM((n + 4 * SUBLANES, CW), f32)],
        compiler_params=_cparams(("parallel", "parallel")), name="conf_branch")(
            u, u, p["conv_dw_w"], p["conv_dw_b"].reshape(1, D))


def _pool_kernel(up_ref, w_ref, s_ref, o_ref, xp_ref, *, n):
    pad = SUBLANES
    gi = pl.program_id(1)
    half = jnp.left_shift(1, gi)
    xp_ref[pl.ds(0, pad), :] = jnp.zeros((pad, CW), f32)
    xp_ref[pl.ds(n + pad, pad), :] = jnp.zeros((pad, CW), f32)
    xp_ref[pl.ds(pad, n), :] = up_ref[...]
    wn = RC + 2 * pad

    def body(i, c):
        r0 = pl.multiple_of(i * RC, RC)
        w = xp_ref[pl.ds(r0, wn), :]
        s2 = w + pltpu.roll(w, 1, axis=0)
        s4 = pltpu.roll(s2, 1, axis=0) + pltpu.roll(s2, wn - 1, axis=0)
        s8 = pltpu.roll(s4, 2, axis=0) + pltpu.roll(s4, wn - 2, axis=0)
        s16 = pltpu.roll(s8, 4, axis=0) + pltpu.roll(s8, wn - 4, axis=0)
        s = jnp.where(gi == 0, s2, jnp.where(gi == 1, s4, jnp.where(gi == 2, s8, s16)))[pad:pad + RC, :]
        t = r0 + lax.broadcasted_iota(i32, (RC, CW), 0)
        cnt = (jnp.minimum(t + half, n) - jnp.maximum(t - half, 0)).astype(f32)
        pooled = s / cnt - w[pad:pad + RC, :]
        o_ref[pl.ds(r0, RC), :] = jnp.dot(pooled.astype(bf16), w_ref[0], preferred_element_type=f32) * s_ref[...]
        return c

    lax.fori_loop(0, n // RC, body, 0)


def pool_branch(u, p, nb, n, row_blk0, t_out):
    assert POOL_SIZES == (2, 4, 8, 16) and D // len(POOL_SIZES) == CW
    cp = COL_UP // CW
    kern = functools.partial(_pool_kernel, n=n)
    return pl.pallas_call(
        kern, out_shape=_sds((t_out, D)), grid=(nb, len(POOL_SIZES)),
        in_specs=[pl.BlockSpec((n, CW), lambda b, c: (row_blk0 + b, cp + c)),
                  pl.BlockSpec((1, CW, CW), lambda b, c: (c, 0, 0)),
                  pl.BlockSpec((1, CW), lambda b, c: (0, c))],
        out_specs=pl.BlockSpec((n, CW), lambda b, c: (b, c)),
        scratch_shapes=[pltpu.VMEM((n + 2 * SUBLANES, CW), f32)],
        compiler_params=_cparams(("parallel", "parallel")), name="pool_branch")(
            u, p["pool_w_bf"], p["pool_scale"].reshape(1, D))


def _head_norm(blk, g128, lane_lo):
    sq = blk * blk
    s_lo = jnp.sum(jnp.where(lane_lo, sq, 0.0), axis=-1, keepdims=True)
    s_hi = jnp.sum(jnp.where(lane_lo, 0.0, sq), axis=-1, keepdims=True)
    r = jnp.where(lane_lo, lax.rsqrt(s_lo * (1.0 / HEAD_DIM) + EPS), lax.rsqrt(s_hi * (1.0 / HEAD_DIM) + EPS))
    return blk * r * g128


def _rope(blk, cos, sin_signed, lane):
    partner = jnp.where((lane % 32) < 16, pltpu.roll(blk, LANES - 16, axis=1), pltpu.roll(blk, 16, axis=1))
    return blk * cos + partner * sin_signed


def _prep_kv(k_ref, v_ref, ks_ref, vs_ref, kg, lane_lo, rope=None, newk_ref=None, newv_ref=None):
    nk = k_ref.shape[0]
    lane = lax.broadcasted_iota(i32, (nk, LANES), 1)
    for tj in range(N_KV * HEAD_DIM // LANES):
        kt = _head_norm(k_ref[:, tj * LANES:(tj + 1) * LANES], kg, lane_lo(nk))
        if rope is not None:
            kt = _rope(kt, rope[0][...], rope[1][...], lane)
        vt = v_ref[:, tj * LANES:(tj + 1) * LANES]
        if newk_ref is not None:
            newk_ref[0, :, tj * LANES:(tj + 1) * LANES] = kt
            newv_ref[0, :, tj * LANES:(tj + 1) * LANES] = vt
        for half in range(2):
            g = 2 * tj + half
            keep = lane_lo(nk) if half == 0 else jnp.logical_not(lane_lo(nk))
            ka = jnp.where(keep, kt, 0.0)
            va = jnp.where(keep, vt, 0.0)
            ks_ref[g, half] = ka.astype(bf16)
            ks_ref[g, 1 - half] = pltpu.roll(ka, HEAD_DIM, axis=1).astype(bf16)
            vs_ref[g, half] = va.astype(bf16)
            vs_ref[g, 1 - half] = pltpu.roll(va, HEAD_DIM, axis=1).astype(bf16)


def _prep_cache(ck_ref, cv_ref, kc_ref, vc_ref, lane_lo):
    nk = ck_ref.shape[2]
    for tj in range(N_KV * HEAD_DIM // LANES):
        kt = ck_ref[0, 0, :, tj * LANES:(tj + 1) * LANES]
        vt = cv_ref[0, 0, :, tj * LANES:(tj + 1) * LANES]
        for half in range(2):
            g = 2 * tj + half
            keep = lane_lo(nk) if half == 0 else jnp.logical_not(lane_lo(nk))
            ka = jnp.where(keep, kt, 0.0)
            va = jnp.where(keep, vt, 0.0)
            kc_ref[g, half] = ka.astype(bf16)
            kc_ref[g, 1 - half] = pltpu.roll(ka, HEAD_DIM, axis=1).astype(bf16)
            vc_ref[g, half] = va.astype(bf16)
            vc_ref[g, 1 - half] = pltpu.roll(va, HEAD_DIM, axis=1).astype(bf16)


def _qk(qm, k):
    return lax.dot_general(qm, k, (((1,), (1,)), ((), ())), preferred_element_type=f32)


def _attend(sink_ref, q_tiles, segments, att_ref):
    lane_lo = lax.broadcasted_iota(i32, (TQ, LANES), 1) < HEAD_DIM
    for j in range(N_HEADS // 2):
        out_tile = None
        for half in range(2):
            h = 2 * j + half
            g = h // (N_HEADS // N_KV)
            qm = jnp.where(lane_lo if half == 0 else jnp.logical_not(lane_lo), q_tiles[j], 0.0).astype(bf16)
            sink = sink_ref[h]
            scores = []
            m = jnp.full((TQ, 1), sink, f32)
            for kget, _, mask in segments:
                s = _qk(qm, kget(g, half))
                if mask is not None:
                    s = jnp.where(mask, s, NEG_INF)
                m = jnp.maximum(m, jnp.max(s, axis=-1, keepdims=True))
                scores.append(s)
            den = jnp.exp(sink - m)
            o = jnp.zeros((TQ, LANES), f32)
            for s, (_, vget, _) in zip(scores, segments):
                pr = jnp.exp(s - m)
                den = den + jnp.sum(pr, axis=-1, keepdims=True)
                o = o + jnp.dot(pr.astype(bf16), vget(g, half), preferred_element_type=f32)
            o = o / den
            out_tile = o if out_tile is None else out_tile + o
        att_ref[:, j * LANES:(j + 1) * LANES] = out_tile


def _q_tiles(q_ref, qg, rope=None):
    lane = lax.broadcasted_iota(i32, (TQ, LANES), 1)
    lane_lo = lane < HEAD_DIM
    tiles = []
    for j in range(N_HEADS // 2):
        qt = _head_norm(q_ref[:, j * LANES:(j + 1) * LANES], qg, lane_lo)
        if rope is not None:
            qt = _rope(qt, rope[0], rope[1], lane)
        tiles.append(qt * (HEAD_DIM ** -0.5))
    return tiles


def _lane_lo_fn(nrows):
    return lax.broadcasted_iota(i32, (nrows, LANES), 1) < HEAD_DIM


def _ctx_attn_kernel(sink_ref, q_ref, k_ref, v_ref, qg_ref, kg_ref, att_ref, newk_ref, newv_ref, ks_ref, vs_ref):
    @pl.when(pl.program_id(1) == 0)
    def _():
        _prep_kv(k_ref, v_ref, ks_ref, vs_ref, kg_ref[...], _lane_lo_fn, None, newk_ref, newv_ref)

    tiles = _q_tiles(q_ref, qg_ref[...])
    seg = [(lambda g, half: ks_ref[g, half], lambda g, half: vs_ref[g, half], None)]
    _attend(sink_ref, tiles, seg, att_ref)


def ctx_attention(u, p):
    nq = N_CTX // TQ
    kvw = N_KV * HEAD_DIM
    return pl.pallas_call(
        _ctx_attn_kernel,
        out_shape=(_sds((T_CTX, D)), _sds((N_CTX_B, N_CTX, kvw)), _sds((N_CTX_B, N_CTX, kvw))),
        grid=(N_CTX_B, nq),
        in_specs=[pl.BlockSpec(memory_space=pltpu.SMEM),
                  pl.BlockSpec((TQ, D), lambda b, i: (b * nq + i, COL_UQ // D)),
                  pl.BlockSpec((N_CTX, kvw), lambda b, i: (b, COL_UK // kvw)),
                  pl.BlockSpec((N_CTX, kvw), lambda b, i: (b, COL_UV // kvw)),
                  pl.BlockSpec((1, LANES), lambda b, i: (0, 0)),
                  pl.BlockSpec((1, LANES), lambda b, i: (0, 0))],
        out_specs=(pl.BlockSpec((TQ, D), lambda b, i: (b * nq + i, 0)),
                   pl.BlockSpec((1, N_CTX, kvw), lambda b, i: (b, 0, 0)),
                   pl.BlockSpec((1, N_CTX, kvw), lambda b, i: (b, 0, 0))),
        scratch_shapes=[pltpu.VMEM((N_KV, 2, N_CTX, LANES), bf16), pltpu.VMEM((N_KV, 2, N_CTX, LANES), bf16)],
        compiler_params=_cparams(("parallel", "arbitrary")), name="ctx_attention")(
            p["attn_sink"], u, u, u, p["q_norm_g2"], p["k_norm_g2"])


def _lat_attn_kernel(sink_ref, q_ref, k_ref, v_ref, ck_ref, cv_ref, cosq_ref, sinq_ref, cosk_ref, sink_tab_ref,
                     qg_ref, kg_ref, att_ref, ks_ref, vs_ref, kc_ref, vc_ref):
    i = pl.program_id(1)
    nq = N_LAT // TQ

    @pl.when(i == 0)
    def _():
        _prep_kv(k_ref, v_ref, ks_ref, vs_ref, kg_ref[...], _lane_lo_fn, (cosk_ref, sink_tab_ref))
        _prep_cache(ck_ref, cv_ref, kc_ref, vc_ref, _lane_lo_fn)

    tiles = _q_tiles(q_ref, qg_ref[...], (cosq_ref[...], sinq_ref[...]))
    r = lax.broadcasted_iota(i32, (TQ, TQ), 0)
    c = lax.broadcasted_iota(i32, (TQ, TQ), 1)
    prev0 = pl.multiple_of(jnp.maximum(i - 1, 0) * TQ, TQ)
    cur0 = pl.multiple_of(i * TQ, TQ)
    next0 = pl.multiple_of(jnp.minimum(i + 1, nq - 1) * TQ, TQ)
    mask_prev = jnp.logical_and(c >= r, i > 0)
    mask_next = jnp.logical_and(c <= r, i < nq - 1)

    def seg(r0, mask):
        return (lambda g, half: ks_ref[g, half, pl.ds(r0, TQ), :],
                lambda g, half: vs_ref[g, half, pl.ds(r0, TQ), :], mask)

    segments = [seg(prev0, mask_prev), seg(cur0, None), seg(next0, mask_next),
                (lambda g, half: kc_ref[g, half], lambda g, half: vc_ref[g, half], None)]
    _attend(sink_ref, tiles, segments, att_ref)


def lat_attention(u, p, cache_k, cache_v, rope_cos, rope_sin):
    nq = N_LAT // TQ
    kvw = N_KV * HEAD_DIM
    l = p["layer"]
    qblk0 = T_CTX // TQ
    kblk0 = T_CTX // N_LAT
    return pl.pallas_call(
        _lat_attn_kernel, out_shape=_sds((T_LAT, D)), grid=(N_LAT_B, nq),
        in_specs=[pl.BlockSpec(memory_space=pltpu.SMEM),
                  pl.BlockSpec((TQ, D), lambda b, i: (qblk0 + b * nq + i, COL_UQ // D)),
                  pl.BlockSpec((N_LAT, kvw), lambda b, i: (kblk0 + b, COL_UK // kvw)),
                  pl.BlockSpec((N_LAT, kvw), lambda b, i: (kblk0 + b, COL_UV // kvw)),
                  pl.BlockSpec((1, 1, PAST, kvw), lambda b, i: (b, l, 0, 0)),
                  pl.BlockSpec((1, 1, PAST, kvw), lambda b, i: (b, l, 0, 0)),
                  pl.BlockSpec((TQ, LANES), lambda b, i: (i, 0)),
                  pl.BlockSpec((TQ, LANES), lambda b, i: (i, 0)),
                  pl.BlockSpec((N_LAT, LANES), lambda b, i: (0, 0)),
                  pl.BlockSpec((N_LAT, LANES), lambda b, i: (0, 0)),
                  pl.BlockSpec((1, LANES), lambda b, i: (0, 0)),
                  pl.BlockSpec((1, LANES), lambda b, i: (0, 0))],
        out_specs=pl.BlockSpec((TQ, D), lambda b, i: (b * nq + i, 0)),
        scratch_shapes=[pltpu.VMEM((N_KV, 2, N_LAT, LANES), bf16), pltpu.VMEM((N_KV, 2, N_LAT, LANES), bf16),
                        pltpu.VMEM((N_KV, 2, PAST, LANES), bf16), pltpu.VMEM((N_KV, 2, PAST, LANES), bf16)],
        compiler_params=_cparams(("parallel", "arbitrary")), name="lat_attention")(
            p["attn_sink"], u, u, u, cache_k, cache_v, rope_cos, rope_sin, rope_cos, rope_sin,
            p["q_norm_g2"], p["k_norm_g2"])


def rope_tables():
    pos = jnp.arange(N_LAT)
    rows = (pos // GRID_W).astype(f32)
    cols = (pos % GRID_W).astype(f32)
    half = HEAD_DIM // 2
    freqs = ROPE_BASE ** (-jnp.arange(0, half, 2, dtype=f32) / half)
    lane = jnp.arange(LANES)
    within = lane % half
    fidx = within % (half // 2)
    use_cols = (lane % HEAD_DIM) >= half
    ang = jnp.where(use_cols[None, :], cols[:, None], rows[:, None]) * freqs[fidx][None, :]
    sign = jnp.where(within < half // 2, -1.0, 1.0)
    return jnp.cos(ang), jnp.sin(ang) * sign[None, :]


def _merge_kernel(h_ref, xn_ref, lru_ref, conv_ref, att_ref, pool_ref, mod_ref, lng_ref, lnb_ref,
                  wg_ref, bg_ref, wb_ref, wo_ref, o_ref):
    xn = xn_ref[...]
    hc = conv_ref[...]
    mu = jnp.mean(hc, axis=-1, keepdims=True)
    xc = hc - mu
    var = jnp.mean(xc * xc, axis=-1, keepdims=True)
    y = xc * lax.rsqrt(var + EPS) * lng_ref[...] + lnb_ref[...]
    conv = y * _sigmoid(y)
    merged = jnp.zeros((TM, D), f32)
    for j, br in enumerate((lru_ref[...], conv, att_ref[...], pool_ref[...])):
        gate = _sigmoid(jnp.dot(xn, wg_ref[:, j * D:(j + 1) * D], preferred_element_type=f32)
                        + bg_ref[:, j * D:(j + 1) * D])
        proj = jnp.dot(br.astype(bf16), wb_ref[j], preferred_element_type=f32)
        merged = merged + gate * proj
    out = jnp.dot(merged.astype(bf16), wo_ref[...], preferred_element_type=f32)
    o_ref[...] = h_ref[...] + mod_ref[0][2:3, :] * out


def merge(h, xn, lru, conv, att, pool, mod_tiles, p):
    nt = h.shape[0] // TM
    tok = pl.BlockSpec((TM, D), lambda i: (i, 0))
    const = lambda shape: pl.BlockSpec(shape, lambda i: (0,) * len(shape), pipeline_mode=pl.Buffered(1))
    return pl.pallas_call(
        _merge_kernel, out_shape=_sds(h.shape), grid=(nt,),
        in_specs=[tok, tok, tok, tok, tok, tok,
                  pl.BlockSpec((1, 6, D), lambda i: (i, 0, 0)),
                  const((1, D)), const((1, D)),
                  const((D, 4 * D)), const((1, 4 * D)), const((4, D, D)), const((D, D))],
        out_specs=tok,
        compiler_params=_cparams(("parallel",)), name="merge")(
            h, xn, lru, conv, att, pool, mod_tiles, p["conv_ln_g"].reshape(1, D), p["conv_ln_b"].reshape(1, D),
            p["w_gate_bf"], p["b_gate"].reshape(1, 4 * D), p["w_branch_bf"], p["w_out_bf"])


def _router_kernel(h_ref, g_ref, mod_ref, rw_ref, rb_ref, xn_ref, topi_ref, topw_ref, rank_ref, cnt_ref, carry_ref):
    step = pl.program_id(0)

    @pl.when(step == 0)
    def _():
        carry_ref[...] = jnp.zeros_like(carry_ref)

    m = mod_ref[0]
    xn = _rms_mod(h_ref[...], g_ref[...], m[3:4, :], m[4:5, :])
    xn_ref[...] = xn
    logits = jnp.dot(xn, rw_ref[...], preferred_element_type=f32, precision=lax.Precision.HIGHEST) + rb_ref[...]
    lane = lax.broadcasted_iota(i32, (TM, LANES), 1).astype(f32)
    lg = jnp.where(lane < N_EXPERTS, logits, NEG_INF)
    vals, idxs = [], []
    for _ in range(TOP_K):
        mx = jnp.max(lg, axis=-1, keepdims=True)
        idx = jnp.min(jnp.where(lg == mx, lane, float(LANES)), axis=-1, keepdims=True)
        vals.append(mx)
        idxs.append(idx)
        lg = jnp.where(lane == idx, -3e38, lg)
    exps = [jnp.exp(v - vals[0]) for v in vals]
    den = exps[0] + exps[1] + exps[2] + exps[3]
    cnt = jnp.zeros((TM, LANES), f32)
    for idx in idxs:
        cnt = cnt + jnp.where(lane == idx, 1.0, 0.0)
    rr = lax.broadcasted_iota(i32, (TM, TM), 0)
    cc = lax.broadcasted_iota(i32, (TM, TM), 1)
    tri = jnp.where(rr > cc, 1.0, 0.0).astype(bf16)
    before = jnp.dot(tri, cnt.astype(bf16), preferred_element_type=f32) + carry_ref[0:1, :]
    topi = jnp.zeros((TM, LANES), f32)
    topw = jnp.zeros((TM, LANES), f32)
    rank = jnp.zeros((TM, LANES), f32)
    for k in range(TOP_K):
        rk = jnp.sum(jnp.where(lane == idxs[k], before, 0.0), axis=-1, keepdims=True)
        topi = jnp.where(lane == k, idxs[k], topi)
        topw = jnp.where(lane == k, exps[k] / den, topw)
        rank = jnp.where(lane == k, rk, rank)
    topi_ref[...] = topi.astype(i32)
    topw_ref[...] = topw
    rank_ref[...] = rank.astype(i32)
    total = carry_ref[0:1, :] + jnp.sum(cnt, axis=0, keepdims=True)
    carry_ref[...] = jnp.broadcast_to(total, carry_ref.shape)
    cnt_ref[...] = jnp.broadcast_to(total, cnt_ref.shape).astype(i32)


def router(h, mod_tiles, p):
    nt = h.shape[0] // TM
    tok = lambda w, dt=f32: pl.BlockSpec((TM, w), lambda i: (i, 0))
    t = h.shape[0]
    return pl.pallas_call(
        _router_kernel,
        out_shape=(_sds((t, D)), _sds((t, LANES), i32), _sds((t, LANES)), _sds((t, LANES), i32),
                   _sds((SUBLANES, LANES), i32)),
        grid=(nt,),
        in_specs=[tok(D), pl.BlockSpec((1, D), lambda i: (0, 0)), pl.BlockSpec((1, 6, D), lambda i: (i, 0, 0)),
                  pl.BlockSpec((D, LANES), lambda i: (0, 0)), pl.BlockSpec((1, LANES), lambda i: (0, 0))],
        out_specs=(tok(D), tok(LANES), tok(LANES), tok(LANES), pl.BlockSpec((SUBLANES, LANES), lambda i: (0, 0))),
        scratch_shapes=[pltpu.VMEM((SUBLANES, LANES), f32)],
        compiler_params=_cparams(("arbitrary",)), name="router")(
            h, p["norm2_g"].reshape(1, D), mod_tiles, p["router_w_pad"], p["router_b_pad"])


def _dest_kernel(topi_ref, rank_ref, pstart_ref, o_ref):
    lane = lax.broadcasted_iota(i32, (TM, LANES), 1).astype(f32)
    topi = topi_ref[...].astype(f32)
    pstart = pstart_ref[...].astype(f32)
    dest = rank_ref[...].astype(f32)
    for k in range(TOP_K):
        e = jnp.sum(jnp.where(lane == k, topi, 0.0), axis=-1, keepdims=True)
        ps = jnp.sum(jnp.where(lane == e, pstart, 0.0), axis=-1, keepdims=True)
        dest = dest + jnp.where(lane == k, ps, 0.0)
    o_ref[...] = dest.astype(i32)


def dest_rows(topi, rank, pstart):
    t = topi.shape[0]
    tok = pl.BlockSpec((TM, LANES), lambda i: (i, 0))
    return pl.pallas_call(
        _dest_kernel, out_shape=_sds((t, LANES), i32), grid=(t // TM,),
        in_specs=[tok, tok, pl.BlockSpec((1, LANES), lambda i: (0, 0))], out_specs=tok,
        compiler_params=_cparams(("parallel",)), name="dest_rows")(topi, rank, pstart)


def _row_copy(src_ref, s, dst_ref, d, sem):
    return pltpu.make_async_copy(src_ref.at[pl.ds(s, 1), :], dst_ref.at[pl.ds(d, 1), :], sem)


def _dispatch_kernel(dest_ref, x_ref, xb_in_ref, xb_ref, sem):
    del xb_in_ref

    def start(t, c):
        for k in range(TOP_K):
            _row_copy(x_ref, t, xb_ref, dest_ref[t * TOP_K + k], sem).start()
        return c

    lax.fori_loop(0, TM, start, 0)

    def wait(t, c):
        for k in range(TOP_K):
            _row_copy(x_ref, 0, xb_ref, 0, sem).wait()
        return c

    lax.fori_loop(0, TM, wait, 0)


def dispatch(dest_flat, xn, xb_zero):
    t = xn.shape[0]
    return pl.pallas_call(
        _dispatch_kernel, out_shape=_sds(xb_zero.shape), grid=(t // TM,),
        in_specs=[pl.BlockSpec((TM * TOP_K,), lambda i: (i,), memory_space=pltpu.SMEM),
                  pl.BlockSpec((TM, D), lambda i: (i, 0)),
                  pl.BlockSpec(memory_space=pl.ANY)],
        out_specs=pl.BlockSpec(memory_space=pl.ANY),
        scratch_shapes=[pltpu.SemaphoreType.DMA(())],
        input_output_aliases={2: 0},
        compiler_params=_cparams(("arbitrary",)), name="moe_dispatch")(dest_flat, xn, xb_zero)


def _expert_kernel(be_ref, nused_ref, xb_ref, wgu_ref, bgu_ref, wdn_ref, bdn_ref, yb_ref, wgu_bf, wdn_bf):
    i = pl.program_id(0)
    e = be_ref[i]
    prev = be_ref[jnp.maximum(i - 1, 0)]

    @pl.when(jnp.logical_or(i == 0, e != prev))
    def _():
        wgu_bf[...] = wgu_ref[0, 0].astype(bf16)
        wdn_bf[...] = wdn_ref[0, 0].astype(bf16)

    @pl.when(i < nused_ref[0])
    def _():
        x = xb_ref[...].astype(bf16)
        h = jnp.dot(x, wgu_bf[...], preferred_element_type=f32) + bgu_ref[0, 0]
        gate = jnp.minimum(h[:, :D], SWIGLU_LIMIT)
        lin = jnp.clip(h[:, D:], -SWIGLU_LIMIT, SWIGLU_LIMIT)
        act = (lin + 1.0) * (gate * _sigmoid(SWIGLU_ALPHA * gate))
        yb_ref[...] = jnp.dot(act.astype(bf16), wdn_bf[...], preferred_element_type=f32) + bdn_ref[0, 0]

    @pl.when(i >= nused_ref[0])
    def _():
        yb_ref[...] = jnp.zeros_like(yb_ref)


def experts(block_e, nused, xb, p):
    l = p["layer"]
    grid_spec = pltpu.PrefetchScalarGridSpec(
        num_scalar_prefetch=2, grid=(MOE_NBLK,),
        in_specs=[pl.BlockSpec((MOE_BM, D), lambda i, be, nu: (i, 0)),
                  pl.BlockSpec((1, 1, D, 2 * D), lambda i, be, nu: (l, be[i], 0, 0)),
                  pl.BlockSpec((1, 1, 1, 2 * D), lambda i, be, nu: (l, be[i], 0, 0)),
                  pl.BlockSpec((1, 1, D, D), lambda i, be, nu: (l, be[i], 0, 0)),
                  pl.BlockSpec((1, 1, 1, D), lambda i, be, nu: (l, be[i], 0, 0))],
        out_specs=pl.BlockSpec((MOE_BM, D), lambda i, be, nu: (i, 0)),
        scratch_shapes=[pltpu.VMEM((D, 2 * D), bf16), pltpu.VMEM((D, D), bf16)])
    return pl.pallas_call(
        _expert_kernel, out_shape=_sds((MOE_CAP, D)), grid_spec=grid_spec,
        compiler_params=_cparams(("arbitrary",)), name="moe_experts")(
            block_e, nused, xb, p["exp_w_gu"], p["exp_b_gu"].reshape(DEPTH, N_EXPERTS, 1, 2 * D),
            p["exp_w_down"], p["exp_b_down"].reshape(DEPTH, N_EXPERTS, 1, D))


def _combine_kernel(dest_ref, yb_ref, topw_ref, h_ref, mod_ref, o_ref, buf, sem):
    def start(t, c):
        for k in range(TOP_K):
            pltpu.make_async_copy(yb_ref.at[pl.ds(dest_ref[t * TOP_K + k], 1), :], buf.at[k, pl.ds(t, 1), :], sem).start()
        return c

    lax.fori_loop(0, TM, start, 0)

    def wait(t, c):
        for k in range(TOP_K):
            pltpu.make_async_copy(yb_ref.at[pl.ds(0, 1), :], buf.at[k, pl.ds(0, 1), :], sem).wait()
        return c

    lax.fori_loop(0, TM, wait, 0)
    w = topw_ref[...]
    y = jnp.zeros((TM, D), f32)
    for k in range(TOP_K):
        y = y + buf[k] * w[:, k:k + 1]
    o_ref[...] = h_ref[...] + mod_ref[0][5:6, :] * y


def combine(dest_flat, yb, topw, h, mod_tiles):
    t = h.shape[0]
    tok = pl.BlockSpec((TM, D), lambda i: (i, 0))
    return pl.pallas_call(
        _combine_kernel, out_shape=_sds(h.shape), grid=(t // TM,),
        in_specs=[pl.BlockSpec((TM * TOP_K,), lambda i: (i,), memory_space=pltpu.SMEM),
                  pl.BlockSpec(memory_space=pl.ANY),
                  pl.BlockSpec((TM, LANES), lambda i: (i, 0)), tok,
                  pl.BlockSpec((1, 6, D), lambda i: (i, 0, 0))],
        out_specs=tok,
        scratch_shapes=[pltpu.VMEM((TOP_K, TM, D), f32), pltpu.SemaphoreType.DMA(())],
        compiler_params=_cparams(("arbitrary",)), name="moe_combine")(dest_flat, yb, topw, h, mod_tiles)


def moe_layer(h, mod_tiles, p):
    xn, topi, topw, rank, counts = router(h, mod_tiles, p)
    cnt = counts[0, :N_EXPERTS]
    padded = (cnt + MOE_BM - 1) // MOE_BM * MOE_BM
    pend = jnp.cumsum(padded)
    pstart = jnp.zeros((1, LANES), i32).at[0, :N_EXPERTS].set(pend - padded)
    block_e = jnp.clip(jnp.searchsorted(pend, jnp.arange(MOE_NBLK, dtype=i32) * MOE_BM, side="right"),
                       0, N_EXPERTS - 1).astype(i32)
    nused = (pend[-1:] // MOE_BM).astype(i32)
    dest = dest_rows(topi, rank, pstart)
    dest_flat = dest[:, :TOP_K].reshape(-1)
    xb = dispatch(dest_flat, xn, jnp.zeros((MOE_CAP, D), f32))
    yb = experts(block_e, nused, xb, p)
    return combine(dest_flat, yb, topw, h, mod_tiles)


def _block_diag(w):
    per = CW // LRU_BW
    w5 = w.reshape(2, D // CW, per, LRU_BW, LRU_BW)
    eye = jnp.eye(per, dtype=w.dtype)
    bd = w5[:, :, :, :, None, :] * eye[None, None, :, None, :, None]
    return bd.reshape(2, D // CW, CW, CW).astype(bf16)


def _layer_params(l, a):
    tile2 = lambda g: jnp.tile(g, 2).reshape(1, LANES)
    p = {k: v[l] for k, v in a.items() if k not in ("exp_w_gu", "exp_b_gu", "exp_w_down", "exp_b_down")}
    p["layer"] = l
    for k in ("exp_w_gu", "exp_b_gu", "exp_w_down", "exp_b_down"):
        p[k] = a[k]
    p["w_in_bf"] = cast_bf16(p["w_in"])
    p["w_gate_bf"] = cast_bf16(p["w_gate"])
    p["w_branch_bf"] = cast_bf16(p["w_branch"].reshape(4 * D, D)).reshape(4, D, D)
    p["w_out_bf"] = cast_bf16(p["w_out"])
    p["pool_w_bf"] = cast_bf16(p["pool_w"].reshape(D, CW)).reshape(4, CW, CW)
    p["lru_wr_bd"] = _block_diag(p["lru_w_r"])
    p["lru_wi_bd"] = _block_diag(p["lru_w_i"])
    p["q_norm_g2"] = tile2(p["q_norm_g"])
    p["k_norm_g2"] = tile2(p["k_norm_g"])
    p["router_w_pad"] = jnp.pad(p["router_w"], ((0, 0), (0, LANES - N_EXPERTS)))
    p["router_b_pad"] = jnp.pad(p["router_b"], (0, LANES - N_EXPERTS)).reshape(1, LANES)
    return p


def kernel(x_prompt, x_sample, cache_k, cache_v, state_lru, c, c_ctx, norm1_g, norm2_g, w_mod, b_mod, w_in,
           lru_conv_w, lru_conv_b, lru_w_r, lru_b_r, lru_w_i, lru_b_i, lru_lambda, conv_dw_w, conv_dw_b,
           conv_ln_g, conv_ln_b, q_norm_g, k_norm_g, attn_sink, pool_w, pool_scale, w_branch, w_gate, b_gate,
           w_out, router_w, router_b, exp_w_gu, exp_b_gu, exp_w_down, exp_b_down):
    weights = dict(norm1_g=norm1_g, norm2_g=norm2_g, w_in=w_in, lru_conv_w=lru_conv_w, lru_conv_b=lru_conv_b,
                   lru_w_r=lru_w_r, lru_b_r=lru_b_r, lru_w_i=lru_w_i, lru_b_i=lru_b_i, lru_lambda=lru_lambda,
                   conv_dw_w=conv_dw_w, conv_dw_b=conv_dw_b, conv_ln_g=conv_ln_g, conv_ln_b=conv_ln_b,
                   q_norm_g=q_norm_g, k_norm_g=k_norm_g, attn_sink=attn_sink, pool_w=pool_w, pool_scale=pool_scale,
                   w_branch=w_branch, w_gate=w_gate, b_gate=b_gate, w_out=w_out, router_w=router_w,
                   router_b=router_b, exp_w_gu=exp_w_gu, exp_b_gu=exp_b_gu, exp_w_down=exp_w_down,
                   exp_b_down=exp_b_down)
    kvw = N_KV * HEAD_DIM
    n_cond = 2 * SUBLANES
    cond = jnp.concatenate([c_ctx[None, :], c, jnp.zeros((n_cond - 1 - N_LAT_B, D), f32)], axis=0)
    mod = modulation(cond, w_mod, b_mod)
    tile_start = jnp.arange(T // TM) * TM
    tile_row = jnp.where(tile_start < T_CTX, 0, 1 + (tile_start - T_CTX) // N_LAT)
    h = jnp.concatenate([x_prompt.reshape(T_CTX, D), x_sample.reshape(T_LAT, D)], axis=0)
    ck = cache_k.reshape(N_LAT_B, DEPTH, PAST, kvw)
    cv = cache_v.reshape(N_LAT_B, DEPTH, PAST, kvw)
    rope_cos, rope_sin = rope_tables()
    zero_state = jnp.zeros((N_CTX_B, 2, D), f32)
    new_k, new_v, new_s = [], [], []
    for l in range(DEPTH):
        p = _layer_params(l, weights)
        mod_tiles = mod[l][tile_row].reshape(T // TM, 6, D)
        xn = norm1(h, p["norm1_g"], mod_tiles)
        u = in_proj(xn, p["w_in_bf"])
        lru_c, st_c = lru_branch(u, p, zero_state, N_CTX_B, N_CTX, 0, T_CTX)
        lru_l, _ = lru_branch(u, p, state_lru[:, l], N_LAT_B, N_LAT, T_CTX // N_LAT, T_LAT)
        conv_c = conf_branch(u, p, N_CTX_B, N_CTX, 0, T_CTX)
        conv_l = conf_branch(u, p, N_LAT_B, N_LAT, T_CTX // N_LAT, T_LAT)
        pool_c = pool_branch(u, p, N_CTX_B, N_CTX, 0, T_CTX)
        pool_l = pool_branch(u, p, N_LAT_B, N_LAT, T_CTX // N_LAT, T_LAT)
        att_c, k_l, v_l = ctx_attention(u, p)
        att_l = lat_attention(u, p, ck, cv, rope_cos, rope_sin)
        cat = lambda a, b: jnp.concatenate([a, b], axis=0)
        h = merge(h, xn, cat(lru_c, lru_l), cat(conv_c, conv_l), cat(att_c, att_l), cat(pool_c, pool_l),
                  mod_tiles, p)
        h = moe_layer(h, mod_tiles, p)
        new_k.append(k_l.reshape(N_CTX_B, N_CTX, N_KV, HEAD_DIM))
        new_v.append(v_l.reshape(N_CTX_B, N_CTX, N_KV, HEAD_DIM))
        new_s.append(st_c)
    y_prompt = h[:T_CTX].reshape(N_CTX_B, N_CTX, D)
    y_sample = h[T_CTX:].reshape(N_LAT_B, N_LAT, D)
    return (y_prompt, y_sample, jnp.stack(new_k, axis=1), jnp.stack(new_v, axis=1), jnp.stack(new_s, axis=1))
```

```python
import functools

import jax
import jax.numpy as jnp
from jax import lax
from jax.experimental import pallas as pl
from jax.experimental.pallas import tpu as pltpu

f32 = jnp.float32
bf16 = jnp.bfloat16
i32 = jnp.int32

D = 1024
N_CTX_B, N_CTX = 16, 256
N_LAT_B, N_LAT = 8, 1024
T_CTX = N_CTX_B * N_CTX
T_LAT = N_LAT_B * N_LAT
T = T_CTX + T_LAT
DEPTH = 2
PAST = 512
GRID_W = 64
IN_W = 6656
COL_UX, COL_UY, COL_UC, COL_UQ, COL_UK, COL_UV, COL_UP = 0, 1024, 2048, 4096, 5120, 5376, 5632
LRU_BW = 64
LRU_CONV = 4
LRU_C = 8.0
CONV_K = 31
N_HEADS, N_KV, HEAD_DIM = 16, 4, 64
POOL_SIZES = (2, 4, 8, 16)
N_EXPERTS, TOP_K = 32, 4
SWIGLU_LIMIT, SWIGLU_ALPHA = 7.0, 1.702
EPS = 1e-6
NEG_INF = -1e30
ROPE_BASE = 10000.0

LANES = 128
SUBLANES = 8
VMEM_LIMIT = 56 * 1024 * 1024

CW = 256
RC = 128
TQ = 128
TM = 256
MOE_BM = 256
N_ASSIGN = T * TOP_K
MOE_NBLK = N_ASSIGN // MOE_BM + N_EXPERTS
MOE_CAP = MOE_NBLK * MOE_BM


def _sds(shape, dt=f32):
    return jax.ShapeDtypeStruct(shape, dt)


def _cparams(sem, vmem=VMEM_LIMIT):
    return pltpu.CompilerParams(dimension_semantics=sem, vmem_limit_bytes=vmem)


def _sigmoid(x):
    return 1.0 / (1.0 + jnp.exp(-x))


def _log1p(z):
    u = 1.0 + z
    d = u - 1.0
    return jnp.where(d == 0.0, z, jnp.log(u) * (z / jnp.where(d == 0.0, 1.0, d)))


def _cast_kernel(x_ref, o_ref):
    o_ref[...] = x_ref[...].astype(o_ref.dtype)


def cast_bf16(w, l):
    _, r, c = w.shape
    tr = 256
    return pl.pallas_call(
        _cast_kernel, out_shape=_sds((r, c), bf16), grid=(r // tr,),
        in_specs=[pl.BlockSpec((None, tr, c), lambda i: (l, i, 0))],
        out_specs=pl.BlockSpec((tr, c), lambda i: (i, 0)),
        compiler_params=_cparams(("parallel",)), name="cast_bf16")(w)


def _mod_kernel(c_ref, w_ref, b_ref, o_ref):
    x = c_ref[...]
    s = x * _sigmoid(x)
    o_ref[0] = jnp.dot(s.astype(bf16), w_ref[0].astype(bf16), preferred_element_type=f32) + b_ref[0]


def modulation(cond, w_mod, b_mod):
    r = cond.shape[0]
    tn = 1536
    return pl.pallas_call(
        _mod_kernel, out_shape=_sds((DEPTH, r, 6 * D)), grid=(DEPTH, 6 * D // tn),
        in_specs=[pl.BlockSpec((r, D), lambda l, j: (0, 0)),
                  pl.BlockSpec((1, D, tn), lambda l, j: (l, 0, j)),
                  pl.BlockSpec((1, 1, tn), lambda l, j: (l, 0, j))],
        out_specs=pl.BlockSpec((1, r, tn), lambda l, j: (l, 0, j)),
        compiler_params=_cparams(("parallel", "parallel")), name="modulation")(
            cond, w_mod, b_mod.reshape(DEPTH, 1, 6 * D))


def _rms_mod(x, g, shift, scale):
    ms = jnp.mean(x * x, axis=-1, keepdims=True)
    return (x * lax.rsqrt(ms + EPS) * g) * (1.0 + scale) + shift


def _norm1_kernel(h_ref, g_ref, mod_ref, o_ref):
    m = mod_ref[0]
    o_ref[...] = _rms_mod(h_ref[...], g_ref[...], m[0:1, :], m[1:2, :]).astype(bf16)


def norm1(h, g, mod_tiles):
    nt = h.shape[0] // TM
    return pl.pallas_call(
        _norm1_kernel, out_shape=_sds(h.shape, bf16), grid=(nt,),
        in_specs=[pl.BlockSpec((TM, D), lambda i: (i, 0)),
                  pl.BlockSpec((1, D), lambda i: (0, 0)),
                  pl.BlockSpec((1, 6, D), lambda i: (i, 0, 0))],
        out_specs=pl.BlockSpec((TM, D), lambda i: (i, 0)),
        compiler_params=_cparams(("parallel",)), name="norm1")(h, g.reshape(1, D), mod_tiles)


def _matmul_kernel(x_ref, w_ref, o_ref):
    o_ref[...] = jnp.dot(x_ref[...], w_ref[...], preferred_element_type=f32)


def in_proj(xn, w):
    t = xn.shape[0]
    tm, tn = 1024, 1664
    return pl.pallas_call(
        _matmul_kernel, out_shape=_sds((t, IN_W)), grid=(t // tm, IN_W // tn),
        in_specs=[pl.BlockSpec((tm, D), lambda i, j: (i, 0)),
                  pl.BlockSpec((D, tn), lambda i, j: (0, j))],
        out_specs=pl.BlockSpec((tm, tn), lambda i, j: (i, j)),
        compiler_params=_cparams(("parallel", "parallel")), name="in_proj")(xn, w)


def _gelu_tanh(x):
    return 0.5 * x * (1.0 + jnp.tanh(0.7978845608028654 * (x + 0.044715 * (x * x * x))))


def _lru_kernel(ux_ref, uy_ref, cw_ref, cb_ref, wr_ref, wi_ref, br_ref, bi_ref, lam_ref, h0_ref,
                out_ref, st_ref, xp_ref, a_ref, b_ref, *, n):
    pad = SUBLANES
    xp_ref[pl.ds(0, pad), :] = jnp.zeros((pad, CW), f32)
    xp_ref[pl.ds(n + pad, pad), :] = jnp.zeros((pad, CW), f32)
    xp_ref[pl.ds(pad, n), :] = ux_ref[...]
    left = LRU_CONV // 2
    lam = lam_ref[...]
    neg_c_sp = -LRU_C * (jnp.maximum(-lam, 0.0) + _log1p(jnp.exp(-jnp.abs(lam))))

    def coef_body(i, c):
        r0 = pl.multiple_of(i * RC, RC)
        w = xp_ref[pl.ds(r0, RC + 2 * pad), :]
        xc = jnp.zeros((RC, CW), f32) + cb_ref[...]
        for k in range(LRU_CONV):
            off = pad - left + k
            xc = xc + cw_ref[k:k + 1, :] * w[off:off + RC, :]
        xcb = xc.astype(bf16)
        for d in range(2):
            r = _sigmoid(jnp.dot(xcb, wr_ref[d, 0], preferred_element_type=f32) + br_ref[d:d + 1, :])
            g = _sigmoid(jnp.dot(xcb, wi_ref[d, 0], preferred_element_type=f32) + bi_ref[d:d + 1, :])
            log_a = r * neg_c_sp[d:d + 1, :]
            a = jnp.exp(log_a)
            th = jnp.tanh(log_a)
            one_m_a2 = (-2.0 * th) / (1.0 - th)
            a_ref[d, pl.ds(r0, RC), :] = a
            b_ref[d, pl.ds(r0, RC), :] = jnp.sqrt(one_m_a2) * (g * xc)
        return c

    lax.fori_loop(0, n // RC, coef_body, 0)

    row = lax.broadcasted_iota(i32, (SUBLANES, CW), 0)
    nchunk = n // SUBLANES

    def scan_body(j, carry):
        hf, hb = carry
        rf = pl.multiple_of(j * SUBLANES, SUBLANES)
        a = a_ref[0, pl.ds(rf, SUBLANES), :]
        b = b_ref[0, pl.ds(rf, SUBLANES), :]
        for sh in (1, 2, 4):
            a_s = jnp.where(row >= sh, pltpu.roll(a, sh, axis=0), 1.0)
            b_s = jnp.where(row >= sh, pltpu.roll(b, sh, axis=0), 0.0)
            b = a * b_s + b
            a = a * a_s
        h = a * hf + b
        b_ref[0, pl.ds(rf, SUBLANES), :] = h
        hf = h[SUBLANES - 1:SUBLANES, :]
        rb = pl.multiple_of((nchunk - 1 - j) * SUBLANES, SUBLANES)
        a = a_ref[1, pl.ds(rb, SUBLANES), :]
        b = b_ref[1, pl.ds(rb, SUBLANES), :]
        for sh in (1, 2, 4):
            keep = row < SUBLANES - sh
            a_s = jnp.where(keep, pltpu.roll(a, SUBLANES - sh, axis=0), 1.0)
            b_s = jnp.where(keep, pltpu.roll(b, SUBLANES - sh, axis=0), 0.0)
            b = a * b_s + b
            a = a * a_s
        h = a * hb + b
        b_ref[1, pl.ds(rb, SUBLANES), :] = h
        hb = h[0:1, :]
        return hf, hb

    h0 = h0_ref[0]
    hf, hb = lax.fori_loop(0, nchunk, scan_body, (h0[0:1, :], h0[1:2, :]))
    st_ref[0, 0:1, :] = hf
    st_ref[0, 1:2, :] = hb

    def out_body(i, c):
        r0 = pl.multiple_of(i * RC, RC)
        hsum = b_ref[0, pl.ds(r0, RC), :] + b_ref[1, pl.ds(r0, RC), :]
        out_ref[pl.ds(r0, RC), :] = hsum * _gelu_tanh(uy_ref[pl.ds(r0, RC), :])
        return c

    lax.fori_loop(0, n // RC, out_body, 0)


def lru_branch(u, p, h0, nb, n, row_blk0):
    nct = D // CW
    cu = COL_UY // CW
    kern = functools.partial(_lru_kernel, n=n)
    vec = lambda a: pl.BlockSpec((a, CW), lambda b, c: (0, c))
    call = pl.pallas_call(
        kern, out_shape=(_sds((nb * n, D)), _sds((nb, 2, D))), grid=(nb, nct),
        in_specs=[pl.BlockSpec((n, CW), lambda b, c: (row_blk0 + b, c)),
                  pl.BlockSpec((n, CW), lambda b, c: (row_blk0 + b, cu + c)),
                  vec(LRU_CONV), vec(1),
                  pl.BlockSpec((2, 1, CW, CW), lambda b, c: (0, c, 0, 0)),
                  pl.BlockSpec((2, 1, CW, CW), lambda b, c: (0, c, 0, 0)),
                  vec(2), vec(2), vec(2),
                  pl.BlockSpec((1, 2, CW), lambda b, c: (b, 0, c))],
        out_specs=(pl.BlockSpec((n, CW), lambda b, c: (b, c)),
                   pl.BlockSpec((1, 2, CW), lambda b, c: (b, 0, c))),
        scratch_shapes=[pltpu.VMEM((n + 2 * SUBLANES, CW), f32),
                        pltpu.VMEM((2, n, CW), f32), pltpu.VMEM((2, n, CW), f32)],
        compiler_params=_cparams(("parallel", "parallel")), name="lru_branch")
    return call(u, u, p["lru_conv_w"], p["lru_conv_b"].reshape(1, D), p["lru_wr_bd"], p["lru_wi_bd"],
                p["lru_b_r"], p["lru_b_i"], p["lru_lambda"], h0)


def _conf_kernel(ua_ref, ug_ref, w_ref, b_ref, o_ref, gp_ref, *, n):
    pad = 2 * SUBLANES
    left = CONV_K // 2
    gp_ref[pl.ds(0, pad), :] = jnp.zeros((pad, CW), f32)
    gp_ref[pl.ds(n + pad, pad), :] = jnp.zeros((pad, CW), f32)
    gp_ref[pl.ds(pad, n), :] = ua_ref[...] * _sigmoid(ug_ref[...])

    def body(i, c):
        r0 = pl.multiple_of(i * RC, RC)
        w = gp_ref[pl.ds(r0, RC + 2 * pad), :]
        acc = jnp.zeros((RC, CW), f32) + b_ref[...]
        for k in range(CONV_K):
            off = pad - left + k
            acc = acc + w_ref[k:k + 1, :] * w[off:off + RC, :]
        o_ref[pl.ds(r0, RC), :] = acc
        return c

    lax.fori_loop(0, n // RC, body, 0)


def conf_branch(u, p, nb, n, row_blk0):
    nct = D // CW
    ca, cg = COL_UC // CW, (COL_UC + D) // CW
    kern = functools.partial(_conf_kernel, n=n)
    call = pl.pallas_call(
        kern, out_shape=_sds((nb * n, D)), grid=(nb, nct),
        in_specs=[pl.BlockSpec((n, CW), lambda b, c: (row_blk0 + b, ca + c)),
                  pl.BlockSpec((n, CW), lambda b, c: (row_blk0 + b, cg + c)),
                  pl.BlockSpec((CONV_K, CW), lambda b, c: (0, c)),
                  pl.BlockSpec((1, CW), lambda b, c: (0, c))],
        out_specs=pl.BlockSpec((n, CW), lambda b, c: (b, c)),
        scratch_shapes=[pltpu.VMEM((n + 4 * SUBLANES, CW), f32)],
        compiler_params=_cparams(("parallel", "parallel")), name="conf_branch")
    return call(u, u, p["conv_dw_w"], p["conv_dw_b"].reshape(1, D))


def _pool_kernel(up_ref, w_ref, s_ref, o_ref, xp_ref, *, n):
    pad = SUBLANES
    gi = pl.program_id(1)
    half = jnp.left_shift(1, gi)
    xp_ref[pl.ds(0, pad), :] = jnp.zeros((pad, CW), f32)
    xp_ref[pl.ds(n + pad, pad), :] = jnp.zeros((pad, CW), f32)
    xp_ref[pl.ds(pad, n), :] = up_ref[...]
    wn = RC + 2 * pad

    def body(i, c):
        r0 = pl.multiple_of(i * RC, RC)
        w = xp_ref[pl.ds(r0, wn), :]
        s2 = w + pltpu.roll(w, 1, axis=0)
        s4 = pltpu.roll(s2, 1, axis=0) + pltpu.roll(s2, wn - 1, axis=0)
        s8 = pltpu.roll(s4, 2, axis=0) + pltpu.roll(s4, wn - 2, axis=0)
        s16 = pltpu.roll(s8, 4, axis=0) + pltpu.roll(s8, wn - 4, axis=0)
        s = jnp.where(gi == 0, s2, jnp.where(gi == 1, s4, jnp.where(gi == 2, s8, s16)))[pad:pad + RC, :]
        t = r0 + lax.broadcasted_iota(i32, (RC, CW), 0)
        cnt = (jnp.minimum(t + half, n) - jnp.maximum(t - half, 0)).astype(f32)
        pooled = s / cnt - w[pad:pad + RC, :]
        o_ref[pl.ds(r0, RC), :] = jnp.dot(pooled.astype(bf16), w_ref[0], preferred_element_type=f32) * s_ref[...]
        return c

    lax.fori_loop(0, n // RC, body, 0)


def pool_branch(u, p, nb, n, row_blk0):
    assert POOL_SIZES == (2, 4, 8, 16) and D // len(POOL_SIZES) == CW
    cp = COL_UP // CW
    kern = functools.partial(_pool_kernel, n=n)
    call = pl.pallas_call(
        kern, out_shape=_sds((nb * n, D)), grid=(nb, len(POOL_SIZES)),
        in_specs=[pl.BlockSpec((n, CW), lambda b, c: (row_blk0 + b, cp + c)),
                  pl.BlockSpec((1, CW, CW), lambda b, c: (c, 0, 0)),
                  pl.BlockSpec((1, CW), lambda b, c: (0, c))],
        out_specs=pl.BlockSpec((n, CW), lambda b, c: (b, c)),
        scratch_shapes=[pltpu.VMEM((n + 2 * SUBLANES, CW), f32)],
        compiler_params=_cparams(("parallel", "parallel")), name="pool_branch")
    return call(u, p["pool_w_bf"], p["pool_scale"].reshape(1, D))


def _head_norm(blk, g128, lane_lo):
    sq = blk * blk
    s_lo = jnp.sum(jnp.where(lane_lo, sq, 0.0), axis=-1, keepdims=True)
    s_hi = jnp.sum(jnp.where(lane_lo, 0.0, sq), axis=-1, keepdims=True)
    r = jnp.where(lane_lo, lax.rsqrt(s_lo * (1.0 / HEAD_DIM) + EPS), lax.rsqrt(s_hi * (1.0 / HEAD_DIM) + EPS))
    return blk * r * g128


def _rope(blk, cos, sin_signed, lane):
    partner = jnp.where((lane % 32) < 16, pltpu.roll(blk, LANES - 16, axis=1), pltpu.roll(blk, 16, axis=1))
    return blk * cos + partner * sin_signed


def _prep_kv(k_ref, v_ref, ks_ref, vs_ref, kg, lane_lo, rope=None, newk_ref=None, newv_ref=None):
    nk = k_ref.shape[0]
    lane = lax.broadcasted_iota(i32, (nk, LANES), 1)
    for tj in range(N_KV * HEAD_DIM // LANES):
        kt = _head_norm(k_ref[:, tj * LANES:(tj + 1) * LANES], kg, lane_lo(nk))
        if rope is not None:
            kt = _rope(kt, rope[0][...], rope[1][...], lane)
        vt = v_ref[:, tj * LANES:(tj + 1) * LANES]
        if newk_ref is not None:
            newk_ref[0, :, tj * LANES:(tj + 1) * LANES] = kt
            newv_ref[0, :, tj * LANES:(tj + 1) * LANES] = vt
        for half in range(2):
            g = 2 * tj + half
            keep = lane_lo(nk) if half == 0 else jnp.logical_not(lane_lo(nk))
            ka = jnp.where(keep, kt, 0.0)
            va = jnp.where(keep, vt, 0.0)
            ks_ref[g, half] = ka.astype(bf16)
            ks_ref[g, 1 - half] = pltpu.roll(ka, HEAD_DIM, axis=1).astype(bf16)
            vs_ref[g, half] = va.astype(bf16)
            vs_ref[g, 1 - half] = pltpu.roll(va, HEAD_DIM, axis=1).astype(bf16)


def _prep_cache(ck_ref, cv_ref, kc_ref, vc_ref, lane_lo):
    nk = ck_ref.shape[2]
    for tj in range(N_KV * HEAD_DIM // LANES):
        kt = ck_ref[0, 0, :, tj * LANES:(tj + 1) * LANES]
        vt = cv_ref[0, 0, :, tj * LANES:(tj + 1) * LANES]
        for half in range(2):
            g = 2 * tj + half
            keep = lane_lo(nk) if half == 0 else jnp.logical_not(lane_lo(nk))
            ka = jnp.where(keep, kt, 0.0)
            va = jnp.where(keep, vt, 0.0)
            kc_ref[g, half] = ka.astype(bf16)
            kc_ref[g, 1 - half] = pltpu.roll(ka, HEAD_DIM, axis=1).astype(bf16)
            vc_ref[g, half] = va.astype(bf16)
            vc_ref[g, 1 - half] = pltpu.roll(va, HEAD_DIM, axis=1).astype(bf16)


def _qk(qm, k):
    return lax.dot_general(qm, k, (((1,), (1,)), ((), ())), preferred_element_type=f32)


def _attend(sink_ref, q_tiles, segments, att_ref):
    lane_lo = lax.broadcasted_iota(i32, (TQ, LANES), 1) < HEAD_DIM
    for j in range(N_HEADS // 2):
        out_tile = None
        for half in range(2):
            h = 2 * j + half
            g = h // (N_HEADS // N_KV)
            qm = jnp.where(lane_lo if half == 0 else jnp.logical_not(lane_lo), q_tiles[j], 0.0).astype(bf16)
            sink = sink_ref[h]
            scores = []
            m = jnp.full((TQ, 1), sink, f32)
            for kget, _, mask in segments:
                s = _qk(qm, kget(g, half))
                if mask is not None:
                    s = jnp.where(mask, s, NEG_INF)
                m = jnp.maximum(m, jnp.max(s, axis=-1, keepdims=True))
                scores.append(s)
            den = jnp.exp(sink - m)
            o = jnp.zeros((TQ, LANES), f32)
            for s, (_, vget, _) in zip(scores, segments):
                pr = jnp.exp(s - m)
                den = den + jnp.sum(pr, axis=-1, keepdims=True)
                o = o + jnp.dot(pr.astype(bf16), vget(g, half), preferred_element_type=f32)
            o = o / den
            out_tile = o if out_tile is None else out_tile + o
        att_ref[:, j * LANES:(j + 1) * LANES] = out_tile


def _q_tiles(q_ref, qg, rope=None):
    lane = lax.broadcasted_iota(i32, (TQ, LANES), 1)
    lane_lo = lane < HEAD_DIM
    tiles = []
    for j in range(N_HEADS // 2):
        qt = _head_norm(q_ref[:, j * LANES:(j + 1) * LANES], qg, lane_lo)
        if rope is not None:
            qt = _rope(qt, rope[0], rope[1], lane)
        tiles.append(qt * (HEAD_DIM ** -0.5))
    return tiles


def _lane_lo_fn(nrows):
    return lax.broadcasted_iota(i32, (nrows, LANES), 1) < HEAD_DIM


def _ctx_attn_kernel(sink_ref, q_ref, k_ref, v_ref, qg_ref, kg_ref, att_ref, newk_ref, newv_ref, ks_ref, vs_ref):
    @pl.when(pl.program_id(1) == 0)
    def _():
        _prep_kv(k_ref, v_ref, ks_ref, vs_ref, kg_ref[...], _lane_lo_fn, None, newk_ref, newv_ref)

    tiles = _q_tiles(q_ref, qg_ref[...])
    seg = [(lambda g, half: ks_ref[g, half], lambda g, half: vs_ref[g, half], None)]
    _attend(sink_ref, tiles, seg, att_ref)


def ctx_attention(u, p):
    nq = N_CTX // TQ
    kvw = N_KV * HEAD_DIM
    return pl.pallas_call(
        _ctx_attn_kernel,
        out_shape=(_sds((T_CTX, D)), _sds((N_CTX_B, N_CTX, kvw)), _sds((N_CTX_B, N_CTX, kvw))),
        grid=(N_CTX_B, nq),
        in_specs=[pl.BlockSpec(memory_space=pltpu.SMEM),
                  pl.BlockSpec((TQ, D), lambda b, i: (b * nq + i, COL_UQ // D)),
                  pl.BlockSpec((N_CTX, kvw), lambda b, i: (b, COL_UK // kvw)),
                  pl.BlockSpec((N_CTX, kvw), lambda b, i: (b, COL_UV // kvw)),
                  pl.BlockSpec((1, LANES), lambda b, i: (0, 0)),
                  pl.BlockSpec((1, LANES), lambda b, i: (0, 0))],
        out_specs=(pl.BlockSpec((TQ, D), lambda b, i: (b * nq + i, 0)),
                   pl.BlockSpec((1, N_CTX, kvw), lambda b, i: (b, 0, 0)),
                   pl.BlockSpec((1, N_CTX, kvw), lambda b, i: (b, 0, 0))),
        scratch_shapes=[pltpu.VMEM((N_KV, 2, N_CTX, LANES), bf16), pltpu.VMEM((N_KV, 2, N_CTX, LANES), bf16)],
        compiler_params=_cparams(("parallel", "arbitrary")), name="ctx_attention")(
            p["attn_sink"], u, u, u, p["q_norm_g2"], p["k_norm_g2"])


def _lat_attn_kernel(sink_ref, q_ref, k_ref, v_ref, ck_ref, cv_ref, cosq_ref, sinq_ref, cosk_ref, sink_tab_ref,
                     qg_ref, kg_ref, att_ref, ks_ref, vs_ref, kc_ref, vc_ref):
    i = pl.program_id(1)
    nq = N_LAT // TQ

    @pl.when(i == 0)
    def _():
        _prep_kv(k_ref, v_ref, ks_ref, vs_ref, kg_ref[...], _lane_lo_fn, (cosk_ref, sink_tab_ref))
        _prep_cache(ck_ref, cv_ref, kc_ref, vc_ref, _lane_lo_fn)

    tiles = _q_tiles(q_ref, qg_ref[...], (cosq_ref[...], sinq_ref[...]))
    r = lax.broadcasted_iota(i32, (TQ, TQ), 0)
    c = lax.broadcasted_iota(i32, (TQ, TQ), 1)
    prev0 = pl.multiple_of(jnp.maximum(i - 1, 0) * TQ, TQ)
    cur0 = pl.multiple_of(i * TQ, TQ)
    next0 = pl.multiple_of(jnp.minimum(i + 1, nq - 1) * TQ, TQ)
    mask_prev = jnp.logical_and(c >= r, i > 0)
    mask_next = jnp.logical_and(c <= r, i < nq - 1)

    def seg(r0, mask):
        return (lambda g, half: ks_ref[g, half, pl.ds(r0, TQ), :],
                lambda g, half: vs_ref[g, half, pl.ds(r0, TQ), :], mask)

    segments = [seg(prev0, mask_prev), seg(cur0, None), seg(next0, mask_next),
                (lambda g, half: kc_ref[g, half], lambda g, half: vc_ref[g, half], None)]
    _attend(sink_ref, tiles, segments, att_ref)


def lat_attention(u, p, cache_k, cache_v, rope_cos, rope_sin):
    nq = N_LAT // TQ
    kvw = N_KV * HEAD_DIM
    l = p["layer"]
    qblk0 = T_CTX // TQ
    kblk0 = T_CTX // N_LAT
    call = pl.pallas_call(
        _lat_attn_kernel, out_shape=_sds((T_LAT, D)), grid=(N_LAT_B, nq),
        in_specs=[pl.BlockSpec(memory_space=pltpu.SMEM),
                  pl.BlockSpec((TQ, D), lambda b, i: (qblk0 + b * nq + i, COL_UQ // D)),
                  pl.BlockSpec((N_LAT, kvw), lambda b, i: (kblk0 + b, COL_UK // kvw)),
                  pl.BlockSpec((N_LAT, kvw), lambda b, i: (kblk0 + b, COL_UV // kvw)),
                  pl.BlockSpec((1, 1, PAST, kvw), lambda b, i: (b, l, 0, 0)),
                  pl.BlockSpec((1, 1, PAST, kvw), lambda b, i: (b, l, 0, 0)),
                  pl.BlockSpec((TQ, LANES), lambda b, i: (i, 0)),
                  pl.BlockSpec((TQ, LANES), lambda b, i: (i, 0)),
                  pl.BlockSpec((N_LAT, LANES), lambda b, i: (0, 0)),
                  pl.BlockSpec((N_LAT, LANES), lambda b, i: (0, 0)),
                  pl.BlockSpec((1, LANES), lambda b, i: (0, 0)),
                  pl.BlockSpec((1, LANES), lambda b, i: (0, 0))],
        out_specs=pl.BlockSpec((TQ, D), lambda b, i: (b * nq + i, 0)),
        scratch_shapes=[pltpu.VMEM((N_KV, 2, N_LAT, LANES), bf16), pltpu.VMEM((N_KV, 2, N_LAT, LANES), bf16),
                        pltpu.VMEM((N_KV, 2, PAST, LANES), bf16), pltpu.VMEM((N_KV, 2, PAST, LANES), bf16)],
        compiler_params=_cparams(("parallel", "arbitrary")), name="lat_attention")
    return call(p["attn_sink"], u, u, u, cache_k, cache_v, rope_cos, rope_sin, rope_cos, rope_sin,
                p["q_norm_g2"], p["k_norm_g2"])


def rope_tables():
    pos = jnp.arange(N_LAT)
    rows = (pos // GRID_W).astype(f32)
    cols = (pos % GRID_W).astype(f32)
    half = HEAD_DIM // 2
    freqs = ROPE_BASE ** (-jnp.arange(0, half, 2, dtype=f32) / half)
    lane = jnp.arange(LANES)
    within = lane % half
    fidx = within % (half // 2)
    use_cols = (lane % HEAD_DIM) >= half
    ang = jnp.where(use_cols[None, :], cols[:, None], rows[:, None]) * freqs[fidx][None, :]
    sign = jnp.where(within < half // 2, -1.0, 1.0)
    return jnp.cos(ang), jnp.sin(ang) * sign[None, :]


def _merge_kernel(h_ref, xn_ref, lru_c, lru_l, conv_c, conv_l, att_c, att_l, pool_c, pool_l, mod_ref, lng_ref, lnb_ref,
                  wg_ref, bg_ref, wb_ref, wo_ref, o_ref):
    is_ctx = pl.program_id(0) < T_CTX // TM
    pick = lambda c_ref, l_ref: jnp.where(is_ctx, c_ref[...], l_ref[...])
    xn = xn_ref[...]
    hc = pick(conv_c, conv_l)
    mu = jnp.mean(hc, axis=-1, keepdims=True)
    xc = hc - mu
    var = jnp.mean(xc * xc, axis=-1, keepdims=True)
    y = xc * lax.rsqrt(var + EPS) * lng_ref[...] + lnb_ref[...]
    conv = y * _sigmoid(y)
    merged = jnp.zeros((TM, D), f32)
    for j, br in enumerate((pick(lru_c, lru_l), conv, pick(att_c, att_l), pick(pool_c, pool_l))):
        gate = _sigmoid(jnp.dot(xn, wg_ref[:, j * D:(j + 1) * D], preferred_element_type=f32)
                        + bg_ref[:, j * D:(j + 1) * D])
        proj = jnp.dot(br.astype(bf16), wb_ref[j], preferred_element_type=f32)
        merged = merged + gate * proj
    out = jnp.dot(merged.astype(bf16), wo_ref[...], preferred_element_type=f32)
    o_ref[...] = h_ref[...] + mod_ref[0][2:3, :] * out


def merge(h, xn, lru, conv, att, pool, mod_tiles, p):
    nt = h.shape[0] // TM
    n_ctx_tiles = T_CTX // TM
    tok = pl.BlockSpec((TM, D), lambda i: (i, 0))
    tok_c = pl.BlockSpec((TM, D), lambda i: (jnp.minimum(i, n_ctx_tiles - 1), 0))
    tok_l = pl.BlockSpec((TM, D), lambda i: (jnp.maximum(i - n_ctx_tiles, 0), 0))
    const = lambda shape: pl.BlockSpec(shape, lambda i: (0,) * len(shape), pipeline_mode=pl.Buffered(1))
    return pl.pallas_call(
        _merge_kernel, out_shape=_sds(h.shape), grid=(nt,),
        in_specs=[tok, tok, tok_c, tok_l, tok_c, tok_l, tok_c, tok_l, tok_c, tok_l,
                  pl.BlockSpec((1, 6, D), lambda i: (i, 0, 0)),
                  const((1, D)), const((1, D)),
                  const((D, 4 * D)), const((1, 4 * D)), const((4, D, D)), const((D, D))],
        out_specs=tok,
        compiler_params=_cparams(("parallel",)), name="merge")(
            h, xn, *lru, *conv, *att, *pool, mod_tiles, p["conv_ln_g"].reshape(1, D), p["conv_ln_b"].reshape(1, D),
            p["w_gate_bf"], p["b_gate"].reshape(1, 4 * D), p["w_branch_bf"], p["w_out_bf"])


def _router_kernel(h_ref, g_ref, mod_ref, rw_ref, rb_ref, xn_ref, topi_ref, topw_ref, rank_ref, cnt_ref, carry_ref):
    step = pl.program_id(0)

    @pl.when(step == 0)
    def _():
        carry_ref[...] = jnp.zeros_like(carry_ref)

    m = mod_ref[0]
    xn = _rms_mod(h_ref[...], g_ref[...], m[3:4, :], m[4:5, :])
    xn_ref[...] = xn
    logits = jnp.dot(xn, rw_ref[...], preferred_element_type=f32, precision=lax.Precision.HIGHEST) + rb_ref[...]
    lane = lax.broadcasted_iota(i32, (TM, LANES), 1).astype(f32)
    lg = jnp.where(lane < N_EXPERTS, logits, NEG_INF)
    vals, idxs = [], []
    for _ in range(TOP_K):
        mx = jnp.max(lg, axis=-1, keepdims=True)
        idx = jnp.min(jnp.where(lg == mx, lane, float(LANES)), axis=-1, keepdims=True)
        vals.append(mx)
        idxs.append(idx)
        lg = jnp.where(lane == idx, -3e38, lg)
    exps = [jnp.exp(v - vals[0]) for v in vals]
    den = exps[0] + exps[1] + exps[2] + exps[3]
    cnt = jnp.zeros((TM, LANES), f32)
    for idx in idxs:
        cnt = cnt + jnp.where(lane == idx, 1.0, 0.0)
    rr = lax.broadcasted_iota(i32, (TM, TM), 0)
    cc = lax.broadcasted_iota(i32, (TM, TM), 1)
    tri = jnp.where(rr > cc, 1.0, 0.0).astype(bf16)
    before = jnp.dot(tri, cnt.astype(bf16), preferred_element_type=f32) + carry_ref[0:1, :]
    topi = jnp.zeros((TM, LANES), f32)
    topw = jnp.zeros((TM, LANES), f32)
    rank = jnp.zeros((TM, LANES), f32)
    for k in range(TOP_K):
        rk = jnp.sum(jnp.where(lane == idxs[k], before, 0.0), axis=-1, keepdims=True)
        topi = jnp.where(lane == k, idxs[k], topi)
        topw = jnp.where(lane == k, exps[k] / den, topw)
        rank = jnp.where(lane == k, rk, rank)
    topi_ref[...] = topi.astype(i32)
    topw_ref[...] = topw
    rank_ref[...] = rank.astype(i32)
    total = carry_ref[0:1, :] + jnp.sum(cnt, axis=0, keepdims=True)
    carry_ref[...] = jnp.broadcast_to(total, carry_ref.shape)
    cnt_ref[...] = jnp.broadcast_to(total, cnt_ref.shape).astype(i32)


def router(h, mod_tiles, p):
    nt = h.shape[0] // TM
    tok = lambda w, dt=f32: pl.BlockSpec((TM, w), lambda i: (i, 0))
    t = h.shape[0]
    return pl.pallas_call(
        _router_kernel,
        out_shape=(_sds((t, D)), _sds((t, LANES), i32), _sds((t, LANES)), _sds((t, LANES), i32),
                   _sds((SUBLANES, LANES), i32)),
        grid=(nt,),
        in_specs=[tok(D), pl.BlockSpec((1, D), lambda i: (0, 0)), pl.BlockSpec((1, 6, D), lambda i: (i, 0, 0)),
                  pl.BlockSpec((D, LANES), lambda i: (0, 0)), pl.BlockSpec((1, LANES), lambda i: (0, 0))],
        out_specs=(tok(D), tok(LANES), tok(LANES), tok(LANES), pl.BlockSpec((SUBLANES, LANES), lambda i: (0, 0))),
        scratch_shapes=[pltpu.VMEM((SUBLANES, LANES), f32)],
        compiler_params=_cparams(("arbitrary",)), name="router")(
            h, p["norm2_g"].reshape(1, D), mod_tiles, p["router_w_pad"], p["router_b_pad"])


def _dest_kernel(topi_ref, rank_ref, pstart_ref, o_ref):
    lane = lax.broadcasted_iota(i32, (TM, LANES), 1).astype(f32)
    topi = topi_ref[...].astype(f32)
    pstart = pstart_ref[...].astype(f32)
    dest = rank_ref[...].astype(f32)
    for k in range(TOP_K):
        e = jnp.sum(jnp.where(lane == k, topi, 0.0), axis=-1, keepdims=True)
        ps = jnp.sum(jnp.where(lane == e, pstart, 0.0), axis=-1, keepdims=True)
        dest = dest + jnp.where(lane == k, ps, 0.0)
    o_ref[...] = dest.astype(i32)


def dest_rows(topi, rank, pstart):
    t = topi.shape[0]
    tok = pl.BlockSpec((TM, LANES), lambda i: (i, 0))
    return pl.pallas_call(
        _dest_kernel, out_shape=_sds((t, LANES), i32), grid=(t // TM,),
        in_specs=[tok, tok, pl.BlockSpec((1, LANES), lambda i: (0, 0))], out_specs=tok,
        compiler_params=_cparams(("parallel",)), name="dest_rows")(topi, rank, pstart)


def _row_copy(src_ref, s, dst_ref, d, sem):
    return pltpu.make_async_copy(src_ref.at[pl.ds(s, 1), :], dst_ref.at[pl.ds(d, 1), :], sem)


def _dispatch_kernel(zflag_ref, dest_ref, x_ref, xb_ref, zero_ref, sem):
    @pl.when(pl.program_id(0) == 0)
    def _():
        zero_ref[...] = jnp.zeros_like(zero_ref)

        def zcopy(j):
            r0 = pl.multiple_of(j * MOE_BM, MOE_BM)
            return pltpu.make_async_copy(zero_ref, xb_ref.at[pl.ds(r0, MOE_BM), :], sem)

        def zstart(j, c):
            @pl.when(zflag_ref[j] != 0)
            def _():
                zcopy(j).start()
            return c

        def zwait(j, c):
            @pl.when(zflag_ref[j] != 0)
            def _():
                zcopy(j).wait()
            return c

        lax.fori_loop(0, MOE_NBLK, zstart, 0)
        lax.fori_loop(0, MOE_NBLK, zwait, 0)

    def start(t, c):
        for k in range(TOP_K):
            _row_copy(x_ref, t, xb_ref, dest_ref[t * TOP_K + k], sem).start(priority=k % 2)
        return c

    lax.fori_loop(0, TM, start, 0)

    def wait(t, c):
        for k in range(TOP_K):
            _row_copy(x_ref, 0, xb_ref, 0, sem).wait()
        return c

    lax.fori_loop(0, TM, wait, 0)


def dispatch(zflag, dest_flat, xn):
    t = xn.shape[0]
    grid_spec = pltpu.PrefetchScalarGridSpec(
        num_scalar_prefetch=1, grid=(t // TM,),
        in_specs=[pl.BlockSpec((TM * TOP_K,), lambda i, zb: (i,), memory_space=pltpu.SMEM),
                  pl.BlockSpec((TM, D), lambda i, zb: (i, 0))],
        out_specs=pl.BlockSpec(memory_space=pl.ANY),
        scratch_shapes=[pltpu.VMEM((MOE_BM, D), f32), pltpu.SemaphoreType.DMA(())])
    return pl.pallas_call(
        _dispatch_kernel, out_shape=_sds((MOE_CAP, D)), grid_spec=grid_spec,
        compiler_params=_cparams(("arbitrary",)), name="moe_dispatch")(zflag, dest_flat, xn)


def _expert_kernel(be_ref, nused_ref, xb_ref, wgu_ref, bgu_ref, wdn_ref, bdn_ref, yb_ref, wgu_bf, wdn_bf):
    i = pl.program_id(0)
    e = be_ref[i]
    prev = be_ref[jnp.maximum(i - 1, 0)]

    @pl.when(jnp.logical_or(i == 0, e != prev))
    def _():
        wgu_bf[...] = wgu_ref[0, 0].astype(bf16)
        wdn_bf[...] = wdn_ref[0, 0].astype(bf16)

    @pl.when(i < nused_ref[0])
    def _():
        x = xb_ref[...].astype(bf16)
        h = jnp.dot(x, wgu_bf[...], preferred_element_type=f32) + bgu_ref[0, 0]
        gate = jnp.minimum(h[:, :D], SWIGLU_LIMIT)
        lin = jnp.clip(h[:, D:], -SWIGLU_LIMIT, SWIGLU_LIMIT)
        act = (lin + 1.0) * (gate * _sigmoid(SWIGLU_ALPHA * gate))
        yb_ref[...] = jnp.dot(act.astype(bf16), wdn_bf[...], preferred_element_type=f32) + bdn_ref[0, 0]

    @pl.when(i >= nused_ref[0])
    def _():
        yb_ref[...] = jnp.zeros_like(yb_ref)


def experts(block_e, nused, xb, p):
    l = p["layer"]
    grid_spec = pltpu.PrefetchScalarGridSpec(
        num_scalar_prefetch=2, grid=(MOE_NBLK,),
        in_specs=[pl.BlockSpec((MOE_BM, D), lambda i, be, nu: (jnp.minimum(i, nu[0] - 1), 0)),
                  pl.BlockSpec((1, 1, D, 2 * D), lambda i, be, nu: (l, be[i], 0, 0)),
                  pl.BlockSpec((1, 1, 1, 2 * D), lambda i, be, nu: (l, be[i], 0, 0)),
                  pl.BlockSpec((1, 1, D, D), lambda i, be, nu: (l, be[i], 0, 0)),
                  pl.BlockSpec((1, 1, 1, D), lambda i, be, nu: (l, be[i], 0, 0))],
        out_specs=pl.BlockSpec((MOE_BM, D), lambda i, be, nu: (i, 0)),
        scratch_shapes=[pltpu.VMEM((D, 2 * D), bf16), pltpu.VMEM((D, D), bf16)])
    return pl.pallas_call(
        _expert_kernel, out_shape=_sds((MOE_CAP, D)), grid_spec=grid_spec,
        compiler_params=_cparams(("arbitrary",)), name="moe_experts")(
            block_e, nused, xb, p["exp_w_gu"], p["exp_b_gu"].reshape(DEPTH, N_EXPERTS, 1, 2 * D),
            p["exp_w_down"], p["exp_b_down"].reshape(DEPTH, N_EXPERTS, 1, D))


def _combine_kernel(dest_ref, yb_ref, topw_ref, h_ref, mod_ref, o_ref, buf, sem):
    def start(t, c):
        for k in range(TOP_K):
            pltpu.make_async_copy(yb_ref.at[pl.ds(dest_ref[t * TOP_K + k], 1), :], buf.at[k, pl.ds(t, 1), :],
                                  sem).start(priority=k % 2)
        return c

    lax.fori_loop(0, TM, start, 0)

    def wait(t, c):
        for k in range(TOP_K):
            pltpu.make_async_copy(yb_ref.at[pl.ds(0, 1), :], buf.at[k, pl.ds(0, 1), :], sem).wait()
        return c

    lax.fori_loop(0, TM, wait, 0)
    w = topw_ref[...]
    y = jnp.zeros((TM, D), f32)
    for k in range(TOP_K):
        y = y + buf[k] * w[:, k:k + 1]
    o_ref[...] = h_ref[...] + mod_ref[0][5:6, :] * y


def combine(dest_flat, yb, topw, h, mod_tiles):
    t = h.shape[0]
    tok = pl.BlockSpec((TM, D), lambda i: (i, 0))
    return pl.pallas_call(
        _combine_kernel, out_shape=_sds(h.shape), grid=(t // TM,),
        in_specs=[pl.BlockSpec((TM * TOP_K,), lambda i: (i,), memory_space=pltpu.SMEM),
                  pl.BlockSpec(memory_space=pl.ANY),
                  pl.BlockSpec((TM, LANES), lambda i: (i, 0)), tok,
                  pl.BlockSpec((1, 6, D), lambda i: (i, 0, 0))],
        out_specs=tok,
        scratch_shapes=[pltpu.VMEM((TOP_K, TM, D), f32), pltpu.SemaphoreType.DMA(())],
        compiler_params=_cparams(("arbitrary",)), name="moe_combine")(dest_flat, yb, topw, h, mod_tiles)


def moe_layer(h, mod_tiles, p):
    xn, topi, topw, rank, counts = router(h, mod_tiles, p)
    cnt = counts[0, :N_EXPERTS]
    padded = (cnt + MOE_BM - 1) // MOE_BM * MOE_BM
    pend = jnp.cumsum(padded)
    pstart = jnp.zeros((1, LANES), i32).at[0, :N_EXPERTS].set(pend - padded)
    blk_row0 = jnp.arange(MOE_NBLK, dtype=i32) * MOE_BM
    block_e = jnp.minimum(jnp.sum((pend[None, :] <= blk_row0[:, None]).astype(i32), axis=1), N_EXPERTS - 1)
    nused = (pend[-1:] // MOE_BM).astype(i32)
    blk = jnp.arange(MOE_NBLK, dtype=i32)
    last_of_expert = jnp.any((padded > 0)[None, :] & (blk[:, None] == (pend // MOE_BM - 1)[None, :]), axis=1)
    zflag = (last_of_expert | (blk >= nused[0])).astype(i32)
    dest = dest_rows(topi, rank, pstart)
    dest_flat = dest[:, :TOP_K].reshape(-1)
    xb = dispatch(zflag, dest_flat, xn)
    yb = experts(block_e, nused, xb, p)
    return combine(dest_flat, yb, topw, h, mod_tiles)


def _block_diag(w):
    per = CW // LRU_BW
    w5 = w.reshape(2, D // CW, per, LRU_BW, LRU_BW)
    eye = jnp.eye(per, dtype=w.dtype)
    bd = w5[:, :, :, :, None, :] * eye[None, None, :, None, :, None]
    return bd.reshape(2, D // CW, CW, CW).astype(bf16)


def _layer_params(l, a):
    tile2 = lambda g: jnp.tile(g, 2).reshape(1, LANES)
    whole = ("exp_w_gu", "exp_b_gu", "exp_w_down", "exp_b_down", "w_in", "w_gate", "w_branch", "w_out")
    p = {k: v[l] for k, v in a.items() if k not in whole}
    p["layer"] = l
    for k in ("exp_w_gu", "exp_b_gu", "exp_w_down", "exp_b_down"):
        p[k] = a[k]
    p["w_in_bf"] = cast_bf16(a["w_in"], l)
    p["w_gate_bf"] = cast_bf16(a["w_gate"], l)
    p["w_branch_bf"] = cast_bf16(a["w_branch"].reshape(DEPTH, 4 * D, D), l).reshape(4, D, D)
    p["w_out_bf"] = cast_bf16(a["w_out"], l)
    p["pool_w_bf"] = cast_bf16(a["pool_w"].reshape(DEPTH, D, CW), l).reshape(4, CW, CW)
    p["lru_wr_bd"] = _block_diag(p["lru_w_r"])
    p["lru_wi_bd"] = _block_diag(p["lru_w_i"])
    p["q_norm_g2"] = tile2(p["q_norm_g"])
    p["k_norm_g2"] = tile2(p["k_norm_g"])
    p["router_w_pad"] = jnp.pad(p["router_w"], ((0, 0), (0, LANES - N_EXPERTS)))
    p["router_b_pad"] = jnp.pad(p["router_b"], (0, LANES - N_EXPERTS)).reshape(1, LANES)
    return p


def kernel(x_prompt, x_sample, cache_k, cache_v, state_lru, c, c_ctx, norm1_g, norm2_g, w_mod, b_mod, w_in,
           lru_conv_w, lru_conv_b, lru_w_r, lru_b_r, lru_w_i, lru_b_i, lru_lambda, conv_dw_w, conv_dw_b,
           conv_ln_g, conv_ln_b, q_norm_g, k_norm_g, attn_sink, pool_w, pool_scale, w_branch, w_gate, b_gate,
           w_out, router_w, router_b, exp_w_gu, exp_b_gu, exp_w_down, exp_b_down):
    weights = dict(norm1_g=norm1_g, norm2_g=norm2_g, w_in=w_in, lru_conv_w=lru_conv_w, lru_conv_b=lru_conv_b,
                   lru_w_r=lru_w_r, lru_b_r=lru_b_r, lru_w_i=lru_w_i, lru_b_i=lru_b_i, lru_lambda=lru_lambda,
                   conv_dw_w=conv_dw_w, conv_dw_b=conv_dw_b, conv_ln_g=conv_ln_g, conv_ln_b=conv_ln_b,
                   q_norm_g=q_norm_g, k_norm_g=k_norm_g, attn_sink=attn_sink, pool_w=pool_w, pool_scale=pool_scale,
                   w_branch=w_branch, w_gate=w_gate, b_gate=b_gate, w_out=w_out, router_w=router_w,
                   router_b=router_b, exp_w_gu=exp_w_gu, exp_b_gu=exp_b_gu, exp_w_down=exp_w_down,
                   exp_b_down=exp_b_down)
    kvw = N_KV * HEAD_DIM
    n_cond = 2 * SUBLANES
    cond = jnp.concatenate([c_ctx[None, :], c, jnp.zeros((n_cond - 1 - N_LAT_B, D), f32)], axis=0)
    mod = modulation(cond, w_mod, b_mod)
    tile_start = jnp.arange(T // TM) * TM
    tile_row = jnp.where(tile_start < T_CTX, 0, 1 + (tile_start - T_CTX) // N_LAT)
    h = jnp.concatenate([x_prompt.reshape(T_CTX, D), x_sample.reshape(T_LAT, D)], axis=0)
    ck = cache_k.reshape(N_LAT_B, DEPTH, PAST, kvw)
    cv = cache_v.reshape(N_LAT_B, DEPTH, PAST, kvw)
    rope_cos, rope_sin = rope_tables()
    zero_state = jnp.zeros((N_CTX_B, 2, D), f32)
    new_k, new_v, new_s = [], [], []
    for l in range(DEPTH):
        p = _layer_params(l, weights)
        mod_tiles = mod[l][tile_row].reshape(T // TM, 6, D)
        xn = norm1(h, p["norm1_g"], mod_tiles)
        u = in_proj(xn, p["w_in_bf"])
        lat_blk0 = T_CTX // N_LAT
        lru_c, st_c = lru_branch(u, p, zero_state, N_CTX_B, N_CTX, 0)
        lru_l, _ = lru_branch(u, p, state_lru[:, l], N_LAT_B, N_LAT, lat_blk0)
        conv = (conf_branch(u, p, N_CTX_B, N_CTX, 0), conf_branch(u, p, N_LAT_B, N_LAT, lat_blk0))
        pool = (pool_branch(u, p, N_CTX_B, N_CTX, 0), pool_branch(u, p, N_LAT_B, N_LAT, lat_blk0))
        att_c, k_l, v_l = ctx_attention(u, p)
        att_l = lat_attention(u, p, ck, cv, rope_cos, rope_sin)
        h = merge(h, xn, (lru_c, lru_l), conv, (att_c, att_l), pool, mod_tiles, p)
        h = moe_layer(h, mod_tiles, p)
        new_k.append(k_l.reshape(N_CTX_B, N_CTX, N_KV, HEAD_DIM))
        new_v.append(v_l.reshape(N_CTX_B, N_CTX, N_KV, HEAD_DIM))
        new_s.append(st_c)
    y_prompt = h[:T_CTX].reshape(N_CTX_B, N_CTX, D)
    y_sample = h[T_CTX:].reshape(N_LAT_B, N_LAT, D)
    return (y_prompt, y_sample, jnp.stack(new_k, axis=1), jnp.stack(new_v, axis=1), jnp.stack(new_s, axis=1))
```

```python
import functools

import jax
import jax.numpy as jnp
from jax import lax
from jax.experimental import pallas as pl
from jax.experimental.pallas import tpu as pltpu

f32 = jnp.float32
bf16 = jnp.bfloat16
i32 = jnp.int32

D = 1024
N_CTX_B, N_CTX = 16, 256
N_LAT_B, N_LAT = 8, 1024
T_CTX = N_CTX_B * N_CTX
T_LAT = N_LAT_B * N_LAT
T = T_CTX + T_LAT
DEPTH = 2
PAST = 512
GRID_W = 64
IN_W = 6656
COL_UX, COL_UY, COL_UC, COL_UQ, COL_UK, COL_UV, COL_UP = 0, 1024, 2048, 4096, 5120, 5376, 5632
LRU_BW = 64
LRU_CONV = 4
LRU_C = 8.0
CONV_K = 31
N_HEADS, N_KV, HEAD_DIM = 16, 4, 64
POOL_SIZES = (2, 4, 8, 16)
N_EXPERTS, TOP_K = 32, 4
SWIGLU_LIMIT, SWIGLU_ALPHA = 7.0, 1.702
EPS = 1e-6
NEG_INF = -1e30
ROPE_BASE = 10000.0
LOG2E = 1.4426950408889634

LANES = 128
SUBLANES = 8
VMEM_LIMIT = 56 * 1024 * 1024

CW = 256
RC = 128
TQ = 128
TM = 256
MOE_BM = 256
N_ASSIGN = T * TOP_K
MOE_NBLK = N_ASSIGN // MOE_BM + N_EXPERTS
MOE_CAP = MOE_NBLK * MOE_BM


def _sds(shape, dt=f32):
    return jax.ShapeDtypeStruct(shape, dt)


def _cparams(sem, vmem=VMEM_LIMIT):
    return pltpu.CompilerParams(dimension_semantics=sem, vmem_limit_bytes=vmem)


def _sigmoid(x):
    return 0.5 * jnp.tanh(0.5 * x) + 0.5


def _log1p(z):
    u = 1.0 + z
    d = u - 1.0
    return jnp.where(d == 0.0, z, jnp.log(u) * (z / jnp.where(d == 0.0, 1.0, d)))


def _cast_kernel(x_ref, o_ref):
    o_ref[...] = x_ref[...].astype(o_ref.dtype)


def cast_bf16(w, l):
    _, r, c = w.shape
    tr = 256
    return pl.pallas_call(
        _cast_kernel, out_shape=_sds((r, c), bf16), grid=(r // tr,),
        in_specs=[pl.BlockSpec((None, tr, c), lambda i: (l, i, 0))],
        out_specs=pl.BlockSpec((tr, c), lambda i: (i, 0)),
        compiler_params=_cparams(("parallel",)), name="cast_bf16")(w)


def _mod_kernel(c_ref, w_ref, b_ref, o_ref):
    x = c_ref[...]
    s = x * _sigmoid(x)
    o_ref[0] = jnp.dot(s.astype(bf16), w_ref[0].astype(bf16), preferred_element_type=f32) + b_ref[0]


def modulation(cond, w_mod, b_mod):
    r = cond.shape[0]
    tn = 1536
    return pl.pallas_call(
        _mod_kernel, out_shape=_sds((DEPTH, r, 6 * D)), grid=(DEPTH, 6 * D // tn),
        in_specs=[pl.BlockSpec((r, D), lambda l, j: (0, 0)),
                  pl.BlockSpec((1, D, tn), lambda l, j: (l, 0, j)),
                  pl.BlockSpec((1, 1, tn), lambda l, j: (l, 0, j))],
        out_specs=pl.BlockSpec((1, r, tn), lambda l, j: (l, 0, j)),
        compiler_params=_cparams(("parallel", "parallel")), name="modulation")(
            cond, w_mod, b_mod.reshape(DEPTH, 1, 6 * D))


def _rms_mod(x, g, shift, scale):
    ms = jnp.mean(x * x, axis=-1, keepdims=True)
    return (x * lax.rsqrt(ms + EPS) * g) * (1.0 + scale) + shift


def _norm1_kernel(h_ref, g_ref, mod_ref, o_ref):
    m = mod_ref[0]
    o_ref[...] = _rms_mod(h_ref[...], g_ref[...], m[0:1, :], m[1:2, :]).astype(bf16)


def norm1(h, g, mod_tiles):
    nt = h.shape[0] // TM
    return pl.pallas_call(
        _norm1_kernel, out_shape=_sds(h.shape, bf16), grid=(nt,),
        in_specs=[pl.BlockSpec((TM, D), lambda i: (i, 0)),
                  pl.BlockSpec((1, D), lambda i: (0, 0)),
                  pl.BlockSpec((1, 6, D), lambda i: (i, 0, 0))],
        out_specs=pl.BlockSpec((TM, D), lambda i: (i, 0)),
        compiler_params=_cparams(("parallel",)), name="norm1")(h, g.reshape(1, D), mod_tiles)


def _matmul_kernel(x_ref, w_ref, o_ref):
    o_ref[...] = jnp.dot(x_ref[...], w_ref[...], preferred_element_type=f32)


def in_proj(xn, w):
    t = xn.shape[0]
    tm, tn = 1024, 1664
    return pl.pallas_call(
        _matmul_kernel, out_shape=_sds((t, IN_W)), grid=(t // tm, IN_W // tn),
        in_specs=[pl.BlockSpec((tm, D), lambda i, j: (i, 0)),
                  pl.BlockSpec((D, tn), lambda i, j: (0, j))],
        out_specs=pl.BlockSpec((tm, tn), lambda i, j: (i, j)),
        compiler_params=_cparams(("parallel", "parallel")), name="in_proj")(xn, w)


def _gelu_tanh(x):
    return 0.5 * x * (1.0 + jnp.tanh(0.7978845608028654 * (x + 0.044715 * (x * x * x))))


def _lru_kernel(ux_ref, uy_ref, cw_ref, cb_ref, wr_ref, wi_ref, br_ref, bi_ref, lam_ref, h0_ref,
                out_ref, st_ref, xp_ref, a_ref, b_ref, hs_ref, *, n):
    pad = SUBLANES
    xp_ref[pl.ds(0, pad), :] = jnp.zeros((pad, CW), f32)
    xp_ref[pl.ds(n + pad, pad), :] = jnp.zeros((pad, CW), f32)
    xp_ref[pl.ds(pad, n), :] = ux_ref[...]
    left = LRU_CONV // 2
    lam = lam_ref[...]
    neg_c_sp = -LRU_C * (jnp.maximum(-lam, 0.0) + _log1p(jnp.exp(-jnp.abs(lam))))

    def coef_body(i, c):
        r0 = pl.multiple_of(i * RC, RC)
        w = xp_ref[pl.ds(r0, RC + 2 * pad), :]
        xc = jnp.zeros((RC, CW), f32) + cb_ref[...]
        for k in range(LRU_CONV):
            off = pad - left + k
            xc = xc + cw_ref[k:k + 1, :] * w[off:off + RC, :]
        xcb = xc.astype(bf16)
        for d in range(2):
            r = _sigmoid(jnp.dot(xcb, wr_ref[d, 0], preferred_element_type=f32) + br_ref[d:d + 1, :])
            g = _sigmoid(jnp.dot(xcb, wi_ref[d, 0], preferred_element_type=f32) + bi_ref[d:d + 1, :])
            log_a = r * neg_c_sp[d:d + 1, :]
            a = jnp.exp(log_a)
            th = jnp.tanh(log_a)
            one_m_a2 = (-2.0 * th) / (1.0 - th)
            a_ref[d, pl.ds(r0, RC), :] = a
            b_ref[d, pl.ds(r0, RC), :] = jnp.sqrt(one_m_a2) * (g * xc)
        return c

    lax.fori_loop(0, n // RC, coef_body, 0)

    row = lax.broadcasted_iota(i32, (SUBLANES, CW), 0)
    nchunk = n // SUBLANES

    def scan_body(j, carry):
        hf, hb = carry
        rf = pl.multiple_of(j * SUBLANES, SUBLANES)
        a = a_ref[0, pl.ds(rf, SUBLANES), :]
        b = b_ref[0, pl.ds(rf, SUBLANES), :]
        for sh in (1, 2, 4):
            a_s = jnp.where(row >= sh, pltpu.roll(a, sh, axis=0), 1.0)
            b_s = jnp.where(row >= sh, pltpu.roll(b, sh, axis=0), 0.0)
            b = a * b_s + b
            a = a * a_s
        h = a * hf + b
        hs_ref[0, pl.ds(rf, SUBLANES), :] = h
        hf = h[SUBLANES - 1:SUBLANES, :]
        rb = pl.multiple_of((nchunk - 1 - j) * SUBLANES, SUBLANES)
        a = a_ref[1, pl.ds(rb, SUBLANES), :]
        b = b_ref[1, pl.ds(rb, SUBLANES), :]
        for sh in (1, 2, 4):
            keep = row < SUBLANES - sh
            a_s = jnp.where(keep, pltpu.roll(a, SUBLANES - sh, axis=0), 1.0)
            b_s = jnp.where(keep, pltpu.roll(b, SUBLANES - sh, axis=0), 0.0)
            b = a * b_s + b
            a = a * a_s
        h = a * hb + b
        hs_ref[1, pl.ds(rb, SUBLANES), :] = h
        hb = h[0:1, :]
        return hf, hb

    h0 = h0_ref[0]
    hf, hb = lax.fori_loop(0, nchunk, scan_body, (h0[0:1, :], h0[1:2, :]), unroll=4)
    st_ref[0, 0:1, :] = hf
    st_ref[0, 1:2, :] = hb

    def out_body(i, c):
        r0 = pl.multiple_of(i * RC, RC)
        hsum = hs_ref[0, pl.ds(r0, RC), :] + hs_ref[1, pl.ds(r0, RC), :]
        out_ref[pl.ds(r0, RC), :] = hsum * _gelu_tanh(uy_ref[pl.ds(r0, RC), :])
        return c

    lax.fori_loop(0, n // RC, out_body, 0)


def lru_branch(u, p, h0, nb, n, row_blk0):
    nct = D // CW
    cu = COL_UY // CW
    kern = functools.partial(_lru_kernel, n=n)
    vec = lambda a: pl.BlockSpec((a, CW), lambda b, c: (0, c))
    call = pl.pallas_call(
        kern, out_shape=(_sds((nb * n, D)), _sds((nb, 2, D))), grid=(nb, nct),
        in_specs=[pl.BlockSpec((n, CW), lambda b, c: (row_blk0 + b, c)),
                  pl.BlockSpec((n, CW), lambda b, c: (row_blk0 + b, cu + c)),
                  vec(LRU_CONV), vec(1),
                  pl.BlockSpec((2, 1, CW, CW), lambda b, c: (0, c, 0, 0)),
                  pl.BlockSpec((2, 1, CW, CW), lambda b, c: (0, c, 0, 0)),
                  vec(2), vec(2), vec(2),
                  pl.BlockSpec((1, 2, CW), lambda b, c: (b, 0, c))],
        out_specs=(pl.BlockSpec((n, CW), lambda b, c: (b, c)),
                   pl.BlockSpec((1, 2, CW), lambda b, c: (b, 0, c))),
        scratch_shapes=[pltpu.VMEM((n + 2 * SUBLANES, CW), f32),
                        pltpu.VMEM((2, n, CW), f32), pltpu.VMEM((2, n, CW), f32), pltpu.VMEM((2, n, CW), f32)],
        compiler_params=_cparams(("parallel", "parallel")), name="lru_branch")
    return call(u, u, p["lru_conv_w"], p["lru_conv_b"].reshape(1, D), p["lru_wr_bd"], p["lru_wi_bd"],
                p["lru_b_r"], p["lru_b_i"], p["lru_lambda"], h0)


def _conf_kernel(ua_ref, ug_ref, w_ref, b_ref, o_ref, gp_ref, *, n):
    pad = 2 * SUBLANES
    left = CONV_K // 2
    gp_ref[pl.ds(0, pad), :] = jnp.zeros((pad, CW), f32)
    gp_ref[pl.ds(n + pad, pad), :] = jnp.zeros((pad, CW), f32)
    gp_ref[pl.ds(pad, n), :] = ua_ref[...] * _sigmoid(ug_ref[...])

    wn = RC + 2 * pad

    def body(i, c):
        r0 = pl.multiple_of(i * RC, RC)
        w = gp_ref[pl.ds(r0, wn), :]
        rolled = [w] + [pltpu.roll(w, wn - m, axis=0) for m in range(1, SUBLANES)]
        acc = jnp.zeros((RC, CW), f32) + b_ref[...]
        for k in range(CONV_K):
            off = pad - left + k
            q, m = off // SUBLANES, off % SUBLANES
            acc = acc + w_ref[k:k + 1, :] * rolled[m][q * SUBLANES:q * SUBLANES + RC, :]
        o_ref[pl.ds(r0, RC), :] = acc
        return c

    lax.fori_loop(0, n // RC, body, 0)


def conf_branch(u, p, nb, n, row_blk0):
    nct = D // CW
    ca, cg = COL_UC // CW, (COL_UC + D) // CW
    kern = functools.partial(_conf_kernel, n=n)
    call = pl.pallas_call(
        kern, out_shape=_sds((nb * n, D)), grid=(nb, nct),
        in_specs=[pl.BlockSpec((n, CW), lambda b, c: (row_blk0 + b, ca + c)),
                  pl.BlockSpec((n, CW), lambda b, c: (row_blk0 + b, cg + c)),
                  pl.BlockSpec((CONV_K, CW), lambda b, c: (0, c)),
                  pl.BlockSpec((1, CW), lambda b, c: (0, c))],
        out_specs=pl.BlockSpec((n, CW), lambda b, c: (b, c)),
        scratch_shapes=[pltpu.VMEM((n + 4 * SUBLANES, CW), f32)],
        compiler_params=_cparams(("parallel", "parallel")), name="conf_branch")
    return call(u, u, p["conv_dw_w"], p["conv_dw_b"].reshape(1, D))


def _pool_kernel(up_ref, w_ref, s_ref, o_ref, xp_ref, *, n):
    pad = SUBLANES
    gi = pl.program_id(1)
    half = jnp.left_shift(1, gi)
    xp_ref[pl.ds(0, pad), :] = jnp.zeros((pad, CW), f32)
    xp_ref[pl.ds(n + pad, pad), :] = jnp.zeros((pad, CW), f32)
    xp_ref[pl.ds(pad, n), :] = up_ref[...]
    wn = RC + 2 * pad

    def body(i, c):
        r0 = pl.multiple_of(i * RC, RC)
        w = xp_ref[pl.ds(r0, wn), :]
        s2 = w + pltpu.roll(w, 1, axis=0)
        s4 = pltpu.roll(s2, 1, axis=0) + pltpu.roll(s2, wn - 1, axis=0)
        s8 = pltpu.roll(s4, 2, axis=0) + pltpu.roll(s4, wn - 2, axis=0)
        s16 = pltpu.roll(s8, 4, axis=0) + pltpu.roll(s8, wn - 4, axis=0)
        s = jnp.where(gi == 0, s2, jnp.where(gi == 1, s4, jnp.where(gi == 2, s8, s16)))[pad:pad + RC, :]
        t = r0 + lax.broadcasted_iota(i32, (RC, CW), 0)
        cnt = (jnp.minimum(t + half, n) - jnp.maximum(t - half, 0)).astype(f32)
        pooled = s / cnt - w[pad:pad + RC, :]
        o_ref[pl.ds(r0, RC), :] = jnp.dot(pooled.astype(bf16), w_ref[0], preferred_element_type=f32) * s_ref[...]
        return c

    lax.fori_loop(0, n // RC, body, 0)


def pool_branch(u, p, nb, n, row_blk0):
    assert POOL_SIZES == (2, 4, 8, 16) and D // len(POOL_SIZES) == CW
    cp = COL_UP // CW
    kern = functools.partial(_pool_kernel, n=n)
    call = pl.pallas_call(
        kern, out_shape=_sds((nb * n, D)), grid=(nb, len(POOL_SIZES)),
        in_specs=[pl.BlockSpec((n, CW), lambda b, c: (row_blk0 + b, cp + c)),
                  pl.BlockSpec((1, CW, CW), lambda b, c: (c, 0, 0)),
                  pl.BlockSpec((1, CW), lambda b, c: (0, c))],
        out_specs=pl.BlockSpec((n, CW), lambda b, c: (b, c)),
        scratch_shapes=[pltpu.VMEM((n + 2 * SUBLANES, CW), f32)],
        compiler_params=_cparams(("parallel", "parallel")), name="pool_branch")
    return call(u, p["pool_w_bf"], p["pool_scale"].reshape(1, D))


def _head_norm(blk, g128, lane_lo):
    sq = blk * blk
    s_lo = jnp.sum(jnp.where(lane_lo, sq, 0.0), axis=-1, keepdims=True)
    s_hi = jnp.sum(jnp.where(lane_lo, 0.0, sq), axis=-1, keepdims=True)
    r = jnp.where(lane_lo, lax.rsqrt(s_lo * (1.0 / HEAD_DIM) + EPS), lax.rsqrt(s_hi * (1.0 / HEAD_DIM) + EPS))
    return blk * r * g128


def _rope(blk, cos, sin_signed, lane):
    partner = jnp.where((lane % 32) < 16, pltpu.roll(blk, LANES - 16, axis=1), pltpu.roll(blk, 16, axis=1))
    return blk * cos + partner * sin_signed


def _prep_kv(k_ref, v_ref, ks_ref, vs_ref, kg, lane_lo, rope=None, newk_ref=None, newv_ref=None):
    nk = k_ref.shape[0]
    lane = lax.broadcasted_iota(i32, (nk, LANES), 1)
    for tj in range(N_KV * HEAD_DIM // LANES):
        kt = _head_norm(k_ref[:, tj * LANES:(tj + 1) * LANES], kg, lane_lo(nk))
        if rope is not None:
            kt = _rope(kt, rope[0][...], rope[1][...], lane)
        vt = v_ref[:, tj * LANES:(tj + 1) * LANES]
        if newk_ref is not None:
            newk_ref[0, :, tj * LANES:(tj + 1) * LANES] = kt
            newv_ref[0, :, tj * LANES:(tj + 1) * LANES] = vt
        _store_low_half(kt, vt, ks_ref, vs_ref, tj, lane_lo(nk))


def _store_low_half(kt, vt, ks_ref, vs_ref, tj, lane_lo):
    ks_ref[2 * tj] = jnp.where(lane_lo, kt, 0.0).astype(bf16)
    vs_ref[2 * tj] = jnp.where(lane_lo, vt, 0.0).astype(bf16)
    ks_ref[2 * tj + 1] = jnp.where(lane_lo, pltpu.roll(kt, HEAD_DIM, axis=1), 0.0).astype(bf16)
    vs_ref[2 * tj + 1] = jnp.where(lane_lo, pltpu.roll(vt, HEAD_DIM, axis=1), 0.0).astype(bf16)


def _prep_cache(ck_ref, cv_ref, kc_ref, vc_ref, lane_lo):
    nk = ck_ref.shape[2]
    for tj in range(N_KV * HEAD_DIM // LANES):
        kt = ck_ref[0, 0, :, tj * LANES:(tj + 1) * LANES]
        vt = cv_ref[0, 0, :, tj * LANES:(tj + 1) * LANES]
        _store_low_half(kt, vt, kc_ref, vc_ref, tj, lane_lo(nk))


def _qk(qm, k):
    return lax.dot_general(qm, k, (((1,), (1,)), ((), ())), preferred_element_type=f32)


def _attend(sink_ref, q_tiles, segments, att_ref):
    per_kv = N_HEADS // N_KV
    assert per_kv == 4
    rows = per_kv * TQ
    row = lax.broadcasted_iota(i32, (rows, 1), 0)
    masks = [None if mk is None else jnp.concatenate([mk] * per_kv, axis=0) for _, _, mk in segments]
    for g in range(N_KV):
        t0, t1 = q_tiles[2 * g], q_tiles[2 * g + 1]
        qm = jnp.concatenate([t0, t1, pltpu.roll(t0, HEAD_DIM, axis=1), pltpu.roll(t1, HEAD_DIM, axis=1)],
                             axis=0).astype(bf16)
        hd = [per_kv * g, per_kv * g + 2, per_kv * g + 1, per_kv * g + 3]
        sink = jnp.where(row < TQ, sink_ref[hd[0]],
                         jnp.where(row < 2 * TQ, sink_ref[hd[1]],
                                   jnp.where(row < 3 * TQ, sink_ref[hd[2]], sink_ref[hd[3]]))) * LOG2E
        scores = []
        m_el = None
        for (kget, _, _), mask in zip(segments, masks):
            s = _qk(qm, kget(g))
            if mask is not None:
                s = jnp.where(mask, s, NEG_INF)
            for c in range(s.shape[1] // LANES):
                t = s[:, c * LANES:(c + 1) * LANES]
                m_el = t if m_el is None else jnp.maximum(m_el, t)
            scores.append(s)
        m = jnp.maximum(jnp.max(m_el, axis=-1, keepdims=True), sink)
        d_el = jnp.zeros((rows, LANES), f32)
        o = jnp.zeros((rows, LANES), f32)
        for s, (_, vget, _) in zip(scores, segments):
            pr = jnp.exp2(s - m)
            for c in range(s.shape[1] // LANES):
                d_el = d_el + pr[:, c * LANES:(c + 1) * LANES]
            o = o + jnp.dot(pr.astype(bf16), vget(g), preferred_element_type=f32)
        den = jnp.exp2(sink - m) + jnp.sum(d_el, axis=-1, keepdims=True)
        o = o / den
        att_ref[:, 2 * g * LANES:(2 * g + 1) * LANES] = o[:TQ] + pltpu.roll(o[2 * TQ:3 * TQ], HEAD_DIM, axis=1)
        att_ref[:, (2 * g + 1) * LANES:(2 * g + 2) * LANES] = o[TQ:2 * TQ] + pltpu.roll(o[3 * TQ:], HEAD_DIM, axis=1)


def _q_tiles(q_ref, qg, rope=None):
    lane = lax.broadcasted_iota(i32, (TQ, LANES), 1)
    lane_lo = lane < HEAD_DIM
    tiles = []
    for j in range(N_HEADS // 2):
        qt = _head_norm(q_ref[:, j * LANES:(j + 1) * LANES], qg, lane_lo)
        if rope is not None:
            qt = _rope(qt, rope[0], rope[1], lane)
        tiles.append(qt * (HEAD_DIM ** -0.5 * LOG2E))
    return tiles


def _lane_lo_fn(nrows):
    return lax.broadcasted_iota(i32, (nrows, LANES), 1) < HEAD_DIM


def _ctx_attn_kernel(sink_ref, q_ref, k_ref, v_ref, qg_ref, kg_ref, att_ref, newk_ref, newv_ref, ks_ref, vs_ref):
    @pl.when(pl.program_id(1) == 0)
    def _():
        _prep_kv(k_ref, v_ref, ks_ref, vs_ref, kg_ref[...], _lane_lo_fn, None, newk_ref, newv_ref)

    tiles = _q_tiles(q_ref, qg_ref[...])
    seg = [(lambda g: ks_ref[g], lambda g: vs_ref[g], None)]
    _attend(sink_ref, tiles, seg, att_ref)


def ctx_attention(u, p):
    nq = N_CTX // TQ
    kvw = N_KV * HEAD_DIM
    return pl.pallas_call(
        _ctx_attn_kernel,
        out_shape=(_sds((T_CTX, D)), _sds((N_CTX_B, N_CTX, kvw)), _sds((N_CTX_B, N_CTX, kvw))),
        grid=(N_CTX_B, nq),
        in_specs=[pl.BlockSpec(memory_space=pltpu.SMEM),
                  pl.BlockSpec((TQ, D), lambda b, i: (b * nq + i, COL_UQ // D)),
                  pl.BlockSpec((N_CTX, kvw), lambda b, i: (b, COL_UK // kvw)),
                  pl.BlockSpec((N_CTX, kvw), lambda b, i: (b, COL_UV // kvw)),
                  pl.BlockSpec((1, LANES), lambda b, i: (0, 0)),
                  pl.BlockSpec((1, LANES), lambda b, i: (0, 0))],
        out_specs=(pl.BlockSpec((TQ, D), lambda b, i: (b * nq + i, 0)),
                   pl.BlockSpec((1, N_CTX, kvw), lambda b, i: (b, 0, 0)),
                   pl.BlockSpec((1, N_CTX, kvw), lambda b, i: (b, 0, 0))),
        scratch_shapes=[pltpu.VMEM((N_KV, N_CTX, LANES), bf16), pltpu.VMEM((N_KV, N_CTX, LANES), bf16)],
        compiler_params=_cparams(("parallel", "arbitrary")), name="ctx_attention")(
            p["attn_sink"], u, u, u, p["q_norm_g2"], p["k_norm_g2"])


def _lat_attn_kernel(sink_ref, q_ref, k_ref, v_ref, ck_ref, cv_ref, cosq_ref, sinq_ref, cosk_ref, sink_tab_ref,
                     qg_ref, kg_ref, att_ref, ks_ref, vs_ref, kc_ref, vc_ref):
    i = pl.program_id(1)
    nq = N_LAT // TQ

    @pl.when(i == 0)
    def _():
        _prep_kv(k_ref, v_ref, ks_ref, vs_ref, kg_ref[...], _lane_lo_fn, (cosk_ref, sink_tab_ref))
        _prep_cache(ck_ref, cv_ref, kc_ref, vc_ref, _lane_lo_fn)

    tiles = _q_tiles(q_ref, qg_ref[...], (cosq_ref[...], sinq_ref[...]))
    r = lax.broadcasted_iota(i32, (TQ, TQ), 0)
    c = lax.broadcasted_iota(i32, (TQ, TQ), 1)
    prev0 = pl.multiple_of(jnp.maximum(i - 1, 0) * TQ, TQ)
    cur0 = pl.multiple_of(i * TQ, TQ)
    next0 = pl.multiple_of(jnp.minimum(i + 1, nq - 1) * TQ, TQ)
    mask_prev = jnp.logical_and(c >= r, i > 0)
    mask_next = jnp.logical_and(c <= r, i < nq - 1)

    def seg(r0, mask):
        return (lambda g: ks_ref[g, pl.ds(r0, TQ), :], lambda g: vs_ref[g, pl.ds(r0, TQ), :], mask)

    segments = [seg(prev0, mask_prev), seg(cur0, None), seg(next0, mask_next),
                (lambda g: kc_ref[g], lambda g: vc_ref[g], None)]
    _attend(sink_ref, tiles, segments, att_ref)


def lat_attention(u, p, cache_k, cache_v, rope_cos, rope_sin):
    nq = N_LAT // TQ
    kvw = N_KV * HEAD_DIM
    l = p["layer"]
    qblk0 = T_CTX // TQ
    kblk0 = T_CTX // N_LAT
    call = pl.pallas_call(
        _lat_attn_kernel, out_shape=_sds((T_LAT, D)), grid=(N_LAT_B, nq),
        in_specs=[pl.BlockSpec(memory_space=pltpu.SMEM),
                  pl.BlockSpec((TQ, D), lambda b, i: (qblk0 + b * nq + i, COL_UQ // D)),
                  pl.BlockSpec((N_LAT, kvw), lambda b, i: (kblk0 + b, COL_UK // kvw)),
                  pl.BlockSpec((N_LAT, kvw), lambda b, i: (kblk0 + b, COL_UV // kvw)),
                  pl.BlockSpec((1, 1, PAST, kvw), lambda b, i: (b, l, 0, 0)),
                  pl.BlockSpec((1, 1, PAST, kvw), lambda b, i: (b, l, 0, 0)),
                  pl.BlockSpec((TQ, LANES), lambda b, i: (i, 0)),
                  pl.BlockSpec((TQ, LANES), lambda b, i: (i, 0)),
                  pl.BlockSpec((N_LAT, LANES), lambda b, i: (0, 0)),
                  pl.BlockSpec((N_LAT, LANES), lambda b, i: (0, 0)),
                  pl.BlockSpec((1, LANES), lambda b, i: (0, 0)),
                  pl.BlockSpec((1, LANES), lambda b, i: (0, 0))],
        out_specs=pl.BlockSpec((TQ, D), lambda b, i: (b * nq + i, 0)),
        scratch_shapes=[pltpu.VMEM((N_KV, N_LAT, LANES), bf16), pltpu.VMEM((N_KV, N_LAT, LANES), bf16),
                        pltpu.VMEM((N_KV, PAST, LANES), bf16), pltpu.VMEM((N_KV, PAST, LANES), bf16)],
        compiler_params=_cparams(("parallel", "arbitrary")), name="lat_attention")
    return call(p["attn_sink"], u, u, u, cache_k, cache_v, rope_cos, rope_sin, rope_cos, rope_sin,
                p["q_norm_g2"], p["k_norm_g2"])


def rope_tables():
    pos = jnp.arange(N_LAT)
    rows = (pos // GRID_W).astype(f32)
    cols = (pos % GRID_W).astype(f32)
    half = HEAD_DIM // 2
    freqs = ROPE_BASE ** (-jnp.arange(0, half, 2, dtype=f32) / half)
    lane = jnp.arange(LANES)
    within = lane % half
    fidx = within % (half // 2)
    use_cols = (lane % HEAD_DIM) >= half
    ang = jnp.where(use_cols[None, :], cols[:, None], rows[:, None]) * freqs[fidx][None, :]
    sign = jnp.where(within < half // 2, -1.0, 1.0)
    return jnp.cos(ang), jnp.sin(ang) * sign[None, :]


def _merge_kernel(h_ref, xn_ref, lru_c, lru_l, conv_c, conv_l, att_c, att_l, pool_c, pool_l, mod_ref, lng_ref, lnb_ref,
                  wg_ref, bg_ref, wb_ref, wo_ref, o_ref):
    is_ctx = pl.program_id(0) < T_CTX // TM
    pick = lambda c_ref, l_ref: jnp.where(is_ctx, c_ref[...], l_ref[...])
    xn = xn_ref[...]
    hc = pick(conv_c, conv_l)
    mu = jnp.mean(hc, axis=-1, keepdims=True)
    xc = hc - mu
    var = jnp.mean(xc * xc, axis=-1, keepdims=True)
    y = xc * lax.rsqrt(var + EPS) * lng_ref[...] + lnb_ref[...]
    conv = y * _sigmoid(y)
    merged = jnp.zeros((TM, D), f32)
    for j, br in enumerate((pick(lru_c, lru_l), conv, pick(att_c, att_l), pick(pool_c, pool_l))):
        gate = _sigmoid(jnp.dot(xn, wg_ref[:, j * D:(j + 1) * D], preferred_element_type=f32)
                        + bg_ref[:, j * D:(j + 1) * D])
        proj = jnp.dot(br.astype(bf16), wb_ref[j], preferred_element_type=f32)
        merged = merged + gate * proj
    out = jnp.dot(merged.astype(bf16), wo_ref[...], preferred_element_type=f32)
    o_ref[...] = h_ref[...] + mod_ref[0][2:3, :] * out


def merge(h, xn, lru, conv, att, pool, mod_tiles, p):
    nt = h.shape[0] // TM
    n_ctx_tiles = T_CTX // TM
    tok = pl.BlockSpec((TM, D), lambda i: (i, 0))
    tok_c = pl.BlockSpec((TM, D), lambda i: (jnp.minimum(i, n_ctx_tiles - 1), 0))
    tok_l = pl.BlockSpec((TM, D), lambda i: (jnp.maximum(i - n_ctx_tiles, 0), 0))
    const = lambda shape: pl.BlockSpec(shape, lambda i: (0,) * len(shape), pipeline_mode=pl.Buffered(1))
    return pl.pallas_call(
        _merge_kernel, out_shape=_sds(h.shape), grid=(nt,),
        in_specs=[tok, tok, tok_c, tok_l, tok_c, tok_l, tok_c, tok_l, tok_c, tok_l,
                  pl.BlockSpec((1, 6, D), lambda i: (i, 0, 0)),
                  const((1, D)), const((1, D)),
                  const((D, 4 * D)), const((1, 4 * D)), const((4, D, D)), const((D, D))],
        out_specs=tok,
        compiler_params=_cparams(("parallel",)), name="merge")(
            h, xn, *lru, *conv, *att, *pool, mod_tiles, p["conv_ln_g"].reshape(1, D), p["conv_ln_b"].reshape(1, D),
            p["w_gate_bf"], p["b_gate"].reshape(1, 4 * D), p["w_branch_bf"], p["w_out_bf"])


def _router_kernel(h_ref, g_ref, mod_ref, rw_ref, rb_ref, xn_ref, topi_ref, topw_ref, rank_ref, cnt_ref, carry_ref):
    step = pl.program_id(0)

    @pl.when(step == 0)
    def _():
        carry_ref[...] = jnp.zeros_like(carry_ref)

    m = mod_ref[0]
    xn = _rms_mod(h_ref[...], g_ref[...], m[3:4, :], m[4:5, :])
    xn_ref[...] = xn
    logits = jnp.dot(xn, rw_ref[...], preferred_element_type=f32, precision=lax.Precision.HIGHEST) + rb_ref[...]
    lane = lax.broadcasted_iota(i32, (TM, LANES), 1).astype(f32)
    lg = jnp.where(lane < N_EXPERTS, logits, NEG_INF)
    vals, idxs = [], []
    for _ in range(TOP_K):
        mx = jnp.max(lg, axis=-1, keepdims=True)
        idx = jnp.min(jnp.where(lg == mx, lane, float(LANES)), axis=-1, keepdims=True)
        vals.append(mx)
        idxs.append(idx)
        lg = jnp.where(lane == idx, -3e38, lg)
    exps = [jnp.exp(v - vals[0]) for v in vals]
    den = exps[0] + exps[1] + exps[2] + exps[3]
    cnt = jnp.zeros((TM, LANES), f32)
    for idx in idxs:
        cnt = cnt + jnp.where(lane == idx, 1.0, 0.0)
    rr = lax.broadcasted_iota(i32, (TM, TM), 0)
    cc = lax.broadcasted_iota(i32, (TM, TM), 1)
    tri = jnp.where(rr > cc, 1.0, 0.0).astype(bf16)
    before = jnp.dot(tri, cnt.astype(bf16), preferred_element_type=f32) + carry_ref[0:1, :]
    topi = jnp.zeros((TM, LANES), f32)
    topw = jnp.zeros((TM, LANES), f32)
    rank = jnp.zeros((TM, LANES), f32)
    for k in range(TOP_K):
        rk = jnp.sum(jnp.where(lane == idxs[k], before, 0.0), axis=-1, keepdims=True)
        topi = jnp.where(lane == k, idxs[k], topi)
        topw = jnp.where(lane == k, exps[k] / den, topw)
        rank = jnp.where(lane == k, rk, rank)
    topi_ref[...] = topi.astype(i32)
    topw_ref[...] = topw
    rank_ref[...] = rank.astype(i32)
    total = carry_ref[0:1, :] + jnp.sum(cnt, axis=0, keepdims=True)
    carry_ref[...] = jnp.broadcast_to(total, carry_ref.shape)
    cnt_ref[...] = jnp.broadcast_to(total, cnt_ref.shape).astype(i32)


def router(h, mod_tiles, p):
    nt = h.shape[0] // TM
    tok = lambda w, dt=f32: pl.BlockSpec((TM, w), lambda i: (i, 0))
    t = h.shape[0]
    return pl.pallas_call(
        _router_kernel,
        out_shape=(_sds((t, D)), _sds((t, LANES), i32), _sds((t, LANES)), _sds((t, LANES), i32),
                   _sds((SUBLANES, LANES), i32)),
        grid=(nt,),
        in_specs=[tok(D), pl.BlockSpec((1, D), lambda i: (0, 0)), pl.BlockSpec((1, 6, D), lambda i: (i, 0, 0)),
                  pl.BlockSpec((D, LANES), lambda i: (0, 0)), pl.BlockSpec((1, LANES), lambda i: (0, 0))],
        out_specs=(tok(D), tok(LANES), tok(LANES), tok(LANES), pl.BlockSpec((SUBLANES, LANES), lambda i: (0, 0))),
        scratch_shapes=[pltpu.VMEM((SUBLANES, LANES), f32)],
        compiler_params=_cparams(("arbitrary",)), name="router")(
            h, p["norm2_g"].reshape(1, D), mod_tiles, p["router_w_pad"], p["router_b_pad"])


def _dest_kernel(topi_ref, rank_ref, pstart_ref, o_ref):
    lane = lax.broadcasted_iota(i32, (TM, LANES), 1).astype(f32)
    topi = topi_ref[...].astype(f32)
    pstart = pstart_ref[...].astype(f32)
    dest = rank_ref[...].astype(f32)
    for k in range(TOP_K):
        e = jnp.sum(jnp.where(lane == k, topi, 0.0), axis=-1, keepdims=True)
        ps = jnp.sum(jnp.where(lane == e, pstart, 0.0), axis=-1, keepdims=True)
        dest = dest + jnp.where(lane == k, ps, 0.0)
    o_ref[...] = dest.astype(i32)


def dest_rows(topi, rank, pstart):
    t = topi.shape[0]
    tok = pl.BlockSpec((TM, LANES), lambda i: (i, 0))
    return pl.pallas_call(
        _dest_kernel, out_shape=_sds((t, LANES), i32), grid=(t // TM,),
        in_specs=[tok, tok, pl.BlockSpec((1, LANES), lambda i: (0, 0))], out_specs=tok,
        compiler_params=_cparams(("parallel",)), name="dest_rows")(topi, rank, pstart)


def _row_copy(src_ref, s, dst_ref, d, sem):
    return pltpu.make_async_copy(src_ref.at[pl.ds(s, 1), :], dst_ref.at[pl.ds(d, 1), :], sem)


def _dispatch_kernel(zflag_ref, dest_ref, x_ref, xb_ref, zero_ref, sem):
    @pl.when(pl.program_id(0) == 0)
    def _():
        zero_ref[...] = jnp.zeros_like(zero_ref)

        def zcopy(j):
            r0 = pl.multiple_of(j * MOE_BM, MOE_BM)
            return pltpu.make_async_copy(zero_ref, xb_ref.at[pl.ds(r0, MOE_BM), :], sem)

        def zstart(j, c):
            @pl.when(zflag_ref[j] != 0)
            def _():
                zcopy(j).start()
            return c

        def zwait(j, c):
            @pl.when(zflag_ref[j] != 0)
            def _():
                zcopy(j).wait()
            return c

        lax.fori_loop(0, MOE_NBLK, zstart, 0)
        lax.fori_loop(0, MOE_NBLK, zwait, 0)

    def start(t, c):
        for k in range(TOP_K):
            _row_copy(x_ref, t, xb_ref, dest_ref[t * TOP_K + k], sem).start(priority=k % 2)
        return c

    lax.fori_loop(0, TM, start, 0)

    def wait(t, c):
        for k in range(TOP_K):
            _row_copy(x_ref, 0, xb_ref, 0, sem).wait()
        return c

    lax.fori_loop(0, TM, wait, 0)


def dispatch(zflag, dest_flat, xn):
    t = xn.shape[0]
    grid_spec = pltpu.PrefetchScalarGridSpec(
        num_scalar_prefetch=1, grid=(t // TM,),
        in_specs=[pl.BlockSpec((TM * TOP_K,), lambda i, zb: (i,), memory_space=pltpu.SMEM),
                  pl.BlockSpec((TM, D), lambda i, zb: (i, 0))],
        out_specs=pl.BlockSpec(memory_space=pl.ANY),
        scratch_shapes=[pltpu.VMEM((MOE_BM, D), f32), pltpu.SemaphoreType.DMA(())])
    return pl.pallas_call(
        _dispatch_kernel, out_shape=_sds((MOE_CAP, D)), grid_spec=grid_spec,
        compiler_params=_cparams(("arbitrary",)), name="moe_dispatch")(zflag, dest_flat, xn)


def _expert_kernel(be_ref, nused_ref, xb_ref, wgu_ref, bgu_ref, wdn_ref, bdn_ref, yb_ref, wgu_bf, wdn_bf):
    i = pl.program_id(0)
    e = be_ref[i]
    prev = be_ref[jnp.maximum(i - 1, 0)]

    @pl.when(jnp.logical_or(i == 0, e != prev))
    def _():
        wgu_bf[...] = wgu_ref[0, 0].astype(bf16)
        wdn_bf[...] = wdn_ref[0, 0].astype(bf16)

    @pl.when(i < nused_ref[0])
    def _():
        x = xb_ref[...].astype(bf16)
        h = jnp.dot(x, wgu_bf[...], preferred_element_type=f32) + bgu_ref[0, 0]
        gate = jnp.minimum(h[:, :D], SWIGLU_LIMIT)
        lin = jnp.clip(h[:, D:], -SWIGLU_LIMIT, SWIGLU_LIMIT)
        act = (lin + 1.0) * (gate * _sigmoid(SWIGLU_ALPHA * gate))
        yb_ref[...] = jnp.dot(act.astype(bf16), wdn_bf[...], preferred_element_type=f32) + bdn_ref[0, 0]

    @pl.when(i >= nused_ref[0])
    def _():
        yb_ref[...] = jnp.zeros_like(yb_ref)


def experts(block_e, nused, xb, p):
    l = p["layer"]
    grid_spec = pltpu.PrefetchScalarGridSpec(
        num_scalar_prefetch=2, grid=(MOE_NBLK,),
        in_specs=[pl.BlockSpec((MOE_BM, D), lambda i, be, nu: (jnp.maximum(jnp.minimum(i, nu[0] - 1), 0), 0)),
                  pl.BlockSpec((1, 1, D, 2 * D), lambda i, be, nu: (l, be[i], 0, 0)),
                  pl.BlockSpec((1, 1, 1, 2 * D), lambda i, be, nu: (l, be[i], 0, 0)),
                  pl.BlockSpec((1, 1, D, D), lambda i, be, nu: (l, be[i], 0, 0)),
                  pl.BlockSpec((1, 1, 1, D), lambda i, be, nu: (l, be[i], 0, 0))],
        out_specs=pl.BlockSpec((MOE_BM, D), lambda i, be, nu: (i, 0)),
        scratch_shapes=[pltpu.VMEM((D, 2 * D), bf16), pltpu.VMEM((D, D), bf16)])
    return pl.pallas_call(
        _expert_kernel, out_shape=_sds((MOE_CAP, D)), grid_spec=grid_spec,
        compiler_params=_cparams(("arbitrary",)), name="moe_experts")(
            block_e, nused, xb, p["exp_w_gu"], p["exp_b_gu"].reshape(DEPTH, N_EXPERTS, 1, 2 * D),
            p["exp_w_down"], p["exp_b_down"].reshape(DEPTH, N_EXPERTS, 1, D))


def _combine_kernel(dest_ref, yb_ref, topw_ref, h_ref, mod_ref, o_ref, buf, sem):
    def start(t, c):
        for k in range(TOP_K):
            pltpu.make_async_copy(yb_ref.at[pl.ds(dest_ref[t * TOP_K + k], 1), :], buf.at[k, pl.ds(t, 1), :],
                                  sem).start(priority=k % 2)
        return c

    lax.fori_loop(0, TM, start, 0)

    def wait(t, c):
        for k in range(TOP_K):
            pltpu.make_async_copy(yb_ref.at[pl.ds(0, 1), :], buf.at[k, pl.ds(0, 1), :], sem).wait()
        return c

    lax.fori_loop(0, TM, wait, 0)
    w = topw_ref[...]
    y = jnp.zeros((TM, D), f32)
    for k in range(TOP_K):
        y = y + buf[k] * w[:, k:k + 1]
    o_ref[...] = h_ref[...] + mod_ref[0][5:6, :] * y


def combine(dest_flat, yb, topw, h, mod_tiles):
    t = h.shape[0]
    tok = pl.BlockSpec((TM, D), lambda i: (i, 0))
    return pl.pallas_call(
        _combine_kernel, out_shape=_sds(h.shape), grid=(t // TM,),
        in_specs=[pl.BlockSpec((TM * TOP_K,), lambda i: (i,), memory_space=pltpu.SMEM),
                  pl.BlockSpec(memory_space=pl.ANY),
                  pl.BlockSpec((TM, LANES), lambda i: (i, 0)), tok,
                  pl.BlockSpec((1, 6, D), lambda i: (i, 0, 0))],
        out_specs=tok,
        scratch_shapes=[pltpu.VMEM((TOP_K, TM, D), f32), pltpu.SemaphoreType.DMA(())],
        compiler_params=_cparams(("arbitrary",)), name="moe_combine")(dest_flat, yb, topw, h, mod_tiles)


def moe_layer(h, mod_tiles, p):
    xn, topi, topw, rank, counts = router(h, mod_tiles, p)
    cnt = counts[0, :N_EXPERTS]
    padded = (cnt + MOE_BM - 1) // MOE_BM * MOE_BM
    pend = jnp.cumsum(padded)
    pstart = jnp.zeros((1, LANES), i32).at[0, :N_EXPERTS].set(pend - padded)
    blk_row0 = jnp.arange(MOE_NBLK, dtype=i32) * MOE_BM
    block_e = jnp.minimum(jnp.sum((pend[None, :] <= blk_row0[:, None]).astype(i32), axis=1), N_EXPERTS - 1)
    nused = (pend[-1:] // MOE_BM).astype(i32)
    blk = jnp.arange(MOE_NBLK, dtype=i32)
    last_of_expert = jnp.any((padded > 0)[None, :] & (blk[:, None] == (pend // MOE_BM - 1)[None, :]), axis=1)
    zflag = (last_of_expert | (blk >= nused[0])).astype(i32)
    dest = dest_rows(topi, rank, pstart)
    dest_flat = dest[:, :TOP_K].reshape(-1)
    xb = dispatch(zflag, dest_flat, xn)
    yb = experts(block_e, nused, xb, p)
    return combine(dest_flat, yb, topw, h, mod_tiles)


def _block_diag(w):
    per = CW // LRU_BW
    w5 = w.reshape(2, D // CW, per, LRU_BW, LRU_BW)
    eye = jnp.eye(per, dtype=w.dtype)
    bd = w5[:, :, :, :, None, :] * eye[None, None, :, None, :, None]
    return bd.reshape(2, D // CW, CW, CW).astype(bf16)


def _layer_params(l, a):
    tile2 = lambda g: jnp.tile(g, 2).reshape(1, LANES)
    whole = ("exp_w_gu", "exp_b_gu", "exp_w_down", "exp_b_down", "w_in", "w_gate", "w_branch", "w_out")
    p = {k: v[l] for k, v in a.items() if k not in whole}
    p["layer"] = l
    for k in ("exp_w_gu", "exp_b_gu", "exp_w_down", "exp_b_down"):
        p[k] = a[k]
    p["w_in_bf"] = cast_bf16(a["w_in"], l)
    p["w_gate_bf"] = cast_bf16(a["w_gate"], l)
    p["w_branch_bf"] = cast_bf16(a["w_branch"].reshape(DEPTH, 4 * D, D), l).reshape(4, D, D)
    p["w_out_bf"] = cast_bf16(a["w_out"], l)
    p["pool_w_bf"] = cast_bf16(a["pool_w"].reshape(DEPTH, D, CW), l).reshape(4, CW, CW)
    p["lru_wr_bd"] = _block_diag(p["lru_w_r"])
    p["lru_wi_bd"] = _block_diag(p["lru_w_i"])
    p["q_norm_g2"] = tile2(p["q_norm_g"])
    p["k_norm_g2"] = tile2(p["k_norm_g"])
    p["router_w_pad"] = jnp.pad(p["router_w"], ((0, 0), (0, LANES - N_EXPERTS)))
    p["router_b_pad"] = jnp.pad(p["router_b"], (0, LANES - N_EXPERTS)).reshape(1, LANES)
    return p


def kernel(x_prompt, x_sample, cache_k, cache_v, state_lru, c, c_ctx, norm1_g, norm2_g, w_mod, b_mod, w_in,
           lru_conv_w, lru_conv_b, lru_w_r, lru_b_r, lru_w_i, lru_b_i, lru_lambda, conv_dw_w, conv_dw_b,
           conv_ln_g, conv_ln_b, q_norm_g, k_norm_g, attn_sink, pool_w, pool_scale, w_branch, w_gate, b_gate,
           w_out, router_w, router_b, exp_w_gu, exp_b_gu, exp_w_down, exp_b_down):
    weights = dict(norm1_g=norm1_g, norm2_g=norm2_g, w_in=w_in, lru_conv_w=lru_conv_w, lru_conv_b=lru_conv_b,
                   lru_w_r=lru_w_r, lru_b_r=lru_b_r, lru_w_i=lru_w_i, lru_b_i=lru_b_i, lru_lambda=lru_lambda,
                   conv_dw_w=conv_dw_w, conv_dw_b=conv_dw_b, conv_ln_g=conv_ln_g, conv_ln_b=conv_ln_b,
                   q_norm_g=q_norm_g, k_norm_g=k_norm_g, attn_sink=attn_sink, pool_w=pool_w, pool_scale=pool_scale,
                   w_branch=w_branch, w_gate=w_gate, b_gate=b_gate, w_out=w_out, router_w=router_w,
                   router_b=router_b, exp_w_gu=exp_w_gu, exp_b_gu=exp_b_gu, exp_w_down=exp_w_down,
                   exp_b_down=exp_b_down)
    kvw = N_KV * HEAD_DIM
    n_cond = 2 * SUBLANES
    cond = jnp.concatenate([c_ctx[None, :], c, jnp.zeros((n_cond - 1 - N_LAT_B, D), f32)], axis=0)
    mod = modulation(cond, w_mod, b_mod)
    tile_start = jnp.arange(T // TM) * TM
    tile_row = jnp.where(tile_start < T_CTX, 0, 1 + (tile_start - T_CTX) // N_LAT)
    h = jnp.concatenate([x_prompt.reshape(T_CTX, D), x_sample.reshape(T_LAT, D)], axis=0)
    ck = cache_k.reshape(N_LAT_B, DEPTH, PAST, kvw)
    cv = cache_v.reshape(N_LAT_B, DEPTH, PAST, kvw)
    rope_cos, rope_sin = rope_tables()
    zero_state = jnp.zeros((N_CTX_B, 2, D), f32)
    new_k, new_v, new_s = [], [], []
    for l in range(DEPTH):
        p = _layer_params(l, weights)
        mod_tiles = mod[l][tile_row].reshape(T // TM, 6, D)
        xn = norm1(h, p["norm1_g"], mod_tiles)
        u = in_proj(xn, p["w_in_bf"])
        lat_blk0 = T_CTX // N_LAT
        lru_c, st_c = lru_branch(u, p, zero_state, N_CTX_B, N_CTX, 0)
        lru_l, _ = lru_branch(u, p, state_lru[:, l], N_LAT_B, N_LAT, lat_blk0)
        conv = (conf_branch(u, p, N_CTX_B, N_CTX, 0), conf_branch(u, p, N_LAT_B, N_LAT, lat_blk0))
        pool = (pool_branch(u, p, N_CTX_B, N_CTX, 0), pool_branch(u, p, N_LAT_B, N_LAT, lat_blk0))
        att_c, k_l, v_l = ctx_attention(u, p)
        att_l = lat_attention(u, p, ck, cv, rope_cos, rope_sin)
        h = merge(h, xn, (lru_c, lru_l), conv, (att_c, att_l), pool, mod_tiles, p)
        h = moe_layer(h, mod_tiles, p)
        new_k.append(k_l.reshape(N_CTX_B, N_CTX, N_KV, HEAD_DIM))
        new_v.append(v_l.reshape(N_CTX_B, N_CTX, N_KV, HEAD_DIM))
        new_s.append(st_c)
    y_prompt = h[:T_CTX].reshape(N_CTX_B, N_CTX, D)
    y_sample = h[T_CTX:].reshape(N_LAT_B, N_LAT, D)
    return (y_prompt, y_sample, jnp.stack(new_k, axis=1), jnp.stack(new_v, axis=1), jnp.stack(new_s, axis=1))
```

```python
import functools

import jax
import jax.numpy as jnp
from jax import lax
from jax.experimental import pallas as pl
from jax.experimental.pallas import tpu as pltpu

f32 = jnp.float32
bf16 = jnp.bfloat16
i32 = jnp.int32

D = 1024
N_CTX_B, N_CTX = 16, 256
N_LAT_B, N_LAT = 8, 1024
T_CTX = N_CTX_B * N_CTX
T_LAT = N_LAT_B * N_LAT
T = T_CTX + T_LAT
DEPTH = 2
PAST = 512
GRID_W = 64
IN_W = 6656
COL_UX, COL_UY, COL_UC, COL_UQ, COL_UK, COL_UV, COL_UP = 0, 1024, 2048, 4096, 5120, 5376, 5632
LRU_BW = 64
LRU_CONV = 4
LRU_C = 8.0
CONV_K = 31
N_HEADS, N_KV, HEAD_DIM = 16, 4, 64
POOL_SIZES = (2, 4, 8, 16)
N_EXPERTS, TOP_K = 32, 4
SWIGLU_LIMIT, SWIGLU_ALPHA = 7.0, 1.702
EPS = 1e-6
NEG_INF = -1e30
ROPE_BASE = 10000.0
LOG2E = 1.4426950408889634

LANES = 128
SUBLANES = 8
VMEM_LIMIT = 56 * 1024 * 1024

CW = 256
RC = 128
TQ = 128
TM = 256
MOE_BM = 256
N_ASSIGN = T * TOP_K
MOE_NBLK = N_ASSIGN // MOE_BM + N_EXPERTS
MOE_CAP = MOE_NBLK * MOE_BM


def _sds(shape, dt=f32):
    return jax.ShapeDtypeStruct(shape, dt)


def _cparams(sem, vmem=VMEM_LIMIT):
    return pltpu.CompilerParams(dimension_semantics=sem, vmem_limit_bytes=vmem)


def _sigmoid(x):
    return 0.5 * jnp.tanh(0.5 * x) + 0.5


def _log1p(z):
    u = 1.0 + z
    d = u - 1.0
    return jnp.where(d == 0.0, z, jnp.log(u) * (z / jnp.where(d == 0.0, 1.0, d)))


def _cast_kernel(x_ref, o_ref):
    o_ref[...] = x_ref[...].astype(o_ref.dtype)


def cast_bf16(w, l):
    _, r, c = w.shape
    tr = 256
    return pl.pallas_call(
        _cast_kernel, out_shape=_sds((r, c), bf16), grid=(r // tr,),
        in_specs=[pl.BlockSpec((None, tr, c), lambda i: (l, i, 0))],
        out_specs=pl.BlockSpec((tr, c), lambda i: (i, 0)),
        compiler_params=_cparams(("parallel",)), name="cast_bf16")(w)


def _mod_kernel(c_ref, w_ref, b_ref, o_ref):
    x = c_ref[...]
    s = x * _sigmoid(x)
    o_ref[0] = jnp.dot(s.astype(bf16), w_ref[0].astype(bf16), preferred_element_type=f32) + b_ref[0]


def modulation(cond, w_mod, b_mod):
    r = cond.shape[0]
    tn = 1536
    return pl.pallas_call(
        _mod_kernel, out_shape=_sds((DEPTH, r, 6 * D)), grid=(DEPTH, 6 * D // tn),
        in_specs=[pl.BlockSpec((r, D), lambda l, j: (0, 0)),
                  pl.BlockSpec((1, D, tn), lambda l, j: (l, 0, j)),
                  pl.BlockSpec((1, 1, tn), lambda l, j: (l, 0, j))],
        out_specs=pl.BlockSpec((1, r, tn), lambda l, j: (l, 0, j)),
        compiler_params=_cparams(("parallel", "parallel")), name="modulation")(
            cond, w_mod, b_mod.reshape(DEPTH, 1, 6 * D))


def _rms_mod(x, g, shift, scale):
    ms = jnp.mean(x * x, axis=-1, keepdims=True)
    return (x * lax.rsqrt(ms + EPS) * g) * (1.0 + scale) + shift


def _norm1_kernel(h_ref, g_ref, mod_ref, o_ref):
    m = mod_ref[0]
    o_ref[...] = _rms_mod(h_ref[...], g_ref[...], m[0:1, :], m[1:2, :]).astype(bf16)


def norm1(h, g, mod_tiles):
    nt = h.shape[0] // TM
    return pl.pallas_call(
        _norm1_kernel, out_shape=_sds(h.shape, bf16), grid=(nt,),
        in_specs=[pl.BlockSpec((TM, D), lambda i: (i, 0)),
                  pl.BlockSpec((1, D), lambda i: (0, 0)),
                  pl.BlockSpec((1, 6, D), lambda i: (i, 0, 0))],
        out_specs=pl.BlockSpec((TM, D), lambda i: (i, 0)),
        compiler_params=_cparams(("parallel",)), name="norm1")(h, g.reshape(1, D), mod_tiles)


def _matmul_kernel(x_ref, w_ref, o_ref):
    o_ref[...] = jnp.dot(x_ref[...], w_ref[...], preferred_element_type=f32)


def in_proj(xn, w):
    t = xn.shape[0]
    tm, tn = 1024, 1664
    return pl.pallas_call(
        _matmul_kernel, out_shape=_sds((t, IN_W)), grid=(t // tm, IN_W // tn),
        in_specs=[pl.BlockSpec((tm, D), lambda i, j: (i, 0)),
                  pl.BlockSpec((D, tn), lambda i, j: (0, j))],
        out_specs=pl.BlockSpec((tm, tn), lambda i, j: (i, j)),
        compiler_params=_cparams(("parallel", "parallel")), name="in_proj")(xn, w)


def _gelu_tanh(x):
    return 0.5 * x * (1.0 + jnp.tanh(0.7978845608028654 * (x + 0.044715 * (x * x * x))))


def _lru_kernel(ux_ref, uy_ref, cw_ref, cb_ref, wr_ref, wi_ref, br_ref, bi_ref, lam_ref, h0_ref,
                out_ref, st_ref, xp_ref, a_ref, b_ref, hs_ref, *, n):
    pad = SUBLANES
    xp_ref[pl.ds(0, pad), :] = jnp.zeros((pad, CW), f32)
    xp_ref[pl.ds(n + pad, pad), :] = jnp.zeros((pad, CW), f32)
    xp_ref[pl.ds(pad, n), :] = ux_ref[...]
    left = LRU_CONV // 2
    lam = lam_ref[...]
    neg_c_sp = -LRU_C * (jnp.maximum(-lam, 0.0) + _log1p(jnp.exp(-jnp.abs(lam))))

    def coef_body(i, c):
        r0 = pl.multiple_of(i * RC, RC)
        w = xp_ref[pl.ds(r0, RC + 2 * pad), :]
        xc = jnp.zeros((RC, CW), f32) + cb_ref[...]
        for k in range(LRU_CONV):
            off = pad - left + k
            xc = xc + cw_ref[k:k + 1, :] * w[off:off + RC, :]
        xcb = xc.astype(bf16)
        for d in range(2):
            r = _sigmoid(jnp.dot(xcb, wr_ref[d, 0], preferred_element_type=f32) + br_ref[d:d + 1, :])
            g = _sigmoid(jnp.dot(xcb, wi_ref[d, 0], preferred_element_type=f32) + bi_ref[d:d + 1, :])
            log_a = r * neg_c_sp[d:d + 1, :]
            a = jnp.exp(log_a)
            th = jnp.tanh(log_a)
            one_m_a2 = (-2.0 * th) / (1.0 - th)
            a_ref[d, pl.ds(r0, RC), :] = a
            b_ref[d, pl.ds(r0, RC), :] = jnp.sqrt(one_m_a2) * (g * xc)
        return c

    lax.fori_loop(0, n // RC, coef_body, 0)

    row = lax.broadcasted_iota(i32, (SUBLANES, CW), 0)
    nchunk = n // SUBLANES

    def scan_body(j, carry):
        hf, hb = carry
        rf = pl.multiple_of(j * SUBLANES, SUBLANES)
        a = a_ref[0, pl.ds(rf, SUBLANES), :]
        b = b_ref[0, pl.ds(rf, SUBLANES), :]
        for sh in (1, 2, 4):
            a_s = jnp.where(row >= sh, pltpu.roll(a, sh, axis=0), 1.0)
            b_s = jnp.where(row >= sh, pltpu.roll(b, sh, axis=0), 0.0)
            b = a * b_s + b
            a = a * a_s
        h = a * hf + b
        hs_ref[0, pl.ds(rf, SUBLANES), :] = h
        hf = h[SUBLANES - 1:SUBLANES, :]
        rb = pl.multiple_of((nchunk - 1 - j) * SUBLANES, SUBLANES)
        a = a_ref[1, pl.ds(rb, SUBLANES), :]
        b = b_ref[1, pl.ds(rb, SUBLANES), :]
        for sh in (1, 2, 4):
            keep = row < SUBLANES - sh
            a_s = jnp.where(keep, pltpu.roll(a, SUBLANES - sh, axis=0), 1.0)
            b_s = jnp.where(keep, pltpu.roll(b, SUBLANES - sh, axis=0), 0.0)
            b = a * b_s + b
            a = a * a_s
        h = a * hb + b
        hs_ref[1, pl.ds(rb, SUBLANES), :] = h
        hb = h[0:1, :]
        return hf, hb

    h0 = h0_ref[0]
    hf, hb = lax.fori_loop(0, nchunk, scan_body, (h0[0:1, :], h0[1:2, :]), unroll=4)
    st_ref[0, 0:1, :] = hf
    st_ref[0, 1:2, :] = hb

    def out_body(i, c):
        r0 = pl.multiple_of(i * RC, RC)
        hsum = hs_ref[0, pl.ds(r0, RC), :] + hs_ref[1, pl.ds(r0, RC), :]
        out_ref[pl.ds(r0, RC), :] = hsum * _gelu_tanh(uy_ref[pl.ds(r0, RC), :])
        return c

    lax.fori_loop(0, n // RC, out_body, 0)


def lru_branch(u, p, h0, nb, n, row_blk0):
    nct = D // CW
    cu = COL_UY // CW
    kern = functools.partial(_lru_kernel, n=n)
    vec = lambda a: pl.BlockSpec((a, CW), lambda b, c: (0, c))
    call = pl.pallas_call(
        kern, out_shape=(_sds((nb * n, D)), _sds((nb, 2, D))), grid=(nb, nct),
        in_specs=[pl.BlockSpec((n, CW), lambda b, c: (row_blk0 + b, c)),
                  pl.BlockSpec((n, CW), lambda b, c: (row_blk0 + b, cu + c)),
                  vec(LRU_CONV), vec(1),
                  pl.BlockSpec((2, 1, CW, CW), lambda b, c: (0, c, 0, 0)),
                  pl.BlockSpec((2, 1, CW, CW), lambda b, c: (0, c, 0, 0)),
                  vec(2), vec(2), vec(2),
                  pl.BlockSpec((1, 2, CW), lambda b, c: (b, 0, c))],
        out_specs=(pl.BlockSpec((n, CW), lambda b, c: (b, c)),
                   pl.BlockSpec((1, 2, CW), lambda b, c: (b, 0, c))),
        scratch_shapes=[pltpu.VMEM((n + 2 * SUBLANES, CW), f32),
                        pltpu.VMEM((2, n, CW), f32), pltpu.VMEM((2, n, CW), f32), pltpu.VMEM((2, n, CW), f32)],
        compiler_params=_cparams(("parallel", "parallel")), name="lru_branch")
    return call(u, u, p["lru_conv_w"], p["lru_conv_b"].reshape(1, D), p["lru_wr_bd"], p["lru_wi_bd"],
                p["lru_b_r"], p["lru_b_i"], p["lru_lambda"], h0)


def _conf_kernel(ua_ref, ug_ref, w_ref, b_ref, o_ref, gp_ref, *, n):
    pad = 2 * SUBLANES
    left = CONV_K // 2
    gp_ref[pl.ds(0, pad), :] = jnp.zeros((pad, CW), f32)
    gp_ref[pl.ds(n + pad, pad), :] = jnp.zeros((pad, CW), f32)
    gp_ref[pl.ds(pad, n), :] = ua_ref[...] * _sigmoid(ug_ref[...])

    wn = RC + 2 * pad

    def body(i, c):
        r0 = pl.multiple_of(i * RC, RC)
        w = gp_ref[pl.ds(r0, wn), :]
        rolled = [w] + [pltpu.roll(w, wn - m, axis=0) for m in range(1, SUBLANES)]
        acc = jnp.zeros((RC, CW), f32) + b_ref[...]
        for k in range(CONV_K):
            off = pad - left + k
            q, m = off // SUBLANES, off % SUBLANES
            acc = acc + w_ref[k:k + 1, :] * rolled[m][q * SUBLANES:q * SUBLANES + RC, :]
        o_ref[pl.ds(r0, RC), :] = acc
        return c

    lax.fori_loop(0, n // RC, body, 0)


def conf_branch(u, p, nb, n, row_blk0):
    nct = D // CW
    ca, cg = COL_UC // CW, (COL_UC + D) // CW
    kern = functools.partial(_conf_kernel, n=n)
    call = pl.pallas_call(
        kern, out_shape=_sds((nb * n, D)), grid=(nb, nct),
        in_specs=[pl.BlockSpec((n, CW), lambda b, c: (row_blk0 + b, ca + c)),
                  pl.BlockSpec((n, CW), lambda b, c: (row_blk0 + b, cg + c)),
                  pl.BlockSpec((CONV_K, CW), lambda b, c: (0, c)),
                  pl.BlockSpec((1, CW), lambda b, c: (0, c))],
        out_specs=pl.BlockSpec((n, CW), lambda b, c: (b, c)),
        scratch_shapes=[pltpu.VMEM((n + 4 * SUBLANES, CW), f32)],
        compiler_params=_cparams(("parallel", "parallel")), name="conf_branch")
    return call(u, u, p["conv_dw_w"], p["conv_dw_b"].reshape(1, D))


def _pool_kernel(up_ref, w_ref, s_ref, o_ref, xp_ref, *, n):
    pad = SUBLANES
    gi = pl.program_id(1)
    half = jnp.left_shift(1, gi)
    xp_ref[pl.ds(0, pad), :] = jnp.zeros((pad, CW), f32)
    xp_ref[pl.ds(n + pad, pad), :] = jnp.zeros((pad, CW), f32)
    xp_ref[pl.ds(pad, n), :] = up_ref[...]
    wn = RC + 2 * pad

    def body(i, c):
        r0 = pl.multiple_of(i * RC, RC)
        w = xp_ref[pl.ds(r0, wn), :]
        s2 = w + pltpu.roll(w, 1, axis=0)
        s4 = pltpu.roll(s2, 1, axis=0) + pltpu.roll(s2, wn - 1, axis=0)
        s8 = pltpu.roll(s4, 2, axis=0) + pltpu.roll(s4, wn - 2, axis=0)
        s16 = pltpu.roll(s8, 4, axis=0) + pltpu.roll(s8, wn - 4, axis=0)
        s = jnp.where(gi == 0, s2, jnp.where(gi == 1, s4, jnp.where(gi == 2, s8, s16)))[pad:pad + RC, :]
        t = r0 + lax.broadcasted_iota(i32, (RC, CW), 0)
        cnt = (jnp.minimum(t + half, n) - jnp.maximum(t - half, 0)).astype(f32)
        pooled = s / cnt - w[pad:pad + RC, :]
        o_ref[pl.ds(r0, RC), :] = jnp.dot(pooled.astype(bf16), w_ref[0], preferred_element_type=f32) * s_ref[...]
        return c

    lax.fori_loop(0, n // RC, body, 0)


def pool_branch(u, p, nb, n, row_blk0):
    assert POOL_SIZES == (2, 4, 8, 16) and D // len(POOL_SIZES) == CW
    cp = COL_UP // CW
    kern = functools.partial(_pool_kernel, n=n)
    call = pl.pallas_call(
        kern, out_shape=_sds((nb * n, D)), grid=(nb, len(POOL_SIZES)),
        in_specs=[pl.BlockSpec((n, CW), lambda b, c: (row_blk0 + b, cp + c)),
                  pl.BlockSpec((1, CW, CW), lambda b, c: (c, 0, 0)),
                  pl.BlockSpec((1, CW), lambda b, c: (0, c))],
        out_specs=pl.BlockSpec((n, CW), lambda b, c: (b, c)),
        scratch_shapes=[pltpu.VMEM((n + 2 * SUBLANES, CW), f32)],
        compiler_params=_cparams(("parallel", "parallel")), name="pool_branch")
    return call(u, p["pool_w_bf"], p["pool_scale"].reshape(1, D))


def _head_norm(blk, g128, lane_lo):
    sq = blk * blk
    s_lo = jnp.sum(jnp.where(lane_lo, sq, 0.0), axis=-1, keepdims=True)
    s_hi = jnp.sum(jnp.where(lane_lo, 0.0, sq), axis=-1, keepdims=True)
    r = jnp.where(lane_lo, lax.rsqrt(s_lo * (1.0 / HEAD_DIM) + EPS), lax.rsqrt(s_hi * (1.0 / HEAD_DIM) + EPS))
    return blk * r * g128


def _rope(blk, cos, sin_signed, lane):
    partner = jnp.where((lane % 32) < 16, pltpu.roll(blk, LANES - 16, axis=1), pltpu.roll(blk, 16, axis=1))
    return blk * cos + partner * sin_signed


def _prep_kv(k_ref, v_ref, ks_ref, vs_ref, kg, lane_lo, rope=None, newk_ref=None, newv_ref=None):
    nk = k_ref.shape[0]
    lane = lax.broadcasted_iota(i32, (nk, LANES), 1)
    for tj in range(N_KV * HEAD_DIM // LANES):
        kt = _head_norm(k_ref[:, tj * LANES:(tj + 1) * LANES], kg, lane_lo(nk))
        if rope is not None:
            kt = _rope(kt, rope[0][...], rope[1][...], lane)
        vt = v_ref[:, tj * LANES:(tj + 1) * LANES]
        if newk_ref is not None:
            newk_ref[0, :, tj * LANES:(tj + 1) * LANES] = kt
            newv_ref[0, :, tj * LANES:(tj + 1) * LANES] = vt
        _store_low_half(kt, vt, ks_ref, vs_ref, tj, lane_lo(nk))


def _store_low_half(kt, vt, ks_ref, vs_ref, tj, lane_lo):
    ks_ref[2 * tj] = jnp.where(lane_lo, kt, 0.0).astype(bf16)
    vs_ref[2 * tj] = jnp.where(lane_lo, vt, 0.0).astype(bf16)
    ks_ref[2 * tj + 1] = jnp.where(lane_lo, pltpu.roll(kt, HEAD_DIM, axis=1), 0.0).astype(bf16)
    vs_ref[2 * tj + 1] = jnp.where(lane_lo, pltpu.roll(vt, HEAD_DIM, axis=1), 0.0).astype(bf16)


def _prep_cache(ck_ref, cv_ref, kc_ref, vc_ref, lane_lo):
    nk = ck_ref.shape[2]
    for tj in range(N_KV * HEAD_DIM // LANES):
        kt = ck_ref[0, 0, :, tj * LANES:(tj + 1) * LANES]
        vt = cv_ref[0, 0, :, tj * LANES:(tj + 1) * LANES]
        _store_low_half(kt, vt, kc_ref, vc_ref, tj, lane_lo(nk))


def _qk(qm, k):
    return lax.dot_general(qm, k, (((1,), (1,)), ((), ())), preferred_element_type=f32)


def _attend(sink_ref, q_tiles, segments, att_ref):
    per_kv = N_HEADS // N_KV
    assert per_kv == 4
    rows = per_kv * TQ
    row = lax.broadcasted_iota(i32, (rows, 1), 0)
    masks = [None if mk is None else jnp.concatenate([mk] * per_kv, axis=0) for _, _, mk in segments]
    for g in range(N_KV):
        t0, t1 = q_tiles[2 * g], q_tiles[2 * g + 1]
        qm = jnp.concatenate([t0, t1, pltpu.roll(t0, HEAD_DIM, axis=1), pltpu.roll(t1, HEAD_DIM, axis=1)],
                             axis=0).astype(bf16)
        hd = [per_kv * g, per_kv * g + 2, per_kv * g + 1, per_kv * g + 3]
        sink = jnp.where(row < TQ, sink_ref[hd[0]],
                         jnp.where(row < 2 * TQ, sink_ref[hd[1]],
                                   jnp.where(row < 3 * TQ, sink_ref[hd[2]], sink_ref[hd[3]]))) * LOG2E
        scores = []
        m_el = None
        for (kget, _, _), mask in zip(segments, masks):
            s = _qk(qm, kget(g))
            if mask is not None:
                s = jnp.where(mask, s, NEG_INF)
            for c in range(s.shape[1] // LANES):
                t = s[:, c * LANES:(c + 1) * LANES]
                m_el = t if m_el is None else jnp.maximum(m_el, t)
            scores.append(s)
        m = jnp.maximum(jnp.max(m_el, axis=-1, keepdims=True), sink)
        d_el = jnp.zeros((rows, LANES), f32)
        o = jnp.zeros((rows, LANES), f32)
        for s, (_, vget, _) in zip(scores, segments):
            pr = jnp.exp2(s - m)
            for c in range(s.shape[1] // LANES):
                d_el = d_el + pr[:, c * LANES:(c + 1) * LANES]
            o = o + jnp.dot(pr.astype(bf16), vget(g), preferred_element_type=f32)
        den = jnp.exp2(sink - m) + jnp.sum(d_el, axis=-1, keepdims=True)
        o = o / den
        att_ref[:, 2 * g * LANES:(2 * g + 1) * LANES] = o[:TQ] + pltpu.roll(o[2 * TQ:3 * TQ], HEAD_DIM, axis=1)
        att_ref[:, (2 * g + 1) * LANES:(2 * g + 2) * LANES] = o[TQ:2 * TQ] + pltpu.roll(o[3 * TQ:], HEAD_DIM, axis=1)


def _q_tiles(q_ref, qg, rope=None):
    lane = lax.broadcasted_iota(i32, (TQ, LANES), 1)
    lane_lo = lane < HEAD_DIM
    tiles = []
    for j in range(N_HEADS // 2):
        qt = _head_norm(q_ref[:, j * LANES:(j + 1) * LANES], qg, lane_lo)
        if rope is not None:
            qt = _rope(qt, rope[0], rope[1], lane)
        tiles.append(qt * (HEAD_DIM ** -0.5 * LOG2E))
    return tiles


def _lane_lo_fn(nrows):
    return lax.broadcasted_iota(i32, (nrows, LANES), 1) < HEAD_DIM


def _ctx_attn_kernel(sink_ref, q_ref, k_ref, v_ref, qg_ref, kg_ref, att_ref, newk_ref, newv_ref, ks_ref, vs_ref):
    @pl.when(pl.program_id(1) == 0)
    def _():
        _prep_kv(k_ref, v_ref, ks_ref, vs_ref, kg_ref[...], _lane_lo_fn, None, newk_ref, newv_ref)

    tiles = _q_tiles(q_ref, qg_ref[...])
    seg = [(lambda g: ks_ref[g], lambda g: vs_ref[g], None)]
    _attend(sink_ref, tiles, seg, att_ref)


def ctx_attention(u, p):
    nq = N_CTX // TQ
    kvw = N_KV * HEAD_DIM
    return pl.pallas_call(
        _ctx_attn_kernel,
        out_shape=(_sds((T_CTX, D)), _sds((N_CTX_B, N_CTX, kvw)), _sds((N_CTX_B, N_CTX, kvw))),
        grid=(N_CTX_B, nq),
        in_specs=[pl.BlockSpec(memory_space=pltpu.SMEM),
                  pl.BlockSpec((TQ, D), lambda b, i: (b * nq + i, COL_UQ // D)),
                  pl.BlockSpec((N_CTX, kvw), lambda b, i: (b, COL_UK // kvw)),
                  pl.BlockSpec((N_CTX, kvw), lambda b, i: (b, COL_UV // kvw)),
                  pl.BlockSpec((1, LANES), lambda b, i: (0, 0)),
                  pl.BlockSpec((1, LANES), lambda b, i: (0, 0))],
        out_specs=(pl.BlockSpec((TQ, D), lambda b, i: (b * nq + i, 0)),
                   pl.BlockSpec((1, N_CTX, kvw), lambda b, i: (b, 0, 0)),
                   pl.BlockSpec((1, N_CTX, kvw), lambda b, i: (b, 0, 0))),
        scratch_shapes=[pltpu.VMEM((N_KV, N_CTX, LANES), bf16), pltpu.VMEM((N_KV, N_CTX, LANES), bf16)],
        compiler_params=_cparams(("parallel", "arbitrary")), name="ctx_attention")(
            p["attn_sink"], u, u, u, p["q_norm_g2"], p["k_norm_g2"])


def _lat_attn_kernel(sink_ref, q_ref, k_ref, v_ref, ck_ref, cv_ref, cosq_ref, sinq_ref, cosk_ref, sink_tab_ref,
                     qg_ref, kg_ref, att_ref, ks_ref, vs_ref, kc_ref, vc_ref):
    i = pl.program_id(1)
    nq = N_LAT // TQ

    @pl.when(i == 0)
    def _():
        _prep_kv(k_ref, v_ref, ks_ref, vs_ref, kg_ref[...], _lane_lo_fn, (cosk_ref, sink_tab_ref))
        _prep_cache(ck_ref, cv_ref, kc_ref, vc_ref, _lane_lo_fn)

    tiles = _q_tiles(q_ref, qg_ref[...], (cosq_ref[...], sinq_ref[...]))
    r = lax.broadcasted_iota(i32, (TQ, TQ), 0)
    c = lax.broadcasted_iota(i32, (TQ, TQ), 1)
    prev0 = pl.multiple_of(jnp.maximum(i - 1, 0) * TQ, TQ)
    cur0 = pl.multiple_of(i * TQ, TQ)
    next0 = pl.multiple_of(jnp.minimum(i + 1, nq - 1) * TQ, TQ)
    mask_prev = jnp.logical_and(c >= r, i > 0)
    mask_next = jnp.logical_and(c <= r, i < nq - 1)

    def seg(r0, mask):
        return (lambda g: ks_ref[g, pl.ds(r0, TQ), :], lambda g: vs_ref[g, pl.ds(r0, TQ), :], mask)

    segments = [seg(prev0, mask_prev), seg(cur0, None), seg(next0, mask_next),
                (lambda g: kc_ref[g], lambda g: vc_ref[g], None)]
    _attend(sink_ref, tiles, segments, att_ref)


def lat_attention(u, p, cache_k, cache_v, rope_cos, rope_sin):
    nq = N_LAT // TQ
    kvw = N_KV * HEAD_DIM
    l = p["layer"]
    qblk0 = T_CTX // TQ
    kblk0 = T_CTX // N_LAT
    call = pl.pallas_call(
        _lat_attn_kernel, out_shape=_sds((T_LAT, D)), grid=(N_LAT_B, nq),
        in_specs=[pl.BlockSpec(memory_space=pltpu.SMEM),
                  pl.BlockSpec((TQ, D), lambda b, i: (qblk0 + b * nq + i, COL_UQ // D)),
                  pl.BlockSpec((N_LAT, kvw), lambda b, i: (kblk0 + b, COL_UK // kvw)),
                  pl.BlockSpec((N_LAT, kvw), lambda b, i: (kblk0 + b, COL_UV // kvw)),
                  pl.BlockSpec((1, 1, PAST, kvw), lambda b, i: (b, l, 0, 0)),
                  pl.BlockSpec((1, 1, PAST, kvw), lambda b, i: (b, l, 0, 0)),
                  pl.BlockSpec((TQ, LANES), lambda b, i: (i, 0)),
                  pl.BlockSpec((TQ, LANES), lambda b, i: (i, 0)),
                  pl.BlockSpec((N_LAT, LANES), lambda b, i: (0, 0)),
                  pl.BlockSpec((N_LAT, LANES), lambda b, i: (0, 0)),
                  pl.BlockSpec((1, LANES), lambda b, i: (0, 0)),
                  pl.BlockSpec((1, LANES), lambda b, i: (0, 0))],
        out_specs=pl.BlockSpec((TQ, D), lambda b, i: (b * nq + i, 0)),
        scratch_shapes=[pltpu.VMEM((N_KV, N_LAT, LANES), bf16), pltpu.VMEM((N_KV, N_LAT, LANES), bf16),
                        pltpu.VMEM((N_KV, PAST, LANES), bf16), pltpu.VMEM((N_KV, PAST, LANES), bf16)],
        compiler_params=_cparams(("parallel", "arbitrary")), name="lat_attention")
    return call(p["attn_sink"], u, u, u, cache_k, cache_v, rope_cos, rope_sin, rope_cos, rope_sin,
                p["q_norm_g2"], p["k_norm_g2"])


def rope_tables():
    pos = jnp.arange(N_LAT)
    rows = (pos // GRID_W).astype(f32)
    cols = (pos % GRID_W).astype(f32)
    half = HEAD_DIM // 2
    freqs = ROPE_BASE ** (-jnp.arange(0, half, 2, dtype=f32) / half)
    lane = jnp.arange(LANES)
    within = lane % half
    fidx = within % (half // 2)
    use_cols = (lane % HEAD_DIM) >= half
    ang = jnp.where(use_cols[None, :], cols[:, None], rows[:, None]) * freqs[fidx][None, :]
    sign = jnp.where(within < half // 2, -1.0, 1.0)
    return jnp.cos(ang), jnp.sin(ang) * sign[None, :]


def _merge_kernel(h_ref, xn_ref, lru_c, lru_l, conv_c, conv_l, att_c, att_l, pool_c, pool_l, mod_ref, lng_ref, lnb_ref,
                  wg_ref, bg_ref, wb_ref, wo_ref, o_ref):
    is_ctx = pl.program_id(0) < T_CTX // TM
    pick = lambda c_ref, l_ref: jnp.where(is_ctx, c_ref[...], l_ref[...])
    xn = xn_ref[...]
    hc = pick(conv_c, conv_l)
    mu = jnp.mean(hc, axis=-1, keepdims=True)
    xc = hc - mu
    var = jnp.mean(xc * xc, axis=-1, keepdims=True)
    y = xc * lax.rsqrt(var + EPS) * lng_ref[...] + lnb_ref[...]
    conv = y * _sigmoid(y)
    merged = jnp.zeros((TM, D), f32)
    for j, br in enumerate((pick(lru_c, lru_l), conv, pick(att_c, att_l), pick(pool_c, pool_l))):
        gate = _sigmoid(jnp.dot(xn, wg_ref[:, j * D:(j + 1) * D], preferred_element_type=f32)
                        + bg_ref[:, j * D:(j + 1) * D])
        proj = jnp.dot(br.astype(bf16), wb_ref[j], preferred_element_type=f32)
        merged = merged + gate * proj
    out = jnp.dot(merged.astype(bf16), wo_ref[...], preferred_element_type=f32)
    o_ref[...] = h_ref[...] + mod_ref[0][2:3, :] * out


def merge(h, xn, lru, conv, att, pool, mod_tiles, p):
    nt = h.shape[0] // TM
    n_ctx_tiles = T_CTX // TM
    tok = pl.BlockSpec((TM, D), lambda i: (i, 0))
    tok_c = pl.BlockSpec((TM, D), lambda i: (jnp.minimum(i, n_ctx_tiles - 1), 0))
    tok_l = pl.BlockSpec((TM, D), lambda i: (jnp.maximum(i - n_ctx_tiles, 0), 0))
    const = lambda shape: pl.BlockSpec(shape, lambda i: (0,) * len(shape), pipeline_mode=pl.Buffered(1))
    return pl.pallas_call(
        _merge_kernel, out_shape=_sds(h.shape), grid=(nt,),
        in_specs=[tok, tok, tok_c, tok_l, tok_c, tok_l, tok_c, tok_l, tok_c, tok_l,
                  pl.BlockSpec((1, 6, D), lambda i: (i, 0, 0)),
                  const((1, D)), const((1, D)),
                  const((D, 4 * D)), const((1, 4 * D)), const((4, D, D)), const((D, D))],
        out_specs=tok,
        compiler_params=_cparams(("parallel",)), name="merge")(
            h, xn, *lru, *conv, *att, *pool, mod_tiles, p["conv_ln_g"].reshape(1, D), p["conv_ln_b"].reshape(1, D),
            p["w_gate_bf"], p["b_gate"].reshape(1, 4 * D), p["w_branch_bf"], p["w_out_bf"])


def _router_kernel(h_ref, g_ref, mod_ref, rw_ref, rb_ref, xn_ref, topi_ref, topw_ref, rank_ref, cnt_ref, carry_ref):
    step = pl.program_id(0)

    @pl.when(step == 0)
    def _():
        carry_ref[...] = jnp.zeros_like(carry_ref)

    m = mod_ref[0]
    xn = _rms_mod(h_ref[...], g_ref[...], m[3:4, :], m[4:5, :])
    xn_ref[...] = xn
    logits = jnp.dot(xn, rw_ref[...], preferred_element_type=f32, precision=lax.Precision.HIGHEST) + rb_ref[...]
    lane = lax.broadcasted_iota(i32, (TM, LANES), 1).astype(f32)
    lg = jnp.where(lane < N_EXPERTS, logits, NEG_INF)
    vals, idxs = [], []
    for _ in range(TOP_K):
        mx = jnp.max(lg, axis=-1, keepdims=True)
        idx = jnp.min(jnp.where(lg == mx, lane, float(LANES)), axis=-1, keepdims=True)
        vals.append(mx)
        idxs.append(idx)
        lg = jnp.where(lane == idx, -3e38, lg)
    exps = [jnp.exp(v - vals[0]) for v in vals]
    den = exps[0] + exps[1] + exps[2] + exps[3]
    cnt = jnp.zeros((TM, LANES), f32)
    for idx in idxs:
        cnt = cnt + jnp.where(lane == idx, 1.0, 0.0)
    rr = lax.broadcasted_iota(i32, (TM, TM), 0)
    cc = lax.broadcasted_iota(i32, (TM, TM), 1)
    tri = jnp.where(rr > cc, 1.0, 0.0).astype(bf16)
    before = jnp.dot(tri, cnt.astype(bf16), preferred_element_type=f32) + carry_ref[0:1, :]
    topi = jnp.zeros((TM, LANES), f32)
    topw = jnp.zeros((TM, LANES), f32)
    rank = jnp.zeros((TM, LANES), f32)
    for k in range(TOP_K):
        rk = jnp.sum(jnp.where(lane == idxs[k], before, 0.0), axis=-1, keepdims=True)
        topi = jnp.where(lane == k, idxs[k], topi)
        topw = jnp.where(lane == k, exps[k] / den, topw)
        rank = jnp.where(lane == k, rk, rank)
    topi_ref[...] = topi.astype(i32)
    topw_ref[...] = topw
    rank_ref[...] = rank.astype(i32)
    total = carry_ref[0:1, :] + jnp.sum(cnt, axis=0, keepdims=True)
    carry_ref[...] = jnp.broadcast_to(total, carry_ref.shape)
    cnt_ref[...] = jnp.broadcast_to(total, cnt_ref.shape).astype(i32)


def router(h, mod_tiles, p):
    nt = h.shape[0] // TM
    tok = lambda w, dt=f32: pl.BlockSpec((TM, w), lambda i: (i, 0))
    t = h.shape[0]
    return pl.pallas_call(
        _router_kernel,
        out_shape=(_sds((t, D)), _sds((t, LANES), i32), _sds((t, LANES)), _sds((t, LANES), i32),
                   _sds((SUBLANES, LANES), i32)),
        grid=(nt,),
        in_specs=[tok(D), pl.BlockSpec((1, D), lambda i: (0, 0)), pl.BlockSpec((1, 6, D), lambda i: (i, 0, 0)),
                  pl.BlockSpec((D, LANES), lambda i: (0, 0)), pl.BlockSpec((1, LANES), lambda i: (0, 0))],
        out_specs=(tok(D), tok(LANES), tok(LANES), tok(LANES), pl.BlockSpec((SUBLANES, LANES), lambda i: (0, 0))),
        scratch_shapes=[pltpu.VMEM((SUBLANES, LANES), f32)],
        compiler_params=_cparams(("arbitrary",)), name="router")(
            h, p["norm2_g"].reshape(1, D), mod_tiles, p["router_w_pad"], p["router_b_pad"])


def _dest_kernel(topi_ref, rank_ref, pstart_ref, o_ref):
    lane = lax.broadcasted_iota(i32, (TM, LANES), 1).astype(f32)
    topi = topi_ref[...].astype(f32)
    pstart = pstart_ref[...].astype(f32)
    dest = rank_ref[...].astype(f32)
    for k in range(TOP_K):
        e = jnp.sum(jnp.where(lane == k, topi, 0.0), axis=-1, keepdims=True)
        ps = jnp.sum(jnp.where(lane == e, pstart, 0.0), axis=-1, keepdims=True)
        dest = dest + jnp.where(lane == k, ps, 0.0)
    o_ref[...] = dest.astype(i32)


def dest_rows(topi, rank, pstart):
    t = topi.shape[0]
    tok = pl.BlockSpec((TM, LANES), lambda i: (i, 0))
    return pl.pallas_call(
        _dest_kernel, out_shape=_sds((t, LANES), i32), grid=(t // TM,),
        in_specs=[tok, tok, pl.BlockSpec((1, LANES), lambda i: (0, 0))], out_specs=tok,
        compiler_params=_cparams(("parallel",)), name="dest_rows")(topi, rank, pstart)


LOG2_TOP_K = 2
assert 1 << LOG2_TOP_K == TOP_K


def _inv_kernel(dest_ref, inv_ref):
    i = pl.program_id(0)
    n = TM * TOP_K

    @pl.when(i == 0)
    def _():
        def zero(j, c):
            inv_ref[j] = 0
            return c

        lax.fori_loop(0, MOE_CAP, zero, 0, unroll=8)

    def body(j, c):
        inv_ref[dest_ref[j]] = i * n + j
        return c

    lax.fori_loop(0, n, body, 0, unroll=8)


def inverse_map(dest_flat):
    return pl.pallas_call(
        _inv_kernel, out_shape=_sds((MOE_CAP,), i32), grid=(T // TM,),
        in_specs=[pl.BlockSpec((TM * TOP_K,), lambda i: (i,), memory_space=pltpu.SMEM)],
        out_specs=pl.BlockSpec(memory_space=pltpu.SMEM),
        compiler_params=_cparams(("arbitrary",)), name="moe_inverse_map")(dest_flat)


def _expert_kernel(be_ref, nused_ref, nvalid_ref, inv_cur, inv_nxt, x_hbm, wgu_ref, bgu_ref, wdn_ref, bdn_ref,
                   y_hbm, xbuf, obuf, gsem, ssem, wgu_bf, wdn_bf):
    i = pl.program_id(0)
    nused = nused_ref[0]
    slot = lax.rem(i, 2)

    def gather_copy(inv_ref, r, s):
        tok = lax.shift_right_logical(inv_ref[r], LOG2_TOP_K)
        return pltpu.make_async_copy(x_hbm.at[pl.ds(tok, 1), :], xbuf.at[s, pl.ds(r, 1), :], gsem.at[s])

    def scatter_copy(r, s):
        a = inv_cur[r]
        dst = (a & (TOP_K - 1)) * T + lax.shift_right_logical(a, LOG2_TOP_K)
        return pltpu.make_async_copy(obuf.at[s, pl.ds(r, 1), :], y_hbm.at[pl.ds(dst, 1), :], ssem.at[s])

    def start_gather(inv_ref, s, nv):
        lax.fori_loop(0, nv, lambda r, c: (gather_copy(inv_ref, r, s).start(), c)[1], 0)

    def wait_gather(s, nv):
        lax.fori_loop(0, nv, lambda r, c: (gather_copy(inv_cur, 0, s).wait(), c)[1], 0)

    def start_scatter(s, nv):
        lax.fori_loop(0, nv, lambda r, c: (scatter_copy(r, s).start(priority=1), c)[1], 0)

    def wait_scatter(s, nv):
        lax.fori_loop(0, nv, lambda r, c: (scatter_copy(0, s).wait(), c)[1], 0)

    @pl.when(i == 0)
    def _():
        xbuf[...] = jnp.zeros_like(xbuf)
        start_gather(inv_cur, 0, nvalid_ref[0])

    @pl.when(i + 1 < nused)
    def _():
        start_gather(inv_nxt, 1 - slot, nvalid_ref[i + 1])

    e = be_ref[i]
    prev = be_ref[jnp.maximum(i - 1, 0)]

    @pl.when(jnp.logical_or(i == 0, e != prev))
    def _():
        wgu_bf[...] = wgu_ref[0, 0].astype(bf16)
        wdn_bf[...] = wdn_ref[0, 0].astype(bf16)

    @pl.when(i < nused)
    def _():
        @pl.when(i >= 2)
        def _():
            wait_scatter(slot, nvalid_ref[jnp.maximum(i - 2, 0)])

        wait_gather(slot, nvalid_ref[i])
        x = xbuf[slot].astype(bf16)
        h = jnp.dot(x, wgu_bf[...], preferred_element_type=f32) + bgu_ref[0, 0]
        gate = jnp.minimum(h[:, :D], SWIGLU_LIMIT)
        lin = jnp.clip(h[:, D:], -SWIGLU_LIMIT, SWIGLU_LIMIT)
        act = (lin + 1.0) * (gate * _sigmoid(SWIGLU_ALPHA * gate))
        obuf[slot] = jnp.dot(act.astype(bf16), wdn_bf[...], preferred_element_type=f32) + bdn_ref[0, 0]
        start_scatter(slot, nvalid_ref[i])

        @pl.when(i == nused - 1)
        def _():
            wait_scatter(slot, nvalid_ref[i])

            @pl.when(i >= 1)
            def _():
                wait_scatter(1 - slot, nvalid_ref[jnp.maximum(i - 1, 0)])


def experts(block_e, nused, nvalid, inv, xn, p):
    l = p["layer"]
    grid_spec = pltpu.PrefetchScalarGridSpec(
        num_scalar_prefetch=3, grid=(MOE_NBLK,),
        in_specs=[pl.BlockSpec((MOE_BM,), lambda i, be, nu, nv: (i,), memory_space=pltpu.SMEM),
                  pl.BlockSpec((MOE_BM,), lambda i, be, nu, nv: (jnp.minimum(i + 1, MOE_NBLK - 1),),
                               memory_space=pltpu.SMEM),
                  pl.BlockSpec(memory_space=pl.ANY),
                  pl.BlockSpec((1, 1, D, 2 * D), lambda i, be, nu, nv: (l, be[i], 0, 0)),
                  pl.BlockSpec((1, 1, 1, 2 * D), lambda i, be, nu, nv: (l, be[i], 0, 0)),
                  pl.BlockSpec((1, 1, D, D), lambda i, be, nu, nv: (l, be[i], 0, 0)),
                  pl.BlockSpec((1, 1, 1, D), lambda i, be, nu, nv: (l, be[i], 0, 0))],
        out_specs=pl.BlockSpec(memory_space=pl.ANY),
        scratch_shapes=[pltpu.VMEM((2, MOE_BM, D), f32), pltpu.VMEM((2, MOE_BM, D), f32),
                        pltpu.SemaphoreType.DMA((2,)), pltpu.SemaphoreType.DMA((2,)),
                        pltpu.VMEM((D, 2 * D), bf16), pltpu.VMEM((D, D), bf16)])
    return pl.pallas_call(
        _expert_kernel, out_shape=_sds((TOP_K * T, D)), grid_spec=grid_spec,
        compiler_params=_cparams(("arbitrary",)), name="moe_experts")(
            block_e, nused, nvalid, inv, inv, xn, p["exp_w_gu"], p["exp_b_gu"].reshape(DEPTH, N_EXPERTS, 1, 2 * D),
            p["exp_w_down"], p["exp_b_down"].reshape(DEPTH, N_EXPERTS, 1, D))


def _combine_kernel(y4_ref, topw_ref, h_ref, mod_ref, o_ref):
    w = topw_ref[...]
    y = jnp.zeros((TM, D), f32)
    for k in range(TOP_K):
        y = y + y4_ref[k] * w[:, k:k + 1]
    o_ref[...] = h_ref[...] + mod_ref[0][5:6, :] * y


def combine(y4, topw, h, mod_tiles):
    t = h.shape[0]
    tok = pl.BlockSpec((TM, D), lambda i: (i, 0))
    return pl.pallas_call(
        _combine_kernel, out_shape=_sds(h.shape), grid=(t // TM,),
        in_specs=[pl.BlockSpec((TOP_K, TM, D), lambda i: (0, i, 0)),
                  pl.BlockSpec((TM, LANES), lambda i: (i, 0)), tok,
                  pl.BlockSpec((1, 6, D), lambda i: (i, 0, 0))],
        out_specs=tok,
        compiler_params=_cparams(("parallel",)), name="moe_combine")(y4.reshape(TOP_K, t, D), topw, h, mod_tiles)


def moe_layer(h, mod_tiles, p):
    xn, topi, topw, rank, counts = router(h, mod_tiles, p)
    cnt = counts[0, :N_EXPERTS]
    padded = (cnt + MOE_BM - 1) // MOE_BM * MOE_BM
    pend = jnp.cumsum(padded)
    pstart = jnp.zeros((1, LANES), i32).at[0, :N_EXPERTS].set(pend - padded)
    blk_row0 = jnp.arange(MOE_NBLK, dtype=i32) * MOE_BM
    block_e = jnp.minimum(jnp.sum((pend[None, :] <= blk_row0[:, None]).astype(i32), axis=1), N_EXPERTS - 1)
    nused = (pend[-1:] // MOE_BM).astype(i32)
    blk = jnp.arange(MOE_NBLK, dtype=i32)
    rows_before = (blk - (pend - padded)[block_e] // MOE_BM) * MOE_BM
    nvalid = jnp.where(blk < nused[0], jnp.clip(cnt[block_e] - rows_before, 0, MOE_BM), 0).astype(i32)
    dest = dest_rows(topi, rank, pstart)
    inv = inverse_map(dest[:, :TOP_K].reshape(-1))
    y4 = experts(block_e, nused, nvalid, inv, xn, p)
    return combine(y4, topw, h, mod_tiles)


def _block_diag(w):
    per = CW // LRU_BW
    w5 = w.reshape(2, D // CW, per, LRU_BW, LRU_BW)
    eye = jnp.eye(per, dtype=w.dtype)
    bd = w5[:, :, :, :, None, :] * eye[None, None, :, None, :, None]
    return bd.reshape(2, D // CW, CW, CW).astype(bf16)


def _layer_params(l, a):
    tile2 = lambda g: jnp.tile(g, 2).reshape(1, LANES)
    whole = ("exp_w_gu", "exp_b_gu", "exp_w_down", "exp_b_down", "w_in", "w_gate", "w_branch", "w_out")
    p = {k: v[l] for k, v in a.items() if k not in whole}
    p["layer"] = l
    for k in ("exp_w_gu", "exp_b_gu", "exp_w_down", "exp_b_down"):
        p[k] = a[k]
    p["w_in_bf"] = cast_bf16(a["w_in"], l)
    p["w_gate_bf"] = cast_bf16(a["w_gate"], l)
    p["w_branch_bf"] = cast_bf16(a["w_branch"].reshape(DEPTH, 4 * D, D), l).reshape(4, D, D)
    p["w_out_bf"] = cast_bf16(a["w_out"], l)
    p["pool_w_bf"] = cast_bf16(a["pool_w"].reshape(DEPTH, D, CW), l).reshape(4, CW, CW)
    p["lru_wr_bd"] = _block_diag(p["lru_w_r"])
    p["lru_wi_bd"] = _block_diag(p["lru_w_i"])
    p["q_norm_g2"] = tile2(p["q_norm_g"])
    p["k_norm_g2"] = tile2(p["k_norm_g"])
    p["router_w_pad"] = jnp.pad(p["router_w"], ((0, 0), (0, LANES - N_EXPERTS)))
    p["router_b_pad"] = jnp.pad(p["router_b"], (0, LANES - N_EXPERTS)).reshape(1, LANES)
    return p


def kernel(x_prompt, x_sample, cache_k, cache_v, state_lru, c, c_ctx, norm1_g, norm2_g, w_mod, b_mod, w_in,
           lru_conv_w, lru_conv_b, lru_w_r, lru_b_r, lru_w_i, lru_b_i, lru_lambda, conv_dw_w, conv_dw_b,
           conv_ln_g, conv_ln_b, q_norm_g, k_norm_g, attn_sink, pool_w, pool_scale, w_branch, w_gate, b_gate,
           w_out, router_w, router_b, exp_w_gu, exp_b_gu, exp_w_down, exp_b_down):
    weights = dict(norm1_g=norm1_g, norm2_g=norm2_g, w_in=w_in, lru_conv_w=lru_conv_w, lru_conv_b=lru_conv_b,
                   lru_w_r=lru_w_r, lru_b_r=lru_b_r, lru_w_i=lru_w_i, lru_b_i=lru_b_i, lru_lambda=lru_lambda,
                   conv_dw_w=conv_dw_w, conv_dw_b=conv_dw_b, conv_ln_g=conv_ln_g, conv_ln_b=conv_ln_b,
                   q_norm_g=q_norm_g, k_norm_g=k_norm_g, attn_sink=attn_sink, pool_w=pool_w, pool_scale=pool_scale,
                   w_branch=w_branch, w_gate=w_gate, b_gate=b_gate, w_out=w_out, router_w=router_w,
                   router_b=router_b, exp_w_gu=exp_w_gu, exp_b_gu=exp_b_gu, exp_w_down=exp_w_down,
                   exp_b_down=exp_b_down)
    kvw = N_KV * HEAD_DIM
    n_cond = 2 * SUBLANES
    cond = jnp.concatenate([c_ctx[None, :], c, jnp.zeros((n_cond - 1 - N_LAT_B, D), f32)], axis=0)
    mod = modulation(cond, w_mod, b_mod)
    tile_start = jnp.arange(T // TM) * TM
    tile_row = jnp.where(tile_start < T_CTX, 0, 1 + (tile_start - T_CTX) // N_LAT)
    h = jnp.concatenate([x_prompt.reshape(T_CTX, D), x_sample.reshape(T_LAT, D)], axis=0)
    ck = cache_k.reshape(N_LAT_B, DEPTH, PAST, kvw)
    cv = cache_v.reshape(N_LAT_B, DEPTH, PAST, kvw)
    rope_cos, rope_sin = rope_tables()
    zero_state = jnp.zeros((N_CTX_B, 2, D), f32)
    new_k, new_v, new_s = [], [], []
    for l in range(DEPTH):
        p = _layer_params(l, weights)
        mod_tiles = mod[l][tile_row].reshape(T // TM, 6, D)
        xn = norm1(h, p["norm1_g"], mod_tiles)
        u = in_proj(xn, p["w_in_bf"])
        lat_blk0 = T_CTX // N_LAT
        lru_c, st_c = lru_branch(u, p, zero_state, N_CTX_B, N_CTX, 0)
        lru_l, _ = lru_branch(u, p, state_lru[:, l], N_LAT_B, N_LAT, lat_blk0)
        conv = (conf_branch(u, p, N_CTX_B, N_CTX, 0), conf_branch(u, p, N_LAT_B, N_LAT, lat_blk0))
        pool = (pool_branch(u, p, N_CTX_B, N_CTX, 0), pool_branch(u, p, N_LAT_B, N_LAT, lat_blk0))
        att_c, k_l, v_l = ctx_attention(u, p)
        att_l = lat_attention(u, p, ck, cv, rope_cos, rope_sin)
        h = merge(h, xn, (lru_c, lru_l), conv, (att_c, att_l), pool, mod_tiles, p)
        h = moe_layer(h, mod_tiles, p)
        new_k.append(k_l.reshape(N_CTX_B, N_CTX, N_KV, HEAD_DIM))
        new_v.append(v_l.reshape(N_CTX_B, N_CTX, N_KV, HEAD_DIM))
        new_s.append(st_c)
    y_prompt = h[:T_CTX].reshape(N_CTX_B, N_CTX, D)
    y_sample = h[T_CTX:].reshape(N_LAT_B, N_LAT, D)
    return (y_prompt, y_sample, jnp.stack(new_k, axis=1), jnp.stack(new_v, axis=1), jnp.stack(new_s, axis=1))
```

```python
import functools

import jax
import jax.numpy as jnp
from jax import lax
from jax.experimental import pallas as pl
from jax.experimental.pallas import tpu as pltpu

f32 = jnp.float32
bf16 = jnp.bfloat16
i32 = jnp.int32

D = 1024
N_CTX_B, N_CTX = 16, 256
N_LAT_B, N_LAT = 8, 1024
T_CTX = N_CTX_B * N_CTX
T_LAT = N_LAT_B * N_LAT
T = T_CTX + T_LAT
DEPTH = 2
PAST = 512
GRID_W = 64
IN_W = 6656
COL_UX, COL_UY, COL_UC, COL_UQ, COL_UK, COL_UV, COL_UP = 0, 1024, 2048, 4096, 5120, 5376, 5632
LRU_BW = 64
LRU_CONV = 4
LRU_C = 8.0
CONV_K = 31
N_HEADS, N_KV, HEAD_DIM = 16, 4, 64
POOL_SIZES = (2, 4, 8, 16)
N_EXPERTS, TOP_K = 32, 4
SWIGLU_LIMIT, SWIGLU_ALPHA = 7.0, 1.702
EPS = 1e-6
NEG_INF = -1e30
ROPE_BASE = 10000.0
LOG2E = 1.4426950408889634

LANES = 128
SUBLANES = 8
VMEM_LIMIT = 56 * 1024 * 1024

CW = 256
RC = 128
TQ = 128
TM = 256
MOE_BM = 256
N_ASSIGN = T * TOP_K
MOE_NBLK = N_ASSIGN // MOE_BM + N_EXPERTS
MOE_CAP = MOE_NBLK * MOE_BM


def _sds(shape, dt=f32):
    return jax.ShapeDtypeStruct(shape, dt)


def _cparams(sem, vmem=VMEM_LIMIT):
    return pltpu.CompilerParams(dimension_semantics=sem, vmem_limit_bytes=vmem)


def _sigmoid(x):
    return 0.5 * jnp.tanh(0.5 * x) + 0.5


def _log1p(z):
    u = 1.0 + z
    d = u - 1.0
    return jnp.where(d == 0.0, z, jnp.log(u) * (z / jnp.where(d == 0.0, 1.0, d)))


def _cast_kernel(x_ref, o_ref):
    o_ref[...] = x_ref[...].astype(o_ref.dtype)


def cast_bf16(w, l):
    _, r, c = w.shape
    tr = 256
    return pl.pallas_call(
        _cast_kernel, out_shape=_sds((r, c), bf16), grid=(r // tr,),
        in_specs=[pl.BlockSpec((None, tr, c), lambda i: (l, i, 0))],
        out_specs=pl.BlockSpec((tr, c), lambda i: (i, 0)),
        compiler_params=_cparams(("parallel",)), name="cast_bf16")(w)


def _mod_kernel(c_ref, w_ref, b_ref, o_ref):
    x = c_ref[...]
    s = x * _sigmoid(x)
    o_ref[0] = jnp.dot(s.astype(bf16), w_ref[0].astype(bf16), preferred_element_type=f32) + b_ref[0]


def modulation(cond, w_mod, b_mod):
    r = cond.shape[0]
    tn = 1536
    return pl.pallas_call(
        _mod_kernel, out_shape=_sds((DEPTH, r, 6 * D)), grid=(DEPTH, 6 * D // tn),
        in_specs=[pl.BlockSpec((r, D), lambda l, j: (0, 0)),
                  pl.BlockSpec((1, D, tn), lambda l, j: (l, 0, j)),
                  pl.BlockSpec((1, 1, tn), lambda l, j: (l, 0, j))],
        out_specs=pl.BlockSpec((1, r, tn), lambda l, j: (l, 0, j)),
        compiler_params=_cparams(("parallel", "parallel")), name="modulation")(
            cond, w_mod, b_mod.reshape(DEPTH, 1, 6 * D))


def _rms_mod(x, g, shift, scale):
    ms = jnp.mean(x * x, axis=-1, keepdims=True)
    return (x * lax.rsqrt(ms + EPS) * g) * (1.0 + scale) + shift


def _norm1_kernel(h_ref, g_ref, mod_ref, o_ref):
    m = mod_ref[0]
    o_ref[...] = _rms_mod(h_ref[...], g_ref[...], m[0:1, :], m[1:2, :]).astype(bf16)


def norm1(h, g, mod_tiles):
    nt = h.shape[0] // TM
    return pl.pallas_call(
        _norm1_kernel, out_shape=_sds(h.shape, bf16), grid=(nt,),
        in_specs=[pl.BlockSpec((TM, D), lambda i: (i, 0)),
                  pl.BlockSpec((1, D), lambda i: (0, 0)),
                  pl.BlockSpec((1, 6, D), lambda i: (i, 0, 0))],
        out_specs=pl.BlockSpec((TM, D), lambda i: (i, 0)),
        compiler_params=_cparams(("parallel",)), name="norm1")(h, g.reshape(1, D), mod_tiles)


def _matmul_kernel(x_ref, w_ref, o_ref):
    o_ref[...] = jnp.dot(x_ref[...], w_ref[...], preferred_element_type=f32)


def in_proj(xn, w):
    t = xn.shape[0]
    tm, tn = 1024, 1664
    return pl.pallas_call(
        _matmul_kernel, out_shape=_sds((t, IN_W)), grid=(t // tm, IN_W // tn),
        in_specs=[pl.BlockSpec((tm, D), lambda i, j: (i, 0)),
                  pl.BlockSpec((D, tn), lambda i, j: (0, j))],
        out_specs=pl.BlockSpec((tm, tn), lambda i, j: (i, j)),
        compiler_params=_cparams(("parallel", "parallel")), name="in_proj")(xn, w)


def _gelu_tanh(x):
    return 0.5 * x * (1.0 + jnp.tanh(0.7978845608028654 * (x + 0.044715 * (x * x * x))))


def _lru_kernel(ux_ref, uy_ref, cw_ref, cb_ref, wr_ref, wi_ref, br_ref, bi_ref, lam_ref, h0_ref,
                out_ref, st_ref, xp_ref, a_ref, b_ref, hs_ref, *, n):
    pad = SUBLANES
    xp_ref[pl.ds(0, pad), :] = jnp.zeros((pad, CW), f32)
    xp_ref[pl.ds(n + pad, pad), :] = jnp.zeros((pad, CW), f32)
    xp_ref[pl.ds(pad, n), :] = ux_ref[...]
    left = LRU_CONV // 2
    lam = lam_ref[...]
    neg_c_sp = -LRU_C * (jnp.maximum(-lam, 0.0) + _log1p(jnp.exp(-jnp.abs(lam))))

    def coef_body(i, c):
        r0 = pl.multiple_of(i * RC, RC)
        w = xp_ref[pl.ds(r0, RC + 2 * pad), :]
        xc = jnp.zeros((RC, CW), f32) + cb_ref[...]
        for k in range(LRU_CONV):
            off = pad - left + k
            xc = xc + cw_ref[k:k + 1, :] * w[off:off + RC, :]
        xcb = xc.astype(bf16)
        for d in range(2):
            r = _sigmoid(jnp.dot(xcb, wr_ref[d, 0], preferred_element_type=f32) + br_ref[d:d + 1, :])
            g = _sigmoid(jnp.dot(xcb, wi_ref[d, 0], preferred_element_type=f32) + bi_ref[d:d + 1, :])
            log_a = r * neg_c_sp[d:d + 1, :]
            a = jnp.exp(log_a)
            th = jnp.tanh(log_a)
            one_m_a2 = (-2.0 * th) / (1.0 - th)
            a_ref[d, pl.ds(r0, RC), :] = a
            b_ref[d, pl.ds(r0, RC), :] = jnp.sqrt(one_m_a2) * (g * xc)
        return c

    lax.fori_loop(0, n // RC, coef_body, 0)

    row = lax.broadcasted_iota(i32, (SUBLANES, CW), 0)
    nchunk = n // SUBLANES

    def scan_body(j, carry):
        hf, hb = carry
        rf = pl.multiple_of(j * SUBLANES, SUBLANES)
        a = a_ref[0, pl.ds(rf, SUBLANES), :]
        b = b_ref[0, pl.ds(rf, SUBLANES), :]
        for sh in (1, 2, 4):
            a_s = jnp.where(row >= sh, pltpu.roll(a, sh, axis=0), 1.0)
            b_s = jnp.where(row >= sh, pltpu.roll(b, sh, axis=0), 0.0)
            b = a * b_s + b
            a = a * a_s
        h = a * hf + b
        hs_ref[0, pl.ds(rf, SUBLANES), :] = h
        hf = h[SUBLANES - 1:SUBLANES, :]
        rb = pl.multiple_of((nchunk - 1 - j) * SUBLANES, SUBLANES)
        a = a_ref[1, pl.ds(rb, SUBLANES), :]
        b = b_ref[1, pl.ds(rb, SUBLANES), :]
        for sh in (1, 2, 4):
            keep = row < SUBLANES - sh
            a_s = jnp.where(keep, pltpu.roll(a, SUBLANES - sh, axis=0), 1.0)
            b_s = jnp.where(keep, pltpu.roll(b, SUBLANES - sh, axis=0), 0.0)
            b = a * b_s + b
            a = a * a_s
        h = a * hb + b
        hs_ref[1, pl.ds(rb, SUBLANES), :] = h
        hb = h[0:1, :]
        return hf, hb

    h0 = h0_ref[0]
    hf, hb = lax.fori_loop(0, nchunk, scan_body, (h0[0:1, :], h0[1:2, :]), unroll=4)
    st_ref[0, 0:1, :] = hf
    st_ref[0, 1:2, :] = hb

    def out_body(i, c):
        r0 = pl.multiple_of(i * RC, RC)
        hsum = hs_ref[0, pl.ds(r0, RC), :] + hs_ref[1, pl.ds(r0, RC), :]
        out_ref[pl.ds(r0, RC), :] = hsum * _gelu_tanh(uy_ref[pl.ds(r0, RC), :])
        return c

    lax.fori_loop(0, n // RC, out_body, 0)


def lru_branch(u, p, h0, nb, n, row_blk0):
    nct = D // CW
    cu = COL_UY // CW
    kern = functools.partial(_lru_kernel, n=n)
    vec = lambda a: pl.BlockSpec((a, CW), lambda b, c: (0, c))
    call = pl.pallas_call(
        kern, out_shape=(_sds((nb * n, D)), _sds((nb, 2, D))), grid=(nb, nct),
        in_specs=[pl.BlockSpec((n, CW), lambda b, c: (row_blk0 + b, c)),
                  pl.BlockSpec((n, CW), lambda b, c: (row_blk0 + b, cu + c)),
                  vec(LRU_CONV), vec(1),
                  pl.BlockSpec((2, 1, CW, CW), lambda b, c: (0, c, 0, 0)),
                  pl.BlockSpec((2, 1, CW, CW), lambda b, c: (0, c, 0, 0)),
                  vec(2), vec(2), vec(2),
                  pl.BlockSpec((1, 2, CW), lambda b, c: (b, 0, c))],
        out_specs=(pl.BlockSpec((n, CW), lambda b, c: (b, c)),
                   pl.BlockSpec((1, 2, CW), lambda b, c: (b, 0, c))),
        scratch_shapes=[pltpu.VMEM((n + 2 * SUBLANES, CW), f32),
                        pltpu.VMEM((2, n, CW), f32), pltpu.VMEM((2, n, CW), f32), pltpu.VMEM((2, n, CW), f32)],
        compiler_params=_cparams(("parallel", "parallel")), name="lru_branch")
    return call(u, u, p["lru_conv_w"], p["lru_conv_b"].reshape(1, D), p["lru_wr_bd"], p["lru_wi_bd"],
                p["lru_b_r"], p["lru_b_i"], p["lru_lambda"], h0)


def _conf_kernel(ua_ref, ug_ref, w_ref, b_ref, o_ref, gp_ref, *, n):
    pad = 2 * SUBLANES
    left = CONV_K // 2
    gp_ref[pl.ds(0, pad), :] = jnp.zeros((pad, CW), f32)
    gp_ref[pl.ds(n + pad, pad), :] = jnp.zeros((pad, CW), f32)
    gp_ref[pl.ds(pad, n), :] = ua_ref[...] * _sigmoid(ug_ref[...])

    wn = RC + 2 * pad

    def body(i, c):
        r0 = pl.multiple_of(i * RC, RC)
        w = gp_ref[pl.ds(r0, wn), :]
        rolled = [w] + [pltpu.roll(w, wn - m, axis=0) for m in range(1, SUBLANES)]
        acc = jnp.zeros((RC, CW), f32) + b_ref[...]
        for k in range(CONV_K):
            off = pad - left + k
            q, m = off // SUBLANES, off % SUBLANES
            acc = acc + w_ref[k:k + 1, :] * rolled[m][q * SUBLANES:q * SUBLANES + RC, :]
        o_ref[pl.ds(r0, RC), :] = acc
        return c

    lax.fori_loop(0, n // RC, body, 0)


def conf_branch(u, p, nb, n, row_blk0):
    nct = D // CW
    ca, cg = COL_UC // CW, (COL_UC + D) // CW
    kern = functools.partial(_conf_kernel, n=n)
    call = pl.pallas_call(
        kern, out_shape=_sds((nb * n, D)), grid=(nb, nct),
        in_specs=[pl.BlockSpec((n, CW), lambda b, c: (row_blk0 + b, ca + c)),
                  pl.BlockSpec((n, CW), lambda b, c: (row_blk0 + b, cg + c)),
                  pl.BlockSpec((CONV_K, CW), lambda b, c: (0, c)),
                  pl.BlockSpec((1, CW), lambda b, c: (0, c))],
        out_specs=pl.BlockSpec((n, CW), lambda b, c: (b, c)),
        scratch_shapes=[pltpu.VMEM((n + 4 * SUBLANES, CW), f32)],
        compiler_params=_cparams(("parallel", "parallel")), name="conf_branch")
    return call(u, u, p["conv_dw_w"], p["conv_dw_b"].reshape(1, D))


def _pool_kernel(up_ref, w_ref, s_ref, o_ref, xp_ref, *, n):
    pad = SUBLANES
    gi = pl.program_id(1)
    half = jnp.left_shift(1, gi)
    xp_ref[pl.ds(0, pad), :] = jnp.zeros((pad, CW), f32)
    xp_ref[pl.ds(n + pad, pad), :] = jnp.zeros((pad, CW), f32)
    xp_ref[pl.ds(pad, n), :] = up_ref[...]
    wn = RC + 2 * pad

    def body(i, c):
        r0 = pl.multiple_of(i * RC, RC)
        w = xp_ref[pl.ds(r0, wn), :]
        s2 = w + pltpu.roll(w, 1, axis=0)
        s4 = pltpu.roll(s2, 1, axis=0) + pltpu.roll(s2, wn - 1, axis=0)
        s8 = pltpu.roll(s4, 2, axis=0) + pltpu.roll(s4, wn - 2, axis=0)
        s16 = pltpu.roll(s8, 4, axis=0) + pltpu.roll(s8, wn - 4, axis=0)
        s = jnp.where(gi == 0, s2, jnp.where(gi == 1, s4, jnp.where(gi == 2, s8, s16)))[pad:pad + RC, :]
        t = r0 + lax.broadcasted_iota(i32, (RC, CW), 0)
        cnt = (jnp.minimum(t + half, n) - jnp.maximum(t - half, 0)).astype(f32)
        pooled = s / cnt - w[pad:pad + RC, :]
        o_ref[pl.ds(r0, RC), :] = jnp.dot(pooled.astype(bf16), w_ref[0], preferred_element_type=f32) * s_ref[...]
        return c

    lax.fori_loop(0, n // RC, body, 0)


def pool_branch(u, p, nb, n, row_blk0):
    assert POOL_SIZES == (2, 4, 8, 16) and D // len(POOL_SIZES) == CW
    cp = COL_UP // CW
    kern = functools.partial(_pool_kernel, n=n)
    call = pl.pallas_call(
        kern, out_shape=_sds((nb * n, D)), grid=(nb, len(POOL_SIZES)),
        in_specs=[pl.BlockSpec((n, CW), lambda b, c: (row_blk0 + b, cp + c)),
                  pl.BlockSpec((1, CW, CW), lambda b, c: (c, 0, 0)),
                  pl.BlockSpec((1, CW), lambda b, c: (0, c))],
        out_specs=pl.BlockSpec((n, CW), lambda b, c: (b, c)),
        scratch_shapes=[pltpu.VMEM((n + 2 * SUBLANES, CW), f32)],
        compiler_params=_cparams(("parallel", "parallel")), name="pool_branch")
    return call(u, p["pool_w_bf"], p["pool_scale"].reshape(1, D))


def _head_norm(blk, g128, lane_lo):
    sq = blk * blk
    s_lo = jnp.sum(jnp.where(lane_lo, sq, 0.0), axis=-1, keepdims=True)
    s_hi = jnp.sum(jnp.where(lane_lo, 0.0, sq), axis=-1, keepdims=True)
    r = jnp.where(lane_lo, lax.rsqrt(s_lo * (1.0 / HEAD_DIM) + EPS), lax.rsqrt(s_hi * (1.0 / HEAD_DIM) + EPS))
    return blk * r * g128


def _rope(blk, cos, sin_signed, lane):
    partner = jnp.where((lane % 32) < 16, pltpu.roll(blk, LANES - 16, axis=1), pltpu.roll(blk, 16, axis=1))
    return blk * cos + partner * sin_signed


def _prep_kv(k_ref, v_ref, ks_ref, vs_ref, kg, lane_lo, rope=None, newk_ref=None, newv_ref=None):
    nk = k_ref.shape[0]
    lane = lax.broadcasted_iota(i32, (nk, LANES), 1)
    for tj in range(N_KV * HEAD_DIM // LANES):
        kt = _head_norm(k_ref[:, tj * LANES:(tj + 1) * LANES], kg, lane_lo(nk))
        if rope is not None:
            kt = _rope(kt, rope[0][...], rope[1][...], lane)
        vt = v_ref[:, tj * LANES:(tj + 1) * LANES]
        if newk_ref is not None:
            newk_ref[0, :, tj * LANES:(tj + 1) * LANES] = kt
            newv_ref[0, :, tj * LANES:(tj + 1) * LANES] = vt
        _store_low_half(kt, vt, ks_ref, vs_ref, tj, lane_lo(nk))


def _store_low_half(kt, vt, ks_ref, vs_ref, tj, lane_lo):
    ks_ref[2 * tj] = jnp.where(lane_lo, kt, 0.0).astype(bf16)
    vs_ref[2 * tj] = jnp.where(lane_lo, vt, 0.0).astype(bf16)
    ks_ref[2 * tj + 1] = jnp.where(lane_lo, pltpu.roll(kt, HEAD_DIM, axis=1), 0.0).astype(bf16)
    vs_ref[2 * tj + 1] = jnp.where(lane_lo, pltpu.roll(vt, HEAD_DIM, axis=1), 0.0).astype(bf16)


def _prep_cache(ck_ref, cv_ref, kc_ref, vc_ref, lane_lo):
    nk = ck_ref.shape[2]
    for tj in range(N_KV * HEAD_DIM // LANES):
        kt = ck_ref[0, 0, :, tj * LANES:(tj + 1) * LANES]
        vt = cv_ref[0, 0, :, tj * LANES:(tj + 1) * LANES]
        _store_low_half(kt, vt, kc_ref, vc_ref, tj, lane_lo(nk))


def _qk(qm, k):
    return lax.dot_general(qm, k, (((1,), (1,)), ((), ())), preferred_element_type=f32)


def _attend(sink_ref, q_tiles, segments, att_ref):
    per_kv = N_HEADS // N_KV
    assert per_kv == 4
    rows = per_kv * TQ
    row = lax.broadcasted_iota(i32, (rows, 1), 0)
    masks = [None if mk is None else jnp.concatenate([mk] * per_kv, axis=0) for _, _, mk in segments]
    for g in range(N_KV):
        t0, t1 = q_tiles[2 * g], q_tiles[2 * g + 1]
        qm = jnp.concatenate([t0, t1, pltpu.roll(t0, HEAD_DIM, axis=1), pltpu.roll(t1, HEAD_DIM, axis=1)],
                             axis=0).astype(bf16)
        hd = [per_kv * g, per_kv * g + 2, per_kv * g + 1, per_kv * g + 3]
        sink = jnp.where(row < TQ, sink_ref[hd[0]],
                         jnp.where(row < 2 * TQ, sink_ref[hd[1]],
                                   jnp.where(row < 3 * TQ, sink_ref[hd[2]], sink_ref[hd[3]]))) * LOG2E
        scores = []
        m_el = None
        for (kget, _, _), mask in zip(segments, masks):
            s = _qk(qm, kget(g))
            if mask is not None:
                s = jnp.where(mask, s, NEG_INF)
            for c in range(s.shape[1] // LANES):
                t = s[:, c * LANES:(c + 1) * LANES]
                m_el = t if m_el is None else jnp.maximum(m_el, t)
            scores.append(s)
        m = jnp.maximum(jnp.max(m_el, axis=-1, keepdims=True), sink)
        d_el = jnp.zeros((rows, LANES), f32)
        o = jnp.zeros((rows, LANES), f32)
        for s, (_, vget, _) in zip(scores, segments):
            pr = jnp.exp2(s - m)
            for c in range(s.shape[1] // LANES):
                d_el = d_el + pr[:, c * LANES:(c + 1) * LANES]
            o = o + jnp.dot(pr.astype(bf16), vget(g), preferred_element_type=f32)
        den = jnp.exp2(sink - m) + jnp.sum(d_el, axis=-1, keepdims=True)
        o = o / den
        att_ref[:, 2 * g * LANES:(2 * g + 1) * LANES] = o[:TQ] + pltpu.roll(o[2 * TQ:3 * TQ], HEAD_DIM, axis=1)
        att_ref[:, (2 * g + 1) * LANES:(2 * g + 2) * LANES] = o[TQ:2 * TQ] + pltpu.roll(o[3 * TQ:], HEAD_DIM, axis=1)


def _q_tiles(q_ref, qg, rope=None):
    lane = lax.broadcasted_iota(i32, (TQ, LANES), 1)
    lane_lo = lane < HEAD_DIM
    tiles = []
    for j in range(N_HEADS // 2):
        qt = _head_norm(q_ref[:, j * LANES:(j + 1) * LANES], qg, lane_lo)
        if rope is not None:
            qt = _rope(qt, rope[0], rope[1], lane)
        tiles.append(qt * (HEAD_DIM ** -0.5 * LOG2E))
    return tiles


def _lane_lo_fn(nrows):
    return lax.broadcasted_iota(i32, (nrows, LANES), 1) < HEAD_DIM


def _ctx_attn_kernel(sink_ref, q_ref, k_ref, v_ref, qg_ref, kg_ref, att_ref, newk_ref, newv_ref, ks_ref, vs_ref):
    @pl.when(pl.program_id(1) == 0)
    def _():
        _prep_kv(k_ref, v_ref, ks_ref, vs_ref, kg_ref[...], _lane_lo_fn, None, newk_ref, newv_ref)

    tiles = _q_tiles(q_ref, qg_ref[...])
    seg = [(lambda g: ks_ref[g], lambda g: vs_ref[g], None)]
    _attend(sink_ref, tiles, seg, att_ref)


def ctx_attention(u, p):
    nq = N_CTX // TQ
    kvw = N_KV * HEAD_DIM
    return pl.pallas_call(
        _ctx_attn_kernel,
        out_shape=(_sds((T_CTX, D)), _sds((N_CTX_B, N_CTX, kvw)), _sds((N_CTX_B, N_CTX, kvw))),
        grid=(N_CTX_B, nq),
        in_specs=[pl.BlockSpec(memory_space=pltpu.SMEM),
                  pl.BlockSpec((TQ, D), lambda b, i: (b * nq + i, COL_UQ // D)),
                  pl.BlockSpec((N_CTX, kvw), lambda b, i: (b, COL_UK // kvw)),
                  pl.BlockSpec((N_CTX, kvw), lambda b, i: (b, COL_UV // kvw)),
                  pl.BlockSpec((1, LANES), lambda b, i: (0, 0)),
                  pl.BlockSpec((1, LANES), lambda b, i: (0, 0))],
        out_specs=(pl.BlockSpec((TQ, D), lambda b, i: (b * nq + i, 0)),
                   pl.BlockSpec((1, N_CTX, kvw), lambda b, i: (b, 0, 0)),
                   pl.BlockSpec((1, N_CTX, kvw), lambda b, i: (b, 0, 0))),
        scratch_shapes=[pltpu.VMEM((N_KV, N_CTX, LANES), bf16), pltpu.VMEM((N_KV, N_CTX, LANES), bf16)],
        compiler_params=_cparams(("parallel", "arbitrary")), name="ctx_attention")(
            p["attn_sink"], u, u, u, p["q_norm_g2"], p["k_norm_g2"])


def _lat_attn_kernel(sink_ref, q_ref, k_ref, v_ref, ck_ref, cv_ref, cosq_ref, sinq_ref, cosk_ref, sink_tab_ref,
                     qg_ref, kg_ref, att_ref, ks_ref, vs_ref, kc_ref, vc_ref):
    i = pl.program_id(1)
    nq = N_LAT // TQ

    @pl.when(i == 0)
    def _():
        _prep_kv(k_ref, v_ref, ks_ref, vs_ref, kg_ref[...], _lane_lo_fn, (cosk_ref, sink_tab_ref))
        _prep_cache(ck_ref, cv_ref, kc_ref, vc_ref, _lane_lo_fn)

    tiles = _q_tiles(q_ref, qg_ref[...], (cosq_ref[...], sinq_ref[...]))
    r = lax.broadcasted_iota(i32, (TQ, TQ), 0)
    c = lax.broadcasted_iota(i32, (TQ, TQ), 1)
    prev0 = pl.multiple_of(jnp.maximum(i - 1, 0) * TQ, TQ)
    cur0 = pl.multiple_of(i * TQ, TQ)
    next0 = pl.multiple_of(jnp.minimum(i + 1, nq - 1) * TQ, TQ)
    mask_prev = jnp.logical_and(c >= r, i > 0)
    mask_next = jnp.logical_and(c <= r, i < nq - 1)

    def seg(r0, mask):
        return (lambda g: ks_ref[g, pl.ds(r0, TQ), :], lambda g: vs_ref[g, pl.ds(r0, TQ), :], mask)

    segments = [seg(prev0, mask_prev), seg(cur0, None), seg(next0, mask_next),
                (lambda g: kc_ref[g], lambda g: vc_ref[g], None)]
    _attend(sink_ref, tiles, segments, att_ref)


def lat_attention(u, p, cache_k, cache_v, rope_cos, rope_sin):
    nq = N_LAT // TQ
    kvw = N_KV * HEAD_DIM
    l = p["layer"]
    qblk0 = T_CTX // TQ
    kblk0 = T_CTX // N_LAT
    call = pl.pallas_call(
        _lat_attn_kernel, out_shape=_sds((T_LAT, D)), grid=(N_LAT_B, nq),
        in_specs=[pl.BlockSpec(memory_space=pltpu.SMEM),
                  pl.BlockSpec((TQ, D), lambda b, i: (qblk0 + b * nq + i, COL_UQ // D)),
                  pl.BlockSpec((N_LAT, kvw), lambda b, i: (kblk0 + b, COL_UK // kvw)),
                  pl.BlockSpec((N_LAT, kvw), lambda b, i: (kblk0 + b, COL_UV // kvw)),
                  pl.BlockSpec((1, 1, PAST, kvw), lambda b, i: (b, l, 0, 0)),
                  pl.BlockSpec((1, 1, PAST, kvw), lambda b, i: (b, l, 0, 0)),
                  pl.BlockSpec((TQ, LANES), lambda b, i: (i, 0)),
                  pl.BlockSpec((TQ, LANES), lambda b, i: (i, 0)),
                  pl.BlockSpec((N_LAT, LANES), lambda b, i: (0, 0)),
                  pl.BlockSpec((N_LAT, LANES), lambda b, i: (0, 0)),
                  pl.BlockSpec((1, LANES), lambda b, i: (0, 0)),
                  pl.BlockSpec((1, LANES), lambda b, i: (0, 0))],
        out_specs=pl.BlockSpec((TQ, D), lambda b, i: (b * nq + i, 0)),
        scratch_shapes=[pltpu.VMEM((N_KV, N_LAT, LANES), bf16), pltpu.VMEM((N_KV, N_LAT, LANES), bf16),
                        pltpu.VMEM((N_KV, PAST, LANES), bf16), pltpu.VMEM((N_KV, PAST, LANES), bf16)],
        compiler_params=_cparams(("parallel", "arbitrary")), name="lat_attention")
    return call(p["attn_sink"], u, u, u, cache_k, cache_v, rope_cos, rope_sin, rope_cos, rope_sin,
                p["q_norm_g2"], p["k_norm_g2"])


def rope_tables():
    pos = jnp.arange(N_LAT)
    rows = (pos // GRID_W).astype(f32)
    cols = (pos % GRID_W).astype(f32)
    half = HEAD_DIM // 2
    freqs = ROPE_BASE ** (-jnp.arange(0, half, 2, dtype=f32) / half)
    lane = jnp.arange(LANES)
    within = lane % half
    fidx = within % (half // 2)
    use_cols = (lane % HEAD_DIM) >= half
    ang = jnp.where(use_cols[None, :], cols[:, None], rows[:, None]) * freqs[fidx][None, :]
    sign = jnp.where(within < half // 2, -1.0, 1.0)
    return jnp.cos(ang), jnp.sin(ang) * sign[None, :]


def _merge_kernel(h_ref, xn_ref, lru_c, lru_l, conv_c, conv_l, att_c, att_l, pool_c, pool_l, mod_ref, lng_ref, lnb_ref,
                  wg_ref, bg_ref, wb_ref, wo_ref, o_ref):
    is_ctx = pl.program_id(0) < T_CTX // TM
    pick = lambda c_ref, l_ref: jnp.where(is_ctx, c_ref[...], l_ref[...])
    xn = xn_ref[...]
    hc = pick(conv_c, conv_l)
    mu = jnp.mean(hc, axis=-1, keepdims=True)
    xc = hc - mu
    var = jnp.mean(xc * xc, axis=-1, keepdims=True)
    y = xc * lax.rsqrt(var + EPS) * lng_ref[...] + lnb_ref[...]
    conv = y * _sigmoid(y)
    merged = jnp.zeros((TM, D), f32)
    for j, br in enumerate((pick(lru_c, lru_l), conv, pick(att_c, att_l), pick(pool_c, pool_l))):
        gate = _sigmoid(jnp.dot(xn, wg_ref[:, j * D:(j + 1) * D], preferred_element_type=f32)
                        + bg_ref[:, j * D:(j + 1) * D])
        proj = jnp.dot(br.astype(bf16), wb_ref[j], preferred_element_type=f32)
        merged = merged + gate * proj
    out = jnp.dot(merged.astype(bf16), wo_ref[...], preferred_element_type=f32)
    o_ref[...] = h_ref[...] + mod_ref[0][2:3, :] * out


def merge(h, xn, lru, conv, att, pool, mod_tiles, p):
    nt = h.shape[0] // TM
    n_ctx_tiles = T_CTX // TM
    tok = pl.BlockSpec((TM, D), lambda i: (i, 0))
    tok_c = pl.BlockSpec((TM, D), lambda i: (jnp.minimum(i, n_ctx_tiles - 1), 0))
    tok_l = pl.BlockSpec((TM, D), lambda i: (jnp.maximum(i - n_ctx_tiles, 0), 0))
    const = lambda shape: pl.BlockSpec(shape, lambda i: (0,) * len(shape), pipeline_mode=pl.Buffered(1))
    return pl.pallas_call(
        _merge_kernel, out_shape=_sds(h.shape), grid=(nt,),
        in_specs=[tok, tok, tok_c, tok_l, tok_c, tok_l, tok_c, tok_l, tok_c, tok_l,
                  pl.BlockSpec((1, 6, D), lambda i: (i, 0, 0)),
                  const((1, D)), const((1, D)),
                  const((D, 4 * D)), const((1, 4 * D)), const((4, D, D)), const((D, D))],
        out_specs=tok,
        compiler_params=_cparams(("parallel",)), name="merge")(
            h, xn, *lru, *conv, *att, *pool, mod_tiles, p["conv_ln_g"].reshape(1, D), p["conv_ln_b"].reshape(1, D),
            p["w_gate_bf"], p["b_gate"].reshape(1, 4 * D), p["w_branch_bf"], p["w_out_bf"])


def _router_kernel(h_ref, g_ref, mod_ref, rw_ref, rb_ref, xn_ref, topi_ref, topw_ref, rank_ref, cnt_ref, carry_ref):
    step = pl.program_id(0)

    @pl.when(step == 0)
    def _():
        carry_ref[...] = jnp.zeros_like(carry_ref)

    m = mod_ref[0]
    xn = _rms_mod(h_ref[...], g_ref[...], m[3:4, :], m[4:5, :])
    xn_ref[...] = xn
    logits = jnp.dot(xn, rw_ref[...], preferred_element_type=f32, precision=lax.Precision.HIGHEST) + rb_ref[...]
    lane = lax.broadcasted_iota(i32, (TM, LANES), 1).astype(f32)
    lg = jnp.where(lane < N_EXPERTS, logits, NEG_INF)
    vals, idxs = [], []
    for _ in range(TOP_K):
        mx = jnp.max(lg, axis=-1, keepdims=True)
        idx = jnp.min(jnp.where(lg == mx, lane, float(LANES)), axis=-1, keepdims=True)
        vals.append(mx)
        idxs.append(idx)
        lg = jnp.where(lane == idx, -3e38, lg)
    exps = [jnp.exp(v - vals[0]) for v in vals]
    den = exps[0] + exps[1] + exps[2] + exps[3]
    cnt = jnp.zeros((TM, LANES), f32)
    for idx in idxs:
        cnt = cnt + jnp.where(lane == idx, 1.0, 0.0)
    rr = lax.broadcasted_iota(i32, (TM, TM), 0)
    cc = lax.broadcasted_iota(i32, (TM, TM), 1)
    tri = jnp.where(rr > cc, 1.0, 0.0).astype(bf16)
    before = jnp.dot(tri, cnt.astype(bf16), preferred_element_type=f32) + carry_ref[0:1, :]
    topi = jnp.zeros((TM, LANES), f32)
    topw = jnp.zeros((TM, LANES), f32)
    rank = jnp.zeros((TM, LANES), f32)
    for k in range(TOP_K):
        rk = jnp.sum(jnp.where(lane == idxs[k], before, 0.0), axis=-1, keepdims=True)
        topi = jnp.where(lane == k, idxs[k], topi)
        topw = jnp.where(lane == k, exps[k] / den, topw)
        rank = jnp.where(lane == k, rk, rank)
    topi_ref[...] = topi.astype(i32)
    topw_ref[...] = topw
    rank_ref[...] = rank.astype(i32)
    total = carry_ref[0:1, :] + jnp.sum(cnt, axis=0, keepdims=True)
    carry_ref[...] = jnp.broadcast_to(total, carry_ref.shape)
    cnt_ref[...] = jnp.broadcast_to(total, cnt_ref.shape).astype(i32)


def router(h, mod_tiles, p):
    nt = h.shape[0] // TM
    tok = lambda w, dt=f32: pl.BlockSpec((TM, w), lambda i: (i, 0))
    t = h.shape[0]
    return pl.pallas_call(
        _router_kernel,
        out_shape=(_sds((t, D)), _sds((t, LANES), i32), _sds((t, LANES)), _sds((t, LANES), i32),
                   _sds((SUBLANES, LANES), i32)),
        grid=(nt,),
        in_specs=[tok(D), pl.BlockSpec((1, D), lambda i: (0, 0)), pl.BlockSpec((1, 6, D), lambda i: (i, 0, 0)),
                  pl.BlockSpec((D, LANES), lambda i: (0, 0)), pl.BlockSpec((1, LANES), lambda i: (0, 0))],
        out_specs=(tok(D), tok(LANES), tok(LANES), tok(LANES), pl.BlockSpec((SUBLANES, LANES), lambda i: (0, 0))),
        scratch_shapes=[pltpu.VMEM((SUBLANES, LANES), f32)],
        compiler_params=_cparams(("arbitrary",)), name="router")(
            h, p["norm2_g"].reshape(1, D), mod_tiles, p["router_w_pad"], p["router_b_pad"])


def _dest_kernel(topi_ref, rank_ref, pstart_ref, o_ref):
    lane = lax.broadcasted_iota(i32, (TM, LANES), 1).astype(f32)
    topi = topi_ref[...].astype(f32)
    pstart = pstart_ref[...].astype(f32)
    dest = rank_ref[...].astype(f32)
    for k in range(TOP_K):
        e = jnp.sum(jnp.where(lane == k, topi, 0.0), axis=-1, keepdims=True)
        ps = jnp.sum(jnp.where(lane == e, pstart, 0.0), axis=-1, keepdims=True)
        dest = dest + jnp.where(lane == k, ps, 0.0)
    o_ref[...] = dest.astype(i32)


def dest_rows(topi, rank, pstart):
    t = topi.shape[0]
    tok = pl.BlockSpec((TM, LANES), lambda i: (i, 0))
    return pl.pallas_call(
        _dest_kernel, out_shape=_sds((t, LANES), i32), grid=(t // TM,),
        in_specs=[tok, tok, pl.BlockSpec((1, LANES), lambda i: (0, 0))], out_specs=tok,
        compiler_params=_cparams(("parallel",)), name="dest_rows")(topi, rank, pstart)


LOG2_TOP_K = 2
assert 1 << LOG2_TOP_K == TOP_K


def _inv_kernel(dest_ref, inv_ref):
    i = pl.program_id(0)
    n = TM * TOP_K

    @pl.when(i == 0)
    def _():
        def zero(j, c):
            inv_ref[j] = 0
            return c

        lax.fori_loop(0, MOE_CAP, zero, 0, unroll=8)

    def body(j, c):
        inv_ref[dest_ref[j]] = i * n + j
        return c

    lax.fori_loop(0, n, body, 0, unroll=8)


def inverse_map(dest_flat):
    return pl.pallas_call(
        _inv_kernel, out_shape=_sds((MOE_CAP,), i32), grid=(T // TM,),
        in_specs=[pl.BlockSpec((TM * TOP_K,), lambda i: (i,), memory_space=pltpu.SMEM)],
        out_specs=pl.BlockSpec(memory_space=pltpu.SMEM),
        compiler_params=_cparams(("arbitrary",)), name="moe_inverse_map")(dest_flat)


def _expert_kernel(be_ref, nused_ref, nvalid_ref, inv_cur, inv_nxt, x_hbm, wgu_ref, bgu_ref, wdn_ref, bdn_ref,
                   y_hbm, xbuf, obuf, gsem, ssem, wgu_bf, wdn_bf):
    i = pl.program_id(0)
    nused = nused_ref[0]

    def issue_gather(inv_ref, s):
        for r in range(MOE_BM):
            tok = lax.shift_right_logical(inv_ref[r], LOG2_TOP_K)
            pltpu.make_async_copy(x_hbm.at[pl.ds(tok, 1), :], xbuf.at[s, pl.ds(r, 1), :], gsem.at[s]).start()

    def wait_gather(s):
        pltpu.make_async_copy(x_hbm.at[pl.ds(0, MOE_BM), :], xbuf.at[s], gsem.at[s]).wait()

    def issue_scatter(s, nv):
        for r in range(MOE_BM):
            a = inv_cur[r]
            real = (a & (TOP_K - 1)) * T + lax.shift_right_logical(a, LOG2_TOP_K)
            dst = jnp.where(r < nv, real, TOP_K * T + s * MOE_BM + r)
            pltpu.make_async_copy(obuf.at[s, pl.ds(r, 1), :], y_hbm.at[pl.ds(dst, 1), :],
                                  ssem.at[s]).start(priority=1)

    def wait_scatter(s):
        pltpu.make_async_copy(obuf.at[s], y_hbm.at[pl.ds(0, MOE_BM), :], ssem.at[s]).wait()

    @pl.when(i == 0)
    def _():
        issue_gather(inv_cur, 0)
        obuf[1] = jnp.zeros((MOE_BM, D), f32)
        dumps = [pltpu.make_async_copy(obuf.at[1], y_hbm.at[pl.ds(TOP_K * T + s * MOE_BM, MOE_BM), :], ssem.at[1])
                 for s in range(2)]
        for cp in dumps:
            cp.start()
        for cp in dumps:
            cp.wait()

    e = be_ref[i]
    prev = be_ref[jnp.maximum(i - 1, 0)]

    @pl.when(jnp.logical_or(i == 0, e != prev))
    def _():
        wgu_bf[...] = wgu_ref[0, 0].astype(bf16)
        wdn_bf[...] = wdn_ref[0, 0].astype(bf16)

    def step(slot):
        other = 1 - slot

        @pl.when(i >= 2)
        def _():
            wait_scatter(slot)

        wait_gather(slot)
        issue_gather(inv_nxt, other)
        x = xbuf[slot].astype(bf16)
        h = jnp.dot(x, wgu_bf[...], preferred_element_type=f32) + bgu_ref[0, 0]
        gate = jnp.minimum(h[:, :D], SWIGLU_LIMIT)
        lin = jnp.clip(h[:, D:], -SWIGLU_LIMIT, SWIGLU_LIMIT)
        act = (lin + 1.0) * (gate * _sigmoid(SWIGLU_ALPHA * gate))
        obuf[slot] = jnp.dot(act.astype(bf16), wdn_bf[...], preferred_element_type=f32) + bdn_ref[0, 0]
        issue_scatter(slot, nvalid_ref[i])

        @pl.when(i == nused - 1)
        def _():
            wait_gather(other)
            wait_scatter(slot)

            @pl.when(i >= 1)
            def _():
                wait_scatter(other)

    for parity in range(2):
        pl.when(jnp.logical_and(i < nused, lax.rem(i, 2) == parity))(functools.partial(step, parity))


def experts(block_e, nused, nvalid, inv, xn, p):
    l = p["layer"]
    grid_spec = pltpu.PrefetchScalarGridSpec(
        num_scalar_prefetch=3, grid=(MOE_NBLK,),
        in_specs=[pl.BlockSpec((MOE_BM,), lambda i, be, nu, nv: (i,), memory_space=pltpu.SMEM),
                  pl.BlockSpec((MOE_BM,), lambda i, be, nu, nv: (jnp.minimum(i + 1, MOE_NBLK - 1),),
                               memory_space=pltpu.SMEM),
                  pl.BlockSpec(memory_space=pl.ANY),
                  pl.BlockSpec((1, 1, D, 2 * D), lambda i, be, nu, nv: (l, be[i], 0, 0)),
                  pl.BlockSpec((1, 1, 1, 2 * D), lambda i, be, nu, nv: (l, be[i], 0, 0)),
                  pl.BlockSpec((1, 1, D, D), lambda i, be, nu, nv: (l, be[i], 0, 0)),
                  pl.BlockSpec((1, 1, 1, D), lambda i, be, nu, nv: (l, be[i], 0, 0))],
        out_specs=pl.BlockSpec(memory_space=pl.ANY),
        scratch_shapes=[pltpu.VMEM((2, MOE_BM, D), f32), pltpu.VMEM((2, MOE_BM, D), f32),
                        pltpu.SemaphoreType.DMA((2,)), pltpu.SemaphoreType.DMA((2,)),
                        pltpu.VMEM((D, 2 * D), bf16), pltpu.VMEM((D, D), bf16)])
    return pl.pallas_call(
        _expert_kernel, out_shape=_sds((TOP_K * T + 2 * MOE_BM, D)), grid_spec=grid_spec,
        compiler_params=_cparams(("arbitrary",)), name="moe_experts")(
            block_e, nused, nvalid, inv, inv, xn, p["exp_w_gu"], p["exp_b_gu"].reshape(DEPTH, N_EXPERTS, 1, 2 * D),
            p["exp_w_down"], p["exp_b_down"].reshape(DEPTH, N_EXPERTS, 1, D))


def _combine_kernel(y0_ref, y1_ref, y2_ref, y3_ref, topw_ref, h_ref, mod_ref, o_ref):
    w = topw_ref[...]
    y = jnp.zeros((TM, D), f32)
    for k, y_ref in enumerate((y0_ref, y1_ref, y2_ref, y3_ref)):
        y = y + y_ref[...] * w[:, k:k + 1]
    o_ref[...] = h_ref[...] + mod_ref[0][5:6, :] * y


def combine(y4, topw, h, mod_tiles):
    t = h.shape[0]
    nt = t // TM
    tok = pl.BlockSpec((TM, D), lambda i: (i, 0))
    slot = lambda k: pl.BlockSpec((TM, D), lambda i: (k * nt + i, 0))
    return pl.pallas_call(
        _combine_kernel, out_shape=_sds(h.shape), grid=(nt,),
        in_specs=[slot(0), slot(1), slot(2), slot(3),
                  pl.BlockSpec((TM, LANES), lambda i: (i, 0)), tok,
                  pl.BlockSpec((1, 6, D), lambda i: (i, 0, 0))],
        out_specs=tok,
        compiler_params=_cparams(("parallel",)), name="moe_combine")(y4, y4, y4, y4, topw, h, mod_tiles)


def moe_layer(h, mod_tiles, p):
    xn, topi, topw, rank, counts = router(h, mod_tiles, p)
    cnt = counts[0, :N_EXPERTS]
    padded = (cnt + MOE_BM - 1) // MOE_BM * MOE_BM
    pend = jnp.cumsum(padded)
    pstart = jnp.zeros((1, LANES), i32).at[0, :N_EXPERTS].set(pend - padded)
    blk_row0 = jnp.arange(MOE_NBLK, dtype=i32) * MOE_BM
    block_e = jnp.minimum(jnp.sum((pend[None, :] <= blk_row0[:, None]).astype(i32), axis=1), N_EXPERTS - 1)
    nused = (pend[-1:] // MOE_BM).astype(i32)
    blk = jnp.arange(MOE_NBLK, dtype=i32)
    rows_before = (blk - (pend - padded)[block_e] // MOE_BM) * MOE_BM
    nvalid = jnp.where(blk < nused[0], jnp.clip(cnt[block_e] - rows_before, 0, MOE_BM), 0).astype(i32)
    dest = dest_rows(topi, rank, pstart)
    inv = inverse_map(dest[:, :TOP_K].reshape(-1))
    y4 = experts(block_e, nused, nvalid, inv, xn, p)
    return combine(y4, topw, h, mod_tiles)


def _block_diag(w):
    per = CW // LRU_BW
    w5 = w.reshape(2, D // CW, per, LRU_BW, LRU_BW)
    eye = jnp.eye(per, dtype=w.dtype)
    bd = w5[:, :, :, :, None, :] * eye[None, None, :, None, :, None]
    return bd.reshape(2, D // CW, CW, CW).astype(bf16)


def _layer_params(l, a):
    tile2 = lambda g: jnp.tile(g, 2).reshape(1, LANES)
    whole = ("exp_w_gu", "exp_b_gu", "exp_w_down", "exp_b_down", "w_in", "w_gate", "w_branch", "w_out")
    p = {k: v[l] for k, v in a.items() if k not in whole}
    p["layer"] = l
    for k in ("exp_w_gu", "exp_b_gu", "exp_w_down", "exp_b_down"):
        p[k] = a[k]
    p["w_in_bf"] = cast_bf16(a["w_in"], l)
    p["w_gate_bf"] = cast_bf16(a["w_gate"], l)
    p["w_branch_bf"] = cast_bf16(a["w_branch"].reshape(DEPTH, 4 * D, D), l).reshape(4, D, D)
    p["w_out_bf"] = cast_bf16(a["w_out"], l)
    p["pool_w_bf"] = cast_bf16(a["pool_w"].reshape(DEPTH, D, CW), l).reshape(4, CW, CW)
    p["lru_wr_bd"] = _block_diag(p["lru_w_r"])
    p["lru_wi_bd"] = _block_diag(p["lru_w_i"])
    p["q_norm_g2"] = tile2(p["q_norm_g"])
    p["k_norm_g2"] = tile2(p["k_norm_g"])
    p["router_w_pad"] = jnp.pad(p["router_w"], ((0, 0), (0, LANES - N_EXPERTS)))
    p["router_b_pad"] = jnp.pad(p["router_b"], (0, LANES - N_EXPERTS)).reshape(1, LANES)
    return p


def kernel(x_prompt, x_sample, cache_k, cache_v, state_lru, c, c_ctx, norm1_g, norm2_g, w_mod, b_mod, w_in,
           lru_conv_w, lru_conv_b, lru_w_r, lru_b_r, lru_w_i, lru_b_i, lru_lambda, conv_dw_w, conv_dw_b,
           conv_ln_g, conv_ln_b, q_norm_g, k_norm_g, attn_sink, pool_w, pool_scale, w_branch, w_gate, b_gate,
           w_out, router_w, router_b, exp_w_gu, exp_b_gu, exp_w_down, exp_b_down):
    weights = dict(norm1_g=norm1_g, norm2_g=norm2_g, w_in=w_in, lru_conv_w=lru_conv_w, lru_conv_b=lru_conv_b,
                   lru_w_r=lru_w_r, lru_b_r=lru_b_r, lru_w_i=lru_w_i, lru_b_i=lru_b_i, lru_lambda=lru_lambda,
                   conv_dw_w=conv_dw_w, conv_dw_b=conv_dw_b, conv_ln_g=conv_ln_g, conv_ln_b=conv_ln_b,
                   q_norm_g=q_norm_g, k_norm_g=k_norm_g, attn_sink=attn_sink, pool_w=pool_w, pool_scale=pool_scale,
                   w_branch=w_branch, w_gate=w_gate, b_gate=b_gate, w_out=w_out, router_w=router_w,
                   router_b=router_b, exp_w_gu=exp_w_gu, exp_b_gu=exp_b_gu, exp_w_down=exp_w_down,
                   exp_b_down=exp_b_down)
    kvw = N_KV * HEAD_DIM
    n_cond = 2 * SUBLANES
    cond = jnp.concatenate([c_ctx[None, :], c, jnp.zeros((n_cond - 1 - N_LAT_B, D), f32)], axis=0)
    mod = modulation(cond, w_mod, b_mod)
    tile_start = jnp.arange(T // TM) * TM
    tile_row = jnp.where(tile_start < T_CTX, 0, 1 + (tile_start - T_CTX) // N_LAT)
    h = jnp.concatenate([x_prompt.reshape(T_CTX, D), x_sample.reshape(T_LAT, D)], axis=0)
    ck = cache_k.reshape(N_LAT_B, DEPTH, PAST, kvw)
    cv = cache_v.reshape(N_LAT_B, DEPTH, PAST, kvw)
    rope_cos, rope_sin = rope_tables()
    zero_state = jnp.zeros((N_CTX_B, 2, D), f32)
    new_k, new_v, new_s = [], [], []
    for l in range(DEPTH):
        p = _layer_params(l, weights)
        mod_tiles = mod[l][tile_row].reshape(T // TM, 6, D)
        xn = norm1(h, p["norm1_g"], mod_tiles)
        u = in_proj(xn, p["w_in_bf"])
        lat_blk0 = T_CTX // N_LAT
        lru_c, st_c = lru_branch(u, p, zero_state, N_CTX_B, N_CTX, 0)
        lru_l, _ = lru_branch(u, p, state_lru[:, l], N_LAT_B, N_LAT, lat_blk0)
        conv = (conf_branch(u, p, N_CTX_B, N_CTX, 0), conf_branch(u, p, N_LAT_B, N_LAT, lat_blk0))
        pool = (pool_branch(u, p, N_CTX_B, N_CTX, 0), pool_branch(u, p, N_LAT_B, N_LAT, lat_blk0))
        att_c, k_l, v_l = ctx_attention(u, p)
        att_l = lat_attention(u, p, ck, cv, rope_cos, rope_sin)
        h = merge(h, xn, (lru_c, lru_l), conv, (att_c, att_l), pool, mod_tiles, p)
        h = moe_layer(h, mod_tiles, p)
        new_k.append(k_l.reshape(N_CTX_B, N_CTX, N_KV, HEAD_DIM))
        new_v.append(v_l.reshape(N_CTX_B, N_CTX, N_KV, HEAD_DIM))
        new_s.append(st_c)
    y_prompt = h[:T_CTX].reshape(N_CTX_B, N_CTX, D)
    y_sample = h[T_CTX:].reshape(N_LAT_B, N_LAT, D)
    return (y_prompt, y_sample, jnp.stack(new_k, axis=1), jnp.stack(new_v, axis=1), jnp.stack(new_s, axis=1))
```

```python
import functools

import jax
import jax.numpy as jnp
from jax import lax
from jax.experimental import pallas as pl
from jax.experimental.pallas import tpu as pltpu

f32 = jnp.float32
bf16 = jnp.bfloat16
i32 = jnp.int32

D = 1024
N_CTX_B, N_CTX = 16, 256
N_LAT_B, N_LAT = 8, 1024
T_CTX = N_CTX_B * N_CTX
T_LAT = N_LAT_B * N_LAT
T = T_CTX + T_LAT
DEPTH = 2
PAST = 512
GRID_W = 64
IN_W = 6656
COL_UX, COL_UY, COL_UC, COL_UQ, COL_UK, COL_UV, COL_UP = 0, 1024, 2048, 4096, 5120, 5376, 5632
LRU_BW = 64
LRU_CONV = 4
LRU_C = 8.0
CONV_K = 31
N_HEADS, N_KV, HEAD_DIM = 16, 4, 64
POOL_SIZES = (2, 4, 8, 16)
N_EXPERTS, TOP_K = 32, 4
SWIGLU_LIMIT, SWIGLU_ALPHA = 7.0, 1.702
EPS = 1e-6
NEG_INF = -1e30
ROPE_BASE = 10000.0
LOG2E = 1.4426950408889634

LANES = 128
SUBLANES = 8
VMEM_LIMIT = 56 * 1024 * 1024

CW = 256
RC = 128
TQ = 128
TM = 256
MOE_BM = 256
N_ASSIGN = T * TOP_K
MOE_NBLK = N_ASSIGN // MOE_BM + N_EXPERTS
MOE_CAP = MOE_NBLK * MOE_BM


def _sds(shape, dt=f32):
    return jax.ShapeDtypeStruct(shape, dt)


def _cparams(sem, vmem=VMEM_LIMIT):
    return pltpu.CompilerParams(dimension_semantics=sem, vmem_limit_bytes=vmem)


def _sigmoid(x):
    return 0.5 * jnp.tanh(0.5 * x) + 0.5


def _log1p(z):
    u = 1.0 + z
    d = u - 1.0
    return jnp.where(d == 0.0, z, jnp.log(u) * (z / jnp.where(d == 0.0, 1.0, d)))


def _cast_kernel(x_ref, o_ref):
    o_ref[...] = x_ref[...].astype(o_ref.dtype)


def cast_bf16(w, l):
    _, r, c = w.shape
    tr = 256
    return pl.pallas_call(
        _cast_kernel, out_shape=_sds((r, c), bf16), grid=(r // tr,),
        in_specs=[pl.BlockSpec((None, tr, c), lambda i: (l, i, 0))],
        out_specs=pl.BlockSpec((tr, c), lambda i: (i, 0)),
        compiler_params=_cparams(("parallel",)), name="cast_bf16")(w)


def _mod_kernel(c_ref, w_ref, b_ref, o_ref):
    x = c_ref[...]
    s = x * _sigmoid(x)
    o_ref[0] = jnp.dot(s.astype(bf16), w_ref[0].astype(bf16), preferred_element_type=f32) + b_ref[0]


def modulation(cond, w_mod, b_mod):
    r = cond.shape[0]
    tn = 1536
    return pl.pallas_call(
        _mod_kernel, out_shape=_sds((DEPTH, r, 6 * D)), grid=(DEPTH, 6 * D // tn),
        in_specs=[pl.BlockSpec((r, D), lambda l, j: (0, 0)),
                  pl.BlockSpec((1, D, tn), lambda l, j: (l, 0, j)),
                  pl.BlockSpec((1, 1, tn), lambda l, j: (l, 0, j))],
        out_specs=pl.BlockSpec((1, r, tn), lambda l, j: (l, 0, j)),
        compiler_params=_cparams(("parallel", "parallel")), name="modulation")(
            cond, w_mod, b_mod.reshape(DEPTH, 1, 6 * D))


def _rms_mod(x, g, shift, scale):
    ms = jnp.mean(x * x, axis=-1, keepdims=True)
    return (x * lax.rsqrt(ms + EPS) * g) * (1.0 + scale) + shift


def _norm1_kernel(h_ref, g_ref, mod_ref, o_ref):
    m = mod_ref[0]
    o_ref[...] = _rms_mod(h_ref[...], g_ref[...], m[0:1, :], m[1:2, :]).astype(bf16)


def norm1(h, g, mod_tiles):
    nt = h.shape[0] // TM
    return pl.pallas_call(
        _norm1_kernel, out_shape=_sds(h.shape, bf16), grid=(nt,),
        in_specs=[pl.BlockSpec((TM, D), lambda i: (i, 0)),
                  pl.BlockSpec((1, D), lambda i: (0, 0)),
                  pl.BlockSpec((1, 6, D), lambda i: (i, 0, 0))],
        out_specs=pl.BlockSpec((TM, D), lambda i: (i, 0)),
        compiler_params=_cparams(("parallel",)), name="norm1")(h, g.reshape(1, D), mod_tiles)


def _matmul_kernel(x_ref, w_ref, o_ref):
    o_ref[...] = jnp.dot(x_ref[...], w_ref[...], preferred_element_type=f32)


def in_proj(xn, w):
    t = xn.shape[0]
    tm, tn = 1024, 1664
    return pl.pallas_call(
        _matmul_kernel, out_shape=_sds((t, IN_W)), grid=(t // tm, IN_W // tn),
        in_specs=[pl.BlockSpec((tm, D), lambda i, j: (i, 0)),
                  pl.BlockSpec((D, tn), lambda i, j: (0, j))],
        out_specs=pl.BlockSpec((tm, tn), lambda i, j: (i, j)),
        compiler_params=_cparams(("parallel", "parallel")), name="in_proj")(xn, w)


def _gelu_tanh(x):
    return 0.5 * x * (1.0 + jnp.tanh(0.7978845608028654 * (x + 0.044715 * (x * x * x))))


def _lru_kernel(ux_ref, uy_ref, cw_ref, cb_ref, wr_ref, wi_ref, br_ref, bi_ref, lam_ref, h0_ref,
                out_ref, st_ref, xp_ref, a_ref, b_ref, hs_ref, *, n):
    pad = SUBLANES
    xp_ref[pl.ds(0, pad), :] = jnp.zeros((pad, CW), f32)
    xp_ref[pl.ds(n + pad, pad), :] = jnp.zeros((pad, CW), f32)
    xp_ref[pl.ds(pad, n), :] = ux_ref[...]
    left = LRU_CONV // 2
    lam = lam_ref[...]
    neg_c_sp = -LRU_C * (jnp.maximum(-lam, 0.0) + _log1p(jnp.exp(-jnp.abs(lam))))

    def coef_body(i, c):
        r0 = pl.multiple_of(i * RC, RC)
        w = xp_ref[pl.ds(r0, RC + 2 * pad), :]
        xc = jnp.zeros((RC, CW), f32) + cb_ref[...]
        for k in range(LRU_CONV):
            off = pad - left + k
            xc = xc + cw_ref[k:k + 1, :] * w[off:off + RC, :]
        xcb = xc.astype(bf16)
        for d in range(2):
            r = _sigmoid(jnp.dot(xcb, wr_ref[d, 0], preferred_element_type=f32) + br_ref[d:d + 1, :])
            g = _sigmoid(jnp.dot(xcb, wi_ref[d, 0], preferred_element_type=f32) + bi_ref[d:d + 1, :])
            log_a = r * neg_c_sp[d:d + 1, :]
            a = jnp.exp(log_a)
            th = jnp.tanh(log_a)
            one_m_a2 = (-2.0 * th) / (1.0 - th)
            a_ref[d, pl.ds(r0, RC), :] = a
            b_ref[d, pl.ds(r0, RC), :] = jnp.sqrt(one_m_a2) * (g * xc)
        return c

    lax.fori_loop(0, n // RC, coef_body, 0)

    row = lax.broadcasted_iota(i32, (SUBLANES, CW), 0)
    nchunk = n // SUBLANES

    def scan_body(j, carry):
        hf, hb = carry
        rf = pl.multiple_of(j * SUBLANES, SUBLANES)
        a = a_ref[0, pl.ds(rf, SUBLANES), :]
        b = b_ref[0, pl.ds(rf, SUBLANES), :]
        for sh in (1, 2, 4):
            a_s = jnp.where(row >= sh, pltpu.roll(a, sh, axis=0), 1.0)
            b_s = jnp.where(row >= sh, pltpu.roll(b, sh, axis=0), 0.0)
            b = a * b_s + b
            a = a * a_s
        h = a * hf + b
        hs_ref[0, pl.ds(rf, SUBLANES), :] = h
        hf = h[SUBLANES - 1:SUBLANES, :]
        rb = pl.multiple_of((nchunk - 1 - j) * SUBLANES, SUBLANES)
        a = a_ref[1, pl.ds(rb, SUBLANES), :]
        b = b_ref[1, pl.ds(rb, SUBLANES), :]
        for sh in (1, 2, 4):
            keep = row < SUBLANES - sh
            a_s = jnp.where(keep, pltpu.roll(a, SUBLANES - sh, axis=0), 1.0)
            b_s = jnp.where(keep, pltpu.roll(b, SUBLANES - sh, axis=0), 0.0)
            b = a * b_s + b
            a = a * a_s
        h = a * hb + b
        hs_ref[1, pl.ds(rb, SUBLANES), :] = h
        hb = h[0:1, :]
        return hf, hb

    h0 = h0_ref[0]
    hf, hb = lax.fori_loop(0, nchunk, scan_body, (h0[0:1, :], h0[1:2, :]), unroll=4)
    st_ref[0, 0:1, :] = hf
    st_ref[0, 1:2, :] = hb

    def out_body(i, c):
        r0 = pl.multiple_of(i * RC, RC)
        hsum = hs_ref[0, pl.ds(r0, RC), :] + hs_ref[1, pl.ds(r0, RC), :]
        out_ref[pl.ds(r0, RC), :] = hsum * _gelu_tanh(uy_ref[pl.ds(r0, RC), :])
        return c

    lax.fori_loop(0, n // RC, out_body, 0)


def lru_branch(u, p, h0, nb, n, row_blk0):
    nct = D // CW
    cu = COL_UY // CW
    kern = functools.partial(_lru_kernel, n=n)
    vec = lambda a: pl.BlockSpec((a, CW), lambda b, c: (0, c))
    call = pl.pallas_call(
        kern, out_shape=(_sds((nb * n, D)), _sds((nb, 2, D))), grid=(nb, nct),
        in_specs=[pl.BlockSpec((n, CW), lambda b, c: (row_blk0 + b, c)),
                  pl.BlockSpec((n, CW), lambda b, c: (row_blk0 + b, cu + c)),
                  vec(LRU_CONV), vec(1),
                  pl.BlockSpec((2, 1, CW, CW), lambda b, c: (0, c, 0, 0)),
                  pl.BlockSpec((2, 1, CW, CW), lambda b, c: (0, c, 0, 0)),
                  vec(2), vec(2), vec(2),
                  pl.BlockSpec((1, 2, CW), lambda b, c: (b, 0, c))],
        out_specs=(pl.BlockSpec((n, CW), lambda b, c: (b, c)),
                   pl.BlockSpec((1, 2, CW), lambda b, c: (b, 0, c))),
        scratch_shapes=[pltpu.VMEM((n + 2 * SUBLANES, CW), f32),
                        pltpu.VMEM((2, n, CW), f32), pltpu.VMEM((2, n, CW), f32), pltpu.VMEM((2, n, CW), f32)],
        compiler_params=_cparams(("parallel", "parallel")), name="lru_branch")
    return call(u, u, p["lru_conv_w"], p["lru_conv_b"].reshape(1, D), p["lru_wr_bd"], p["lru_wi_bd"],
                p["lru_b_r"], p["lru_b_i"], p["lru_lambda"], h0)


def _conf_kernel(ua_ref, ug_ref, w_ref, b_ref, o_ref, gp_ref, *, n):
    pad = 2 * SUBLANES
    left = CONV_K // 2
    gp_ref[pl.ds(0, pad), :] = jnp.zeros((pad, CW), f32)
    gp_ref[pl.ds(n + pad, pad), :] = jnp.zeros((pad, CW), f32)
    gp_ref[pl.ds(pad, n), :] = ua_ref[...] * _sigmoid(ug_ref[...])

    wn = RC + 2 * pad

    def body(i, c):
        r0 = pl.multiple_of(i * RC, RC)
        w = gp_ref[pl.ds(r0, wn), :]
        rolled = [w] + [pltpu.roll(w, wn - m, axis=0) for m in range(1, SUBLANES)]
        acc = jnp.zeros((RC, CW), f32) + b_ref[...]
        for k in range(CONV_K):
            off = pad - left + k
            q, m = off // SUBLANES, off % SUBLANES
            acc = acc + w_ref[k:k + 1, :] * rolled[m][q * SUBLANES:q * SUBLANES + RC, :]
        o_ref[pl.ds(r0, RC), :] = acc
        return c

    lax.fori_loop(0, n // RC, body, 0)


def conf_branch(u, p, nb, n, row_blk0):
    nct = D // CW
    ca, cg = COL_UC // CW, (COL_UC + D) // CW
    kern = functools.partial(_conf_kernel, n=n)
    call = pl.pallas_call(
        kern, out_shape=_sds((nb * n, D)), grid=(nb, nct),
        in_specs=[pl.BlockSpec((n, CW), lambda b, c: (row_blk0 + b, ca + c)),
                  pl.BlockSpec((n, CW), lambda b, c: (row_blk0 + b, cg + c)),
                  pl.BlockSpec((CONV_K, CW), lambda b, c: (0, c)),
                  pl.BlockSpec((1, CW), lambda b, c: (0, c))],
        out_specs=pl.BlockSpec((n, CW), lambda b, c: (b, c)),
        scratch_shapes=[pltpu.VMEM((n + 4 * SUBLANES, CW), f32)],
        compiler_params=_cparams(("parallel", "parallel")), name="conf_branch")
    return call(u, u, p["conv_dw_w"], p["conv_dw_b"].reshape(1, D))


def _pool_kernel(up_ref, w_ref, s_ref, o_ref, xp_ref, *, n):
    pad = SUBLANES
    gi = pl.program_id(1)
    half = jnp.left_shift(1, gi)
    xp_ref[pl.ds(0, pad), :] = jnp.zeros((pad, CW), f32)
    xp_ref[pl.ds(n + pad, pad), :] = jnp.zeros((pad, CW), f32)
    xp_ref[pl.ds(pad, n), :] = up_ref[...]
    wn = RC + 2 * pad

    def body(i, c):
        r0 = pl.multiple_of(i * RC, RC)
        w = xp_ref[pl.ds(r0, wn), :]
        s2 = w + pltpu.roll(w, 1, axis=0)
        s4 = pltpu.roll(s2, 1, axis=0) + pltpu.roll(s2, wn - 1, axis=0)
        s8 = pltpu.roll(s4, 2, axis=0) + pltpu.roll(s4, wn - 2, axis=0)
        s16 = pltpu.roll(s8, 4, axis=0) + pltpu.roll(s8, wn - 4, axis=0)
        s = jnp.where(gi == 0, s2, jnp.where(gi == 1, s4, jnp.where(gi == 2, s8, s16)))[pad:pad + RC, :]
        t = r0 + lax.broadcasted_iota(i32, (RC, CW), 0)
        cnt = (jnp.minimum(t + half, n) - jnp.maximum(t - half, 0)).astype(f32)
        pooled = s / cnt - w[pad:pad + RC, :]
        o_ref[pl.ds(r0, RC), :] = jnp.dot(pooled.astype(bf16), w_ref[0], preferred_element_type=f32) * s_ref[...]
        return c

    lax.fori_loop(0, n // RC, body, 0)


def pool_branch(u, p, nb, n, row_blk0):
    assert POOL_SIZES == (2, 4, 8, 16) and D // len(POOL_SIZES) == CW
    cp = COL_UP // CW
    kern = functools.partial(_pool_kernel, n=n)
    call = pl.pallas_call(
        kern, out_shape=_sds((nb * n, D)), grid=(nb, len(POOL_SIZES)),
        in_specs=[pl.BlockSpec((n, CW), lambda b, c: (row_blk0 + b, cp + c)),
                  pl.BlockSpec((1, CW, CW), lambda b, c: (c, 0, 0)),
                  pl.BlockSpec((1, CW), lambda b, c: (0, c))],
        out_specs=pl.BlockSpec((n, CW), lambda b, c: (b, c)),
        scratch_shapes=[pltpu.VMEM((n + 2 * SUBLANES, CW), f32)],
        compiler_params=_cparams(("parallel", "parallel")), name="pool_branch")
    return call(u, p["pool_w_bf"], p["pool_scale"].reshape(1, D))


def _head_norm(blk, g128, lane_lo):
    sq = blk * blk
    s_lo = jnp.sum(jnp.where(lane_lo, sq, 0.0), axis=-1, keepdims=True)
    s_hi = jnp.sum(jnp.where(lane_lo, 0.0, sq), axis=-1, keepdims=True)
    r = jnp.where(lane_lo, lax.rsqrt(s_lo * (1.0 / HEAD_DIM) + EPS), lax.rsqrt(s_hi * (1.0 / HEAD_DIM) + EPS))
    return blk * r * g128


def _rope(blk, cos, sin_signed, lane):
    partner = jnp.where((lane % 32) < 16, pltpu.roll(blk, LANES - 16, axis=1), pltpu.roll(blk, 16, axis=1))
    return blk * cos + partner * sin_signed


def _prep_kv(k_ref, v_ref, ks_ref, vs_ref, kg, lane_lo, rope=None, newk_ref=None, newv_ref=None):
    nk = k_ref.shape[0]
    lane = lax.broadcasted_iota(i32, (nk, LANES), 1)
    for tj in range(N_KV * HEAD_DIM // LANES):
        kt = _head_norm(k_ref[:, tj * LANES:(tj + 1) * LANES], kg, lane_lo(nk))
        if rope is not None:
            kt = _rope(kt, rope[0][...], rope[1][...], lane)
        vt = v_ref[:, tj * LANES:(tj + 1) * LANES]
        if newk_ref is not None:
            newk_ref[0, :, tj * LANES:(tj + 1) * LANES] = kt
            newv_ref[0, :, tj * LANES:(tj + 1) * LANES] = vt
        _store_low_half(kt, vt, ks_ref, vs_ref, tj, lane_lo(nk))


def _store_low_half(kt, vt, ks_ref, vs_ref, tj, lane_lo):
    ks_ref[2 * tj] = jnp.where(lane_lo, kt, 0.0).astype(bf16)
    vs_ref[2 * tj] = jnp.where(lane_lo, vt, 0.0).astype(bf16)
    ks_ref[2 * tj + 1] = jnp.where(lane_lo, pltpu.roll(kt, HEAD_DIM, axis=1), 0.0).astype(bf16)
    vs_ref[2 * tj + 1] = jnp.where(lane_lo, pltpu.roll(vt, HEAD_DIM, axis=1), 0.0).astype(bf16)


def _prep_cache(ck_ref, cv_ref, kc_ref, vc_ref, lane_lo):
    nk = ck_ref.shape[2]
    for tj in range(N_KV * HEAD_DIM // LANES):
        kt = ck_ref[0, 0, :, tj * LANES:(tj + 1) * LANES]
        vt = cv_ref[0, 0, :, tj * LANES:(tj + 1) * LANES]
        _store_low_half(kt, vt, kc_ref, vc_ref, tj, lane_lo(nk))


def _qk(qm, k):
    return lax.dot_general(qm, k, (((1,), (1,)), ((), ())), preferred_element_type=f32)


def _attend(sink_ref, q_tiles, segments, att_ref):
    per_kv = N_HEADS // N_KV
    assert per_kv == 4
    rows = per_kv * TQ
    row = lax.broadcasted_iota(i32, (rows, 1), 0)
    masks = [None if mk is None else jnp.concatenate([mk] * per_kv, axis=0) for _, _, mk in segments]
    for g in range(N_KV):
        t0, t1 = q_tiles[2 * g], q_tiles[2 * g + 1]
        qm = jnp.concatenate([t0, t1, pltpu.roll(t0, HEAD_DIM, axis=1), pltpu.roll(t1, HEAD_DIM, axis=1)],
                             axis=0).astype(bf16)
        hd = [per_kv * g, per_kv * g + 2, per_kv * g + 1, per_kv * g + 3]
        sink = jnp.where(row < TQ, sink_ref[hd[0]],
                         jnp.where(row < 2 * TQ, sink_ref[hd[1]],
                                   jnp.where(row < 3 * TQ, sink_ref[hd[2]], sink_ref[hd[3]]))) * LOG2E
        scores = []
        m_el = None
        for (kget, _, _), mask in zip(segments, masks):
            s = _qk(qm, kget(g))
            if mask is not None:
                s = jnp.where(mask, s, NEG_INF)
            for c in range(s.shape[1] // LANES):
                t = s[:, c * LANES:(c + 1) * LANES]
                m_el = t if m_el is None else jnp.maximum(m_el, t)
            scores.append(s)
        m = jnp.maximum(jnp.max(m_el, axis=-1, keepdims=True), sink)
        d_el = jnp.zeros((rows, LANES), f32)
        o = jnp.zeros((rows, LANES), f32)
        for s, (_, vget, _) in zip(scores, segments):
            pr = jnp.exp2(s - m)
            for c in range(s.shape[1] // LANES):
                d_el = d_el + pr[:, c * LANES:(c + 1) * LANES]
            o = o + jnp.dot(pr.astype(bf16), vget(g), preferred_element_type=f32)
        den = jnp.exp2(sink - m) + jnp.sum(d_el, axis=-1, keepdims=True)
        o = o / den
        att_ref[:, 2 * g * LANES:(2 * g + 1) * LANES] = o[:TQ] + pltpu.roll(o[2 * TQ:3 * TQ], HEAD_DIM, axis=1)
        att_ref[:, (2 * g + 1) * LANES:(2 * g + 2) * LANES] = o[TQ:2 * TQ] + pltpu.roll(o[3 * TQ:], HEAD_DIM, axis=1)


def _q_tiles(q_ref, qg, rope=None):
    lane = lax.broadcasted_iota(i32, (TQ, LANES), 1)
    lane_lo = lane < HEAD_DIM
    tiles = []
    for j in range(N_HEADS // 2):
        qt = _head_norm(q_ref[:, j * LANES:(j + 1) * LANES], qg, lane_lo)
        if rope is not None:
            qt = _rope(qt, rope[0], rope[1], lane)
        tiles.append(qt * (HEAD_DIM ** -0.5 * LOG2E))
    return tiles


def _lane_lo_fn(nrows):
    return lax.broadcasted_iota(i32, (nrows, LANES), 1) < HEAD_DIM


def _ctx_attn_kernel(sink_ref, q_ref, k_ref, v_ref, qg_ref, kg_ref, att_ref, newk_ref, newv_ref, ks_ref, vs_ref):
    @pl.when(pl.program_id(1) == 0)
    def _():
        _prep_kv(k_ref, v_ref, ks_ref, vs_ref, kg_ref[...], _lane_lo_fn, None, newk_ref, newv_ref)

    tiles = _q_tiles(q_ref, qg_ref[...])
    seg = [(lambda g: ks_ref[g], lambda g: vs_ref[g], None)]
    _attend(sink_ref, tiles, seg, att_ref)


def ctx_attention(u, p):
    nq = N_CTX // TQ
    kvw = N_KV * HEAD_DIM
    return pl.pallas_call(
        _ctx_attn_kernel,
        out_shape=(_sds((T_CTX, D)), _sds((N_CTX_B, N_CTX, kvw)), _sds((N_CTX_B, N_CTX, kvw))),
        grid=(N_CTX_B, nq),
        in_specs=[pl.BlockSpec(memory_space=pltpu.SMEM),
                  pl.BlockSpec((TQ, D), lambda b, i: (b * nq + i, COL_UQ // D)),
                  pl.BlockSpec((N_CTX, kvw), lambda b, i: (b, COL_UK // kvw)),
                  pl.BlockSpec((N_CTX, kvw), lambda b, i: (b, COL_UV // kvw)),
                  pl.BlockSpec((1, LANES), lambda b, i: (0, 0)),
                  pl.BlockSpec((1, LANES), lambda b, i: (0, 0))],
        out_specs=(pl.BlockSpec((TQ, D), lambda b, i: (b * nq + i, 0)),
                   pl.BlockSpec((1, N_CTX, kvw), lambda b, i: (b, 0, 0)),
                   pl.BlockSpec((1, N_CTX, kvw), lambda b, i: (b, 0, 0))),
        scratch_shapes=[pltpu.VMEM((N_KV, N_CTX, LANES), bf16), pltpu.VMEM((N_KV, N_CTX, LANES), bf16)],
        compiler_params=_cparams(("parallel", "arbitrary")), name="ctx_attention")(
            p["attn_sink"], u, u, u, p["q_norm_g2"], p["k_norm_g2"])


def _lat_attn_kernel(sink_ref, q_ref, k_ref, v_ref, ck_ref, cv_ref, cosq_ref, sinq_ref, cosk_ref, sink_tab_ref,
                     qg_ref, kg_ref, att_ref, ks_ref, vs_ref, kc_ref, vc_ref):
    i = pl.program_id(1)
    nq = N_LAT // TQ

    @pl.when(i == 0)
    def _():
        _prep_kv(k_ref, v_ref, ks_ref, vs_ref, kg_ref[...], _lane_lo_fn, (cosk_ref, sink_tab_ref))
        _prep_cache(ck_ref, cv_ref, kc_ref, vc_ref, _lane_lo_fn)

    tiles = _q_tiles(q_ref, qg_ref[...], (cosq_ref[...], sinq_ref[...]))
    r = lax.broadcasted_iota(i32, (TQ, TQ), 0)
    c = lax.broadcasted_iota(i32, (TQ, TQ), 1)
    prev0 = pl.multiple_of(jnp.maximum(i - 1, 0) * TQ, TQ)
    cur0 = pl.multiple_of(i * TQ, TQ)
    next0 = pl.multiple_of(jnp.minimum(i + 1, nq - 1) * TQ, TQ)
    mask_prev = jnp.logical_and(c >= r, i > 0)
    mask_next = jnp.logical_and(c <= r, i < nq - 1)

    def seg(r0, mask):
        return (lambda g: ks_ref[g, pl.ds(r0, TQ), :], lambda g: vs_ref[g, pl.ds(r0, TQ), :], mask)

    segments = [seg(prev0, mask_prev), seg(cur0, None), seg(next0, mask_next),
                (lambda g: kc_ref[g], lambda g: vc_ref[g], None)]
    _attend(sink_ref, tiles, segments, att_ref)


def lat_attention(u, p, cache_k, cache_v, rope_cos, rope_sin):
    nq = N_LAT // TQ
    kvw = N_KV * HEAD_DIM
    l = p["layer"]
    qblk0 = T_CTX // TQ
    kblk0 = T_CTX // N_LAT
    call = pl.pallas_call(
        _lat_attn_kernel, out_shape=_sds((T_LAT, D)), grid=(N_LAT_B, nq),
        in_specs=[pl.BlockSpec(memory_space=pltpu.SMEM),
                  pl.BlockSpec((TQ, D), lambda b, i: (qblk0 + b * nq + i, COL_UQ // D)),
                  pl.BlockSpec((N_LAT, kvw), lambda b, i: (kblk0 + b, COL_UK // kvw)),
                  pl.BlockSpec((N_LAT, kvw), lambda b, i: (kblk0 + b, COL_UV // kvw)),
                  pl.BlockSpec((1, 1, PAST, kvw), lambda b, i: (b, l, 0, 0)),
                  pl.BlockSpec((1, 1, PAST, kvw), lambda b, i: (b, l, 0, 0)),
                  pl.BlockSpec((TQ, LANES), lambda b, i: (i, 0)),
                  pl.BlockSpec((TQ, LANES), lambda b, i: (i, 0)),
                  pl.BlockSpec((N_LAT, LANES), lambda b, i: (0, 0)),
                  pl.BlockSpec((N_LAT, LANES), lambda b, i: (0, 0)),
                  pl.BlockSpec((1, LANES), lambda b, i: (0, 0)),
                  pl.BlockSpec((1, LANES), lambda b, i: (0, 0))],
        out_specs=pl.BlockSpec((TQ, D), lambda b, i: (b * nq + i, 0)),
        scratch_shapes=[pltpu.VMEM((N_KV, N_LAT, LANES), bf16), pltpu.VMEM((N_KV, N_LAT, LANES), bf16),
                        pltpu.VMEM((N_KV, PAST, LANES), bf16), pltpu.VMEM((N_KV, PAST, LANES), bf16)],
        compiler_params=_cparams(("parallel", "arbitrary")), name="lat_attention")
    return call(p["attn_sink"], u, u, u, cache_k, cache_v, rope_cos, rope_sin, rope_cos, rope_sin,
                p["q_norm_g2"], p["k_norm_g2"])


def rope_tables():
    pos = jnp.arange(N_LAT)
    rows = (pos // GRID_W).astype(f32)
    cols = (pos % GRID_W).astype(f32)
    half = HEAD_DIM // 2
    freqs = ROPE_BASE ** (-jnp.arange(0, half, 2, dtype=f32) / half)
    lane = jnp.arange(LANES)
    within = lane % half
    fidx = within % (half // 2)
    use_cols = (lane % HEAD_DIM) >= half
    ang = jnp.where(use_cols[None, :], cols[:, None], rows[:, None]) * freqs[fidx][None, :]
    sign = jnp.where(within < half // 2, -1.0, 1.0)
    return jnp.cos(ang), jnp.sin(ang) * sign[None, :]


def _merge_kernel(h_ref, xn_ref, lru_c, lru_l, conv_c, conv_l, att_c, att_l, pool_c, pool_l, mod_ref, lng_ref, lnb_ref,
                  wg_ref, bg_ref, wb_ref, wo_ref, o_ref):
    is_ctx = pl.program_id(0) < T_CTX // TM
    pick = lambda c_ref, l_ref: jnp.where(is_ctx, c_ref[...], l_ref[...])
    xn = xn_ref[...]
    hc = pick(conv_c, conv_l)
    mu = jnp.mean(hc, axis=-1, keepdims=True)
    xc = hc - mu
    var = jnp.mean(xc * xc, axis=-1, keepdims=True)
    y = xc * lax.rsqrt(var + EPS) * lng_ref[...] + lnb_ref[...]
    conv = y * _sigmoid(y)
    merged = jnp.zeros((TM, D), f32)
    for j, br in enumerate((pick(lru_c, lru_l), conv, pick(att_c, att_l), pick(pool_c, pool_l))):
        gate = _sigmoid(jnp.dot(xn, wg_ref[:, j * D:(j + 1) * D], preferred_element_type=f32)
                        + bg_ref[:, j * D:(j + 1) * D])
        proj = jnp.dot(br.astype(bf16), wb_ref[j], preferred_element_type=f32)
        merged = merged + gate * proj
    out = jnp.dot(merged.astype(bf16), wo_ref[...], preferred_element_type=f32)
    o_ref[...] = h_ref[...] + mod_ref[0][2:3, :] * out


def merge(h, xn, lru, conv, att, pool, mod_tiles, p):
    nt = h.shape[0] // TM
    n_ctx_tiles = T_CTX // TM
    tok = pl.BlockSpec((TM, D), lambda i: (i, 0))
    tok_c = pl.BlockSpec((TM, D), lambda i: (jnp.minimum(i, n_ctx_tiles - 1), 0))
    tok_l = pl.BlockSpec((TM, D), lambda i: (jnp.maximum(i - n_ctx_tiles, 0), 0))
    const = lambda shape: pl.BlockSpec(shape, lambda i: (0,) * len(shape), pipeline_mode=pl.Buffered(1))
    return pl.pallas_call(
        _merge_kernel, out_shape=_sds(h.shape), grid=(nt,),
        in_specs=[tok, tok, tok_c, tok_l, tok_c, tok_l, tok_c, tok_l, tok_c, tok_l,
                  pl.BlockSpec((1, 6, D), lambda i: (i, 0, 0)),
                  const((1, D)), const((1, D)),
                  const((D, 4 * D)), const((1, 4 * D)), const((4, D, D)), const((D, D))],
        out_specs=tok,
        compiler_params=_cparams(("parallel",)), name="merge")(
            h, xn, *lru, *conv, *att, *pool, mod_tiles, p["conv_ln_g"].reshape(1, D), p["conv_ln_b"].reshape(1, D),
            p["w_gate_bf"], p["b_gate"].reshape(1, 4 * D), p["w_branch_bf"], p["w_out_bf"])


def _router_kernel(h_ref, g_ref, mod_ref, rw_ref, rb_ref, xn_ref, topi_ref, topw_ref, rank_ref, cnt_ref, carry_ref):
    step = pl.program_id(0)

    @pl.when(step == 0)
    def _():
        carry_ref[...] = jnp.zeros_like(carry_ref)

    m = mod_ref[0]
    xn = _rms_mod(h_ref[...], g_ref[...], m[3:4, :], m[4:5, :])
    xn_ref[...] = xn.reshape(xn_ref.shape)
    logits = jnp.dot(xn, rw_ref[...], preferred_element_type=f32, precision=lax.Precision.HIGHEST) + rb_ref[...]
    lane = lax.broadcasted_iota(i32, (TM, LANES), 1).astype(f32)
    lg = jnp.where(lane < N_EXPERTS, logits, NEG_INF)
    vals, idxs = [], []
    for _ in range(TOP_K):
        mx = jnp.max(lg, axis=-1, keepdims=True)
        idx = jnp.min(jnp.where(lg == mx, lane, float(LANES)), axis=-1, keepdims=True)
        vals.append(mx)
        idxs.append(idx)
        lg = jnp.where(lane == idx, -3e38, lg)
    exps = [jnp.exp(v - vals[0]) for v in vals]
    den = exps[0] + exps[1] + exps[2] + exps[3]
    cnt = jnp.zeros((TM, LANES), f32)
    for idx in idxs:
        cnt = cnt + jnp.where(lane == idx, 1.0, 0.0)
    rr = lax.broadcasted_iota(i32, (TM, TM), 0)
    cc = lax.broadcasted_iota(i32, (TM, TM), 1)
    tri = jnp.where(rr > cc, 1.0, 0.0).astype(bf16)
    before = jnp.dot(tri, cnt.astype(bf16), preferred_element_type=f32) + carry_ref[0:1, :]
    topi = jnp.zeros((TM, LANES), f32)
    topw = jnp.zeros((TM, LANES), f32)
    rank = jnp.zeros((TM, LANES), f32)
    for k in range(TOP_K):
        rk = jnp.sum(jnp.where(lane == idxs[k], before, 0.0), axis=-1, keepdims=True)
        topi = jnp.where(lane == k, idxs[k], topi)
        topw = jnp.where(lane == k, exps[k] / den, topw)
        rank = jnp.where(lane == k, rk, rank)
    topi_ref[...] = topi.astype(i32)
    topw_ref[...] = topw
    rank_ref[...] = rank.astype(i32)
    total = carry_ref[0:1, :] + jnp.sum(cnt, axis=0, keepdims=True)
    carry_ref[...] = jnp.broadcast_to(total, carry_ref.shape)
    cnt_ref[...] = jnp.broadcast_to(total, cnt_ref.shape).astype(i32)


def router(h, mod_tiles, p):
    nt = h.shape[0] // TM
    tok = lambda w, dt=f32: pl.BlockSpec((TM, w), lambda i: (i, 0))
    t = h.shape[0]
    return pl.pallas_call(
        _router_kernel,
        out_shape=(_sds((t, D // LANES, LANES)), _sds((t, LANES), i32), _sds((t, LANES)), _sds((t, LANES), i32),
                   _sds((SUBLANES, LANES), i32)),
        grid=(nt,),
        in_specs=[tok(D), pl.BlockSpec((1, D), lambda i: (0, 0)), pl.BlockSpec((1, 6, D), lambda i: (i, 0, 0)),
                  pl.BlockSpec((D, LANES), lambda i: (0, 0)), pl.BlockSpec((1, LANES), lambda i: (0, 0))],
        out_specs=(pl.BlockSpec((TM, D // LANES, LANES), lambda i: (i, 0, 0)), tok(LANES), tok(LANES), tok(LANES),
                   pl.BlockSpec((SUBLANES, LANES), lambda i: (0, 0))),
        scratch_shapes=[pltpu.VMEM((SUBLANES, LANES), f32)],
        compiler_params=_cparams(("arbitrary",)), name="router")(
            h, p["norm2_g"].reshape(1, D), mod_tiles, p["router_w_pad"], p["router_b_pad"])


def _dest_kernel(topi_ref, rank_ref, pstart_ref, o_ref):
    lane = lax.broadcasted_iota(i32, (TM, LANES), 1).astype(f32)
    topi = topi_ref[...].astype(f32)
    pstart = pstart_ref[...].astype(f32)
    dest = rank_ref[...].astype(f32)
    for k in range(TOP_K):
        e = jnp.sum(jnp.where(lane == k, topi, 0.0), axis=-1, keepdims=True)
        ps = jnp.sum(jnp.where(lane == e, pstart, 0.0), axis=-1, keepdims=True)
        dest = dest + jnp.where(lane == k, ps, 0.0)
    o_ref[...] = dest.astype(i32)


def dest_rows(topi, rank, pstart):
    t = topi.shape[0]
    tok = pl.BlockSpec((TM, LANES), lambda i: (i, 0))
    return pl.pallas_call(
        _dest_kernel, out_shape=_sds((t, LANES), i32), grid=(t // TM,),
        in_specs=[tok, tok, pl.BlockSpec((1, LANES), lambda i: (0, 0))], out_specs=tok,
        compiler_params=_cparams(("parallel",)), name="dest_rows")(topi, rank, pstart)


ROW3 = (D // LANES, LANES)
assert ROW3[0] == SUBLANES


def _row_copy(src_ref, s, dst_ref, d, sem):
    return pltpu.make_async_copy(src_ref.at[pl.ds(s, 1)], dst_ref.at[pl.ds(d, 1)], sem)


def _dispatch_kernel(zflag_ref, dest_ref, x_ref, xb_ref, zero_ref, sem):
    @pl.when(pl.program_id(0) == 0)
    def _():
        zero_ref[...] = jnp.zeros_like(zero_ref)

        def zcopy(j):
            r0 = pl.multiple_of(j * MOE_BM, MOE_BM)
            return pltpu.make_async_copy(zero_ref, xb_ref.at[pl.ds(r0, MOE_BM)], sem)

        def zstart(j, c):
            @pl.when(zflag_ref[j] != 0)
            def _():
                zcopy(j).start()
            return c

        def zwait(j, c):
            @pl.when(zflag_ref[j] != 0)
            def _():
                zcopy(j).wait()
            return c

        lax.fori_loop(0, MOE_NBLK, zstart, 0)
        lax.fori_loop(0, MOE_NBLK, zwait, 0)

    def start(t, c):
        for k in range(TOP_K):
            _row_copy(x_ref, t, xb_ref, dest_ref[t * TOP_K + k], sem).start(priority=k % 2)
        return c

    lax.fori_loop(0, TM, start, 0)

    def wait(t, c):
        for k in range(TOP_K):
            _row_copy(x_ref, 0, xb_ref, 0, sem).wait()
        return c

    lax.fori_loop(0, TM, wait, 0)


def dispatch(zflag, dest_flat, xn3):
    t = xn3.shape[0]
    grid_spec = pltpu.PrefetchScalarGridSpec(
        num_scalar_prefetch=1, grid=(t // TM,),
        in_specs=[pl.BlockSpec((TM * TOP_K,), lambda i, zb: (i,), memory_space=pltpu.SMEM),
                  pl.BlockSpec((TM,) + ROW3, lambda i, zb: (i, 0, 0))],
        out_specs=pl.BlockSpec(memory_space=pl.ANY),
        scratch_shapes=[pltpu.VMEM((MOE_BM,) + ROW3, f32), pltpu.SemaphoreType.DMA(())])
    return pl.pallas_call(
        _dispatch_kernel, out_shape=_sds((MOE_CAP,) + ROW3), grid_spec=grid_spec,
        compiler_params=_cparams(("arbitrary",)), name="moe_dispatch")(zflag, dest_flat, xn3)


def _expert_kernel(be_ref, nused_ref, xb_ref, wgu_ref, bgu_ref, wdn_ref, bdn_ref, yb_ref, wgu_bf, wdn_bf):
    i = pl.program_id(0)
    e = be_ref[i]
    prev = be_ref[jnp.maximum(i - 1, 0)]

    @pl.when(jnp.logical_or(i == 0, e != prev))
    def _():
        wgu_bf[...] = wgu_ref[0, 0].astype(bf16)
        wdn_bf[...] = wdn_ref[0, 0].astype(bf16)

    @pl.when(i < nused_ref[0])
    def _():
        x = xb_ref[...].reshape(MOE_BM, D).astype(bf16)
        h = jnp.dot(x, wgu_bf[...], preferred_element_type=f32) + bgu_ref[0, 0]
        gate = jnp.minimum(h[:, :D], SWIGLU_LIMIT)
        lin = jnp.clip(h[:, D:], -SWIGLU_LIMIT, SWIGLU_LIMIT)
        act = (lin + 1.0) * (gate * _sigmoid(SWIGLU_ALPHA * gate))
        y = jnp.dot(act.astype(bf16), wdn_bf[...], preferred_element_type=f32) + bdn_ref[0, 0]
        yb_ref[...] = y.reshape((MOE_BM,) + ROW3)

    @pl.when(i >= nused_ref[0])
    def _():
        yb_ref[...] = jnp.zeros_like(yb_ref)


def experts(block_e, nused, xb, p):
    l = p["layer"]
    grid_spec = pltpu.PrefetchScalarGridSpec(
        num_scalar_prefetch=2, grid=(MOE_NBLK,),
        in_specs=[pl.BlockSpec((MOE_BM,) + ROW3,
                               lambda i, be, nu: (jnp.maximum(jnp.minimum(i, nu[0] - 1), 0), 0, 0)),
                  pl.BlockSpec((1, 1, D, 2 * D), lambda i, be, nu: (l, be[i], 0, 0)),
                  pl.BlockSpec((1, 1, 1, 2 * D), lambda i, be, nu: (l, be[i], 0, 0)),
                  pl.BlockSpec((1, 1, D, D), lambda i, be, nu: (l, be[i], 0, 0)),
                  pl.BlockSpec((1, 1, 1, D), lambda i, be, nu: (l, be[i], 0, 0))],
        out_specs=pl.BlockSpec((MOE_BM,) + ROW3, lambda i, be, nu: (i, 0, 0)),
        scratch_shapes=[pltpu.VMEM((D, 2 * D), bf16), pltpu.VMEM((D, D), bf16)])
    return pl.pallas_call(
        _expert_kernel, out_shape=_sds((MOE_CAP,) + ROW3), grid_spec=grid_spec,
        compiler_params=_cparams(("arbitrary",)), name="moe_experts")(
            block_e, nused, xb, p["exp_w_gu"], p["exp_b_gu"].reshape(DEPTH, N_EXPERTS, 1, 2 * D),
            p["exp_w_down"], p["exp_b_down"].reshape(DEPTH, N_EXPERTS, 1, D))


def _combine_kernel(dest_ref, yb_ref, topw_ref, h_ref, mod_ref, o_ref, buf, sem):
    def start(t, c):
        for k in range(TOP_K):
            pltpu.make_async_copy(yb_ref.at[pl.ds(dest_ref[t * TOP_K + k], 1)], buf.at[k, pl.ds(t, 1)],
                                  sem).start(priority=k % 2)
        return c

    lax.fori_loop(0, TM, start, 0)

    def wait(t, c):
        for k in range(TOP_K):
            pltpu.make_async_copy(yb_ref.at[pl.ds(0, 1)], buf.at[k, pl.ds(0, 1)], sem).wait()
        return c

    lax.fori_loop(0, TM, wait, 0)
    w = topw_ref[...]
    y = jnp.zeros((TM, D), f32)
    for k in range(TOP_K):
        y = y + buf[k].reshape(TM, D) * w[:, k:k + 1]
    o_ref[...] = h_ref[...] + mod_ref[0][5:6, :] * y


def combine(dest_flat, yb, topw, h, mod_tiles):
    t = h.shape[0]
    tok = pl.BlockSpec((TM, D), lambda i: (i, 0))
    return pl.pallas_call(
        _combine_kernel, out_shape=_sds(h.shape), grid=(t // TM,),
        in_specs=[pl.BlockSpec((TM * TOP_K,), lambda i: (i,), memory_space=pltpu.SMEM),
                  pl.BlockSpec(memory_space=pl.ANY),
                  pl.BlockSpec((TM, LANES), lambda i: (i, 0)), tok,
                  pl.BlockSpec((1, 6, D), lambda i: (i, 0, 0))],
        out_specs=tok,
        scratch_shapes=[pltpu.VMEM((TOP_K, TM) + ROW3, f32), pltpu.SemaphoreType.DMA(())],
        compiler_params=_cparams(("arbitrary",)), name="moe_combine")(dest_flat, yb, topw, h, mod_tiles)


def moe_layer(h, mod_tiles, p):
    xn3, topi, topw, rank, counts = router(h, mod_tiles, p)
    cnt = counts[0, :N_EXPERTS]
    padded = (cnt + MOE_BM - 1) // MOE_BM * MOE_BM
    pend = jnp.cumsum(padded)
    pstart = jnp.zeros((1, LANES), i32).at[0, :N_EXPERTS].set(pend - padded)
    blk_row0 = jnp.arange(MOE_NBLK, dtype=i32) * MOE_BM
    block_e = jnp.minimum(jnp.sum((pend[None, :] <= blk_row0[:, None]).astype(i32), axis=1), N_EXPERTS - 1)
    nused = (pend[-1:] // MOE_BM).astype(i32)
    blk = jnp.arange(MOE_NBLK, dtype=i32)
    last_of_expert = jnp.any((padded > 0)[None, :] & (blk[:, None] == (pend // MOE_BM - 1)[None, :]), axis=1)
    zflag = (last_of_expert | (blk >= nused[0])).astype(i32)
    dest = dest_rows(topi, rank, pstart)
    dest_flat = dest[:, :TOP_K].reshape(-1)
    xb = dispatch(zflag, dest_flat, xn3)
    yb = experts(block_e, nused, xb, p)
    return combine(dest_flat, yb, topw, h, mod_tiles)


def _block_diag(w):
    per = CW // LRU_BW
    w5 = w.reshape(2, D // CW, per, LRU_BW, LRU_BW)
    eye = jnp.eye(per, dtype=w.dtype)
    bd = w5[:, :, :, :, None, :] * eye[None, None, :, None, :, None]
    return bd.reshape(2, D // CW, CW, CW).astype(bf16)


def _layer_params(l, a):
    tile2 = lambda g: jnp.tile(g, 2).reshape(1, LANES)
    whole = ("exp_w_gu", "exp_b_gu", "exp_w_down", "exp_b_down", "w_in", "w_gate", "w_branch", "w_out")
    p = {k: v[l] for k, v in a.items() if k not in whole}
    p["layer"] = l
    for k in ("exp_w_gu", "exp_b_gu", "exp_w_down", "exp_b_down"):
        p[k] = a[k]
    p["w_in_bf"] = cast_bf16(a["w_in"], l)
    p["w_gate_bf"] = cast_bf16(a["w_gate"], l)
    p["w_branch_bf"] = cast_bf16(a["w_branch"].reshape(DEPTH, 4 * D, D), l).reshape(4, D, D)
    p["w_out_bf"] = cast_bf16(a["w_out"], l)
    p["pool_w_bf"] = cast_bf16(a["pool_w"].reshape(DEPTH, D, CW), l).reshape(4, CW, CW)
    p["lru_wr_bd"] = _block_diag(p["lru_w_r"])
    p["lru_wi_bd"] = _block_diag(p["lru_w_i"])
    p["q_norm_g2"] = tile2(p["q_norm_g"])
    p["k_norm_g2"] = tile2(p["k_norm_g"])
    p["router_w_pad"] = jnp.pad(p["router_w"], ((0, 0), (0, LANES - N_EXPERTS)))
    p["router_b_pad"] = jnp.pad(p["router_b"], (0, LANES - N_EXPERTS)).reshape(1, LANES)
    return p


def kernel(x_prompt, x_sample, cache_k, cache_v, state_lru, c, c_ctx, norm1_g, norm2_g, w_mod, b_mod, w_in,
           lru_conv_w, lru_conv_b, lru_w_r, lru_b_r, lru_w_i, lru_b_i, lru_lambda, conv_dw_w, conv_dw_b,
           conv_ln_g, conv_ln_b, q_norm_g, k_norm_g, attn_sink, pool_w, pool_scale, w_branch, w_gate, b_gate,
           w_out, router_w, router_b, exp_w_gu, exp_b_gu, exp_w_down, exp_b_down):
    weights = dict(norm1_g=norm1_g, norm2_g=norm2_g, w_in=w_in, lru_conv_w=lru_conv_w, lru_conv_b=lru_conv_b,
                   lru_w_r=lru_w_r, lru_b_r=lru_b_r, lru_w_i=lru_w_i, lru_b_i=lru_b_i, lru_lambda=lru_lambda,
                   conv_dw_w=conv_dw_w, conv_dw_b=conv_dw_b, conv_ln_g=conv_ln_g, conv_ln_b=conv_ln_b,
                   q_norm_g=q_norm_g, k_norm_g=k_norm_g, attn_sink=attn_sink, pool_w=pool_w, pool_scale=pool_scale,
                   w_branch=w_branch, w_gate=w_gate, b_gate=b_gate, w_out=w_out, router_w=router_w,
                   router_b=router_b, exp_w_gu=exp_w_gu, exp_b_gu=exp_b_gu, exp_w_down=exp_w_down,
                   exp_b_down=exp_b_down)
    kvw = N_KV * HEAD_DIM
    n_cond = 2 * SUBLANES
    cond = jnp.concatenate([c_ctx[None, :], c, jnp.zeros((n_cond - 1 - N_LAT_B, D), f32)], axis=0)
    mod = modulation(cond, w_mod, b_mod)
    tile_start = jnp.arange(T // TM) * TM
    tile_row = jnp.where(tile_start < T_CTX, 0, 1 + (tile_start - T_CTX) // N_LAT)
    h = jnp.concatenate([x_prompt.reshape(T_CTX, D), x_sample.reshape(T_LAT, D)], axis=0)
    ck = cache_k.reshape(N_LAT_B, DEPTH, PAST, kvw)
    cv = cache_v.reshape(N_LAT_B, DEPTH, PAST, kvw)
    rope_cos, rope_sin = rope_tables()
    zero_state = jnp.zeros((N_CTX_B, 2, D), f32)
    new_k, new_v, new_s = [], [], []
    for l in range(DEPTH):
        p = _layer_params(l, weights)
        mod_tiles = mod[l][tile_row].reshape(T // TM, 6, D)
        xn = norm1(h, p["norm1_g"], mod_tiles)
        u = in_proj(xn, p["w_in_bf"])
        lat_blk0 = T_CTX // N_LAT
        lru_c, st_c = lru_branch(u, p, zero_state, N_CTX_B, N_CTX, 0)
        lru_l, _ = lru_branch(u, p, state_lru[:, l], N_LAT_B, N_LAT, lat_blk0)
        conv = (conf_branch(u, p, N_CTX_B, N_CTX, 0), conf_branch(u, p, N_LAT_B, N_LAT, lat_blk0))
        pool = (pool_branch(u, p, N_CTX_B, N_CTX, 0), pool_branch(u, p, N_LAT_B, N_LAT, lat_blk0))
        att_c, k_l, v_l = ctx_attention(u, p)
        att_l = lat_attention(u, p, ck, cv, rope_cos, rope_sin)
        h = merge(h, xn, (lru_c, lru_l), conv, (att_c, att_l), pool, mod_tiles, p)
        h = moe_layer(h, mod_tiles, p)
        new_k.append(k_l.reshape(N_CTX_B, N_CTX, N_KV, HEAD_DIM))
        new_v.append(v_l.reshape(N_CTX_B, N_CTX, N_KV, HEAD_DIM))
        new_s.append(st_c)
    y_prompt = h[:T_CTX].reshape(N_CTX_B, N_CTX, D)
    y_sample = h[T_CTX:].reshape(N_LAT_B, N_LAT, D)
    return (y_prompt, y_sample, jnp.stack(new_k, axis=1), jnp.stack(new_v, axis=1), jnp.stack(new_s, axis=1))
```

```python
import functools

import jax
import jax.numpy as jnp
from jax import lax
from jax.experimental import pallas as pl
from jax.experimental.pallas import tpu as pltpu

f32 = jnp.float32
bf16 = jnp.bfloat16
i32 = jnp.int32

D = 1024
N_CTX_B, N_CTX = 16, 256
N_LAT_B, N_LAT = 8, 1024
T_CTX = N_CTX_B * N_CTX
T_LAT = N_LAT_B * N_LAT
T = T_CTX + T_LAT
DEPTH = 2
PAST = 512
GRID_W = 64
IN_W = 6656
COL_UX, COL_UY, COL_UC, COL_UQ, COL_UK, COL_UV, COL_UP = 0, 1024, 2048, 4096, 5120, 5376, 5632
LRU_BW = 64
LRU_CONV = 4
LRU_C = 8.0
CONV_K = 31
N_HEADS, N_KV, HEAD_DIM = 16, 4, 64
POOL_SIZES = (2, 4, 8, 16)
N_EXPERTS, TOP_K = 32, 4
SWIGLU_LIMIT, SWIGLU_ALPHA = 7.0, 1.702
EPS = 1e-6
NEG_INF = -1e30
ROPE_BASE = 10000.0
LOG2E = 1.4426950408889634

LANES = 128
SUBLANES = 8
VMEM_LIMIT = 56 * 1024 * 1024

CW = 256
RC = 128
TQ = 128
TM = 256
MOE_BM = 256
N_ASSIGN = T * TOP_K
MOE_NBLK = N_ASSIGN // MOE_BM + N_EXPERTS
MOE_CAP = MOE_NBLK * MOE_BM


def _sds(shape, dt=f32):
    return jax.ShapeDtypeStruct(shape, dt)


def _cparams(sem, vmem=VMEM_LIMIT):
    return pltpu.CompilerParams(dimension_semantics=sem, vmem_limit_bytes=vmem)


def _sigmoid(x):
    return 0.5 * jnp.tanh(0.5 * x) + 0.5


def _log1p(z):
    u = 1.0 + z
    d = u - 1.0
    return jnp.where(d == 0.0, z, jnp.log(u) * (z / jnp.where(d == 0.0, 1.0, d)))


def _cast_kernel(x_ref, o_ref):
    o_ref[...] = x_ref[...].astype(o_ref.dtype)


def cast_bf16(w, l):
    _, r, c = w.shape
    tr = 256
    return pl.pallas_call(
        _cast_kernel, out_shape=_sds((r, c), bf16), grid=(r // tr,),
        in_specs=[pl.BlockSpec((None, tr, c), lambda i: (l, i, 0))],
        out_specs=pl.BlockSpec((tr, c), lambda i: (i, 0)),
        compiler_params=_cparams(("parallel",)), name="cast_bf16")(w)


def _mod_kernel(c_ref, w_ref, b_ref, o_ref):
    x = c_ref[...]
    s = x * _sigmoid(x)
    o_ref[0] = jnp.dot(s.astype(bf16), w_ref[0].astype(bf16), preferred_element_type=f32) + b_ref[0]


def modulation(cond, w_mod, b_mod):
    r = cond.shape[0]
    tn = 1536
    return pl.pallas_call(
        _mod_kernel, out_shape=_sds((DEPTH, r, 6 * D)), grid=(DEPTH, 6 * D // tn),
        in_specs=[pl.BlockSpec((r, D), lambda l, j: (0, 0)),
                  pl.BlockSpec((1, D, tn), lambda l, j: (l, 0, j)),
                  pl.BlockSpec((1, 1, tn), lambda l, j: (l, 0, j))],
        out_specs=pl.BlockSpec((1, r, tn), lambda l, j: (l, 0, j)),
        compiler_params=_cparams(("parallel", "parallel")), name="modulation")(
            cond, w_mod, b_mod.reshape(DEPTH, 1, 6 * D))


def _rms_mod(x, g, shift, scale):
    ms = jnp.mean(x * x, axis=-1, keepdims=True)
    return (x * lax.rsqrt(ms + EPS) * g) * (1.0 + scale) + shift


def _norm1_kernel(h_ref, g_ref, mod_ref, o_ref):
    m = mod_ref[0]
    o_ref[...] = _rms_mod(h_ref[...], g_ref[...], m[0:1, :], m[1:2, :]).astype(bf16)


def norm1(h, g, mod_tiles):
    nt = h.shape[0] // TM
    return pl.pallas_call(
        _norm1_kernel, out_shape=_sds(h.shape, bf16), grid=(nt,),
        in_specs=[pl.BlockSpec((TM, D), lambda i: (i, 0)),
                  pl.BlockSpec((1, D), lambda i: (0, 0)),
                  pl.BlockSpec((1, 6, D), lambda i: (i, 0, 0))],
        out_specs=pl.BlockSpec((TM, D), lambda i: (i, 0)),
        compiler_params=_cparams(("parallel",)), name="norm1")(h, g.reshape(1, D), mod_tiles)


def _matmul_kernel(x_ref, w_ref, o_ref):
    o_ref[...] = jnp.dot(x_ref[...], w_ref[...], preferred_element_type=f32)


def in_proj(xn, w):
    t = xn.shape[0]
    tm, tn = 1024, 1664
    return pl.pallas_call(
        _matmul_kernel, out_shape=_sds((t, IN_W)), grid=(t // tm, IN_W // tn),
        in_specs=[pl.BlockSpec((tm, D), lambda i, j: (i, 0)),
                  pl.BlockSpec((D, tn), lambda i, j: (0, j))],
        out_specs=pl.BlockSpec((tm, tn), lambda i, j: (i, j)),
        compiler_params=_cparams(("parallel", "parallel")), name="in_proj")(xn, w)


def _gelu_tanh(x):
    return 0.5 * x * (1.0 + jnp.tanh(0.7978845608028654 * (x + 0.044715 * (x * x * x))))


def _lru_kernel(ux_ref, uy_ref, cw_ref, cb_ref, wr_ref, wi_ref, br_ref, bi_ref, lam_ref, h0_ref,
                out_ref, st_ref, xp_ref, a_ref, b_ref, hs_ref, *, n):
    pad = SUBLANES
    xp_ref[pl.ds(0, pad), :] = jnp.zeros((pad, CW), f32)
    xp_ref[pl.ds(n + pad, pad), :] = jnp.zeros((pad, CW), f32)
    xp_ref[pl.ds(pad, n), :] = ux_ref[...]
    left = LRU_CONV // 2
    lam = lam_ref[...]
    neg_c_sp = -LRU_C * (jnp.maximum(-lam, 0.0) + _log1p(jnp.exp(-jnp.abs(lam))))

    def coef_body(i, c):
        r0 = pl.multiple_of(i * RC, RC)
        w = xp_ref[pl.ds(r0, RC + 2 * pad), :]
        xc = jnp.zeros((RC, CW), f32) + cb_ref[...]
        for k in range(LRU_CONV):
            off = pad - left + k
            xc = xc + cw_ref[k:k + 1, :] * w[off:off + RC, :]
        xcb = xc.astype(bf16)
        for d in range(2):
            r = _sigmoid(jnp.dot(xcb, wr_ref[d, 0], preferred_element_type=f32) + br_ref[d:d + 1, :])
            g = _sigmoid(jnp.dot(xcb, wi_ref[d, 0], preferred_element_type=f32) + bi_ref[d:d + 1, :])
            log_a = r * neg_c_sp[d:d + 1, :]
            a = jnp.exp(log_a)
            th = jnp.tanh(log_a)
            one_m_a2 = (-2.0 * th) / (1.0 - th)
            a_ref[d, pl.ds(r0, RC), :] = a
            b_ref[d, pl.ds(r0, RC), :] = jnp.sqrt(one_m_a2) * (g * xc)
        return c

    lax.fori_loop(0, n // RC, coef_body, 0)

    row = lax.broadcasted_iota(i32, (SUBLANES, CW), 0)
    nchunk = n // SUBLANES

    def scan_body(j, carry):
        hf, hb = carry
        rf = pl.multiple_of(j * SUBLANES, SUBLANES)
        a = a_ref[0, pl.ds(rf, SUBLANES), :]
        b = b_ref[0, pl.ds(rf, SUBLANES), :]
        for sh in (1, 2, 4):
            a_s = jnp.where(row >= sh, pltpu.roll(a, sh, axis=0), 1.0)
            b_s = jnp.where(row >= sh, pltpu.roll(b, sh, axis=0), 0.0)
            b = a * b_s + b
            a = a * a_s
        h = a * hf + b
        hs_ref[0, pl.ds(rf, SUBLANES), :] = h
        hf = h[SUBLANES - 1:SUBLANES, :]
        rb = pl.multiple_of((nchunk - 1 - j) * SUBLANES, SUBLANES)
        a = a_ref[1, pl.ds(rb, SUBLANES), :]
        b = b_ref[1, pl.ds(rb, SUBLANES), :]
        for sh in (1, 2, 4):
            keep = row < SUBLANES - sh
            a_s = jnp.where(keep, pltpu.roll(a, SUBLANES - sh, axis=0), 1.0)
            b_s = jnp.where(keep, pltpu.roll(b, SUBLANES - sh, axis=0), 0.0)
            b = a * b_s + b
            a = a * a_s
        h = a * hb + b
        hs_ref[1, pl.ds(rb, SUBLANES), :] = h
        hb = h[0:1, :]
        return hf, hb

    h0 = h0_ref[0]
    hf, hb = lax.fori_loop(0, nchunk, scan_body, (h0[0:1, :], h0[1:2, :]), unroll=4)
    st_ref[0, 0:1, :] = hf
    st_ref[0, 1:2, :] = hb

    def out_body(i, c):
        r0 = pl.multiple_of(i * RC, RC)
        hsum = hs_ref[0, pl.ds(r0, RC), :] + hs_ref[1, pl.ds(r0, RC), :]
        out_ref[pl.ds(r0, RC), :] = hsum * _gelu_tanh(uy_ref[pl.ds(r0, RC), :])
        return c

    lax.fori_loop(0, n // RC, out_body, 0)


def lru_branch(u, p, h0, nb, n, row_blk0):
    nct = D // CW
    cu = COL_UY // CW
    kern = functools.partial(_lru_kernel, n=n)
    vec = lambda a: pl.BlockSpec((a, CW), lambda b, c: (0, c))
    call = pl.pallas_call(
        kern, out_shape=(_sds((nb * n, D)), _sds((nb, 2, D))), grid=(nb, nct),
        in_specs=[pl.BlockSpec((n, CW), lambda b, c: (row_blk0 + b, c)),
                  pl.BlockSpec((n, CW), lambda b, c: (row_blk0 + b, cu + c)),
                  vec(LRU_CONV), vec(1),
                  pl.BlockSpec((2, 1, CW, CW), lambda b, c: (0, c, 0, 0)),
                  pl.BlockSpec((2, 1, CW, CW), lambda b, c: (0, c, 0, 0)),
                  vec(2), vec(2), vec(2),
                  pl.BlockSpec((1, 2, CW), lambda b, c: (b, 0, c))],
        out_specs=(pl.BlockSpec((n, CW), lambda b, c: (b, c)),
                   pl.BlockSpec((1, 2, CW), lambda b, c: (b, 0, c))),
        scratch_shapes=[pltpu.VMEM((n + 2 * SUBLANES, CW), f32),
                        pltpu.VMEM((2, n, CW), f32), pltpu.VMEM((2, n, CW), f32), pltpu.VMEM((2, n, CW), f32)],
        compiler_params=_cparams(("parallel", "parallel")), name="lru_branch")
    return call(u, u, p["lru_conv_w"], p["lru_conv_b"].reshape(1, D), p["lru_wr_bd"], p["lru_wi_bd"],
                p["lru_b_r"], p["lru_b_i"], p["lru_lambda"], h0)


def _conf_kernel(ua_ref, ug_ref, w_ref, b_ref, o_ref, gp_ref, *, n):
    pad = 2 * SUBLANES
    left = CONV_K // 2
    gp_ref[pl.ds(0, pad), :] = jnp.zeros((pad, CW), f32)
    gp_ref[pl.ds(n + pad, pad), :] = jnp.zeros((pad, CW), f32)
    gp_ref[pl.ds(pad, n), :] = ua_ref[...] * _sigmoid(ug_ref[...])

    wn = RC + 2 * pad

    def body(i, c):
        r0 = pl.multiple_of(i * RC, RC)
        w = gp_ref[pl.ds(r0, wn), :]
        rolled = [w] + [pltpu.roll(w, wn - m, axis=0) for m in range(1, SUBLANES)]
        acc = jnp.zeros((RC, CW), f32) + b_ref[...]
        for k in range(CONV_K):
            off = pad - left + k
            q, m = off // SUBLANES, off % SUBLANES
            acc = acc + w_ref[k:k + 1, :] * rolled[m][q * SUBLANES:q * SUBLANES + RC, :]
        o_ref[pl.ds(r0, RC), :] = acc
        return c

    lax.fori_loop(0, n // RC, body, 0)


def conf_branch(u, p, nb, n, row_blk0):
    nct = D // CW
    ca, cg = COL_UC // CW, (COL_UC + D) // CW
    kern = functools.partial(_conf_kernel, n=n)
    call = pl.pallas_call(
        kern, out_shape=_sds((nb * n, D)), grid=(nb, nct),
        in_specs=[pl.BlockSpec((n, CW), lambda b, c: (row_blk0 + b, ca + c)),
                  pl.BlockSpec((n, CW), lambda b, c: (row_blk0 + b, cg + c)),
                  pl.BlockSpec((CONV_K, CW), lambda b, c: (0, c)),
                  pl.BlockSpec((1, CW), lambda b, c: (0, c))],
        out_specs=pl.BlockSpec((n, CW), lambda b, c: (b, c)),
        scratch_shapes=[pltpu.VMEM((n + 4 * SUBLANES, CW), f32)],
        compiler_params=_cparams(("parallel", "parallel")), name="conf_branch")
    return call(u, u, p["conv_dw_w"], p["conv_dw_b"].reshape(1, D))


def _pool_kernel(up_ref, w_ref, s_ref, o_ref, xp_ref, *, n):
    pad = SUBLANES
    gi = pl.program_id(1)
    half = jnp.left_shift(1, gi)
    xp_ref[pl.ds(0, pad), :] = jnp.zeros((pad, CW), f32)
    xp_ref[pl.ds(n + pad, pad), :] = jnp.zeros((pad, CW), f32)
    xp_ref[pl.ds(pad, n), :] = up_ref[...]
    wn = RC + 2 * pad

    def body(i, c):
        r0 = pl.multiple_of(i * RC, RC)
        w = xp_ref[pl.ds(r0, wn), :]
        s2 = w + pltpu.roll(w, 1, axis=0)
        s4 = pltpu.roll(s2, 1, axis=0) + pltpu.roll(s2, wn - 1, axis=0)
        s8 = pltpu.roll(s4, 2, axis=0) + pltpu.roll(s4, wn - 2, axis=0)
        s16 = pltpu.roll(s8, 4, axis=0) + pltpu.roll(s8, wn - 4, axis=0)
        s = jnp.where(gi == 0, s2, jnp.where(gi == 1, s4, jnp.where(gi == 2, s8, s16)))[pad:pad + RC, :]
        t = r0 + lax.broadcasted_iota(i32, (RC, CW), 0)
        cnt = (jnp.minimum(t + half, n) - jnp.maximum(t - half, 0)).astype(f32)
        pooled = s / cnt - w[pad:pad + RC, :]
        o_ref[pl.ds(r0, RC), :] = jnp.dot(pooled.astype(bf16), w_ref[0], preferred_element_type=f32) * s_ref[...]
        return c

    lax.fori_loop(0, n // RC, body, 0)


def pool_branch(u, p, nb, n, row_blk0):
    assert POOL_SIZES == (2, 4, 8, 16) and D // len(POOL_SIZES) == CW
    cp = COL_UP // CW
    kern = functools.partial(_pool_kernel, n=n)
    call = pl.pallas_call(
        kern, out_shape=_sds((nb * n, D)), grid=(nb, len(POOL_SIZES)),
        in_specs=[pl.BlockSpec((n, CW), lambda b, c: (row_blk0 + b, cp + c)),
                  pl.BlockSpec((1, CW, CW), lambda b, c: (c, 0, 0)),
                  pl.BlockSpec((1, CW), lambda b, c: (0, c))],
        out_specs=pl.BlockSpec((n, CW), lambda b, c: (b, c)),
        scratch_shapes=[pltpu.VMEM((n + 2 * SUBLANES, CW), f32)],
        compiler_params=_cparams(("parallel", "parallel")), name="pool_branch")
    return call(u, p["pool_w_bf"], p["pool_scale"].reshape(1, D))


def _head_norm(blk, g128, lane_lo):
    sq = blk * blk
    s_lo = jnp.sum(jnp.where(lane_lo, sq, 0.0), axis=-1, keepdims=True)
    s_hi = jnp.sum(jnp.where(lane_lo, 0.0, sq), axis=-1, keepdims=True)
    r = jnp.where(lane_lo, lax.rsqrt(s_lo * (1.0 / HEAD_DIM) + EPS), lax.rsqrt(s_hi * (1.0 / HEAD_DIM) + EPS))
    return blk * r * g128


def _rope(blk, cos, sin_signed, lane):
    partner = jnp.where((lane % 32) < 16, pltpu.roll(blk, LANES - 16, axis=1), pltpu.roll(blk, 16, axis=1))
    return blk * cos + partner * sin_signed


def _prep_kv(k_ref, v_ref, ks_ref, vs_ref, kg, lane_lo, rope=None, newk_ref=None, newv_ref=None):
    nk = k_ref.shape[0]
    lane = lax.broadcasted_iota(i32, (nk, LANES), 1)
    for tj in range(N_KV * HEAD_DIM // LANES):
        kt = _head_norm(k_ref[:, tj * LANES:(tj + 1) * LANES], kg, lane_lo(nk))
        if rope is not None:
            kt = _rope(kt, rope[0][...], rope[1][...], lane)
        vt = v_ref[:, tj * LANES:(tj + 1) * LANES]
        if newk_ref is not None:
            newk_ref[0, :, tj * LANES:(tj + 1) * LANES] = kt
            newv_ref[0, :, tj * LANES:(tj + 1) * LANES] = vt
        _store_low_half(kt, vt, ks_ref, vs_ref, tj, lane_lo(nk))


def _store_low_half(kt, vt, ks_ref, vs_ref, tj, lane_lo):
    ks_ref[2 * tj] = jnp.where(lane_lo, kt, 0.0).astype(bf16)
    vs_ref[2 * tj] = jnp.where(lane_lo, vt, 0.0).astype(bf16)
    ks_ref[2 * tj + 1] = jnp.where(lane_lo, pltpu.roll(kt, HEAD_DIM, axis=1), 0.0).astype(bf16)
    vs_ref[2 * tj + 1] = jnp.where(lane_lo, pltpu.roll(vt, HEAD_DIM, axis=1), 0.0).astype(bf16)


def _prep_cache(ck_ref, cv_ref, kc_ref, vc_ref, lane_lo):
    nk = ck_ref.shape[2]
    for tj in range(N_KV * HEAD_DIM // LANES):
        kt = ck_ref[0, 0, :, tj * LANES:(tj + 1) * LANES]
        vt = cv_ref[0, 0, :, tj * LANES:(tj + 1) * LANES]
        _store_low_half(kt, vt, kc_ref, vc_ref, tj, lane_lo(nk))


def _qk(qm, k):
    return lax.dot_general(qm, k, (((1,), (1,)), ((), ())), preferred_element_type=f32)


def _attend(sink_ref, q_tiles, segments, att_ref):
    per_kv = N_HEADS // N_KV
    assert per_kv == 4
    rows = per_kv * TQ
    row = lax.broadcasted_iota(i32, (rows, 1), 0)
    masks = [None if mk is None else jnp.concatenate([mk] * per_kv, axis=0) for _, _, mk in segments]
    for g in range(N_KV):
        t0, t1 = q_tiles[2 * g], q_tiles[2 * g + 1]
        qm = jnp.concatenate([t0, t1, pltpu.roll(t0, HEAD_DIM, axis=1), pltpu.roll(t1, HEAD_DIM, axis=1)],
                             axis=0).astype(bf16)
        hd = [per_kv * g, per_kv * g + 2, per_kv * g + 1, per_kv * g + 3]
        sink = jnp.where(row < TQ, sink_ref[hd[0]],
                         jnp.where(row < 2 * TQ, sink_ref[hd[1]],
                                   jnp.where(row < 3 * TQ, sink_ref[hd[2]], sink_ref[hd[3]]))) * LOG2E
        scores = []
        m_el = None
        for (kget, _, _), mask in zip(segments, masks):
            s = _qk(qm, kget(g))
            if mask is not None:
                s = jnp.where(mask, s, NEG_INF)
            for c in range(s.shape[1] // LANES):
                t = s[:, c * LANES:(c + 1) * LANES]
                m_el = t if m_el is None else jnp.maximum(m_el, t)
            scores.append(s)
        m = jnp.maximum(jnp.max(m_el, axis=-1, keepdims=True), sink)
        d_el = jnp.zeros((rows, LANES), f32)
        o = jnp.zeros((rows, LANES), f32)
        for s, (_, vget, _) in zip(scores, segments):
            pr = jnp.exp2(s - m)
            for c in range(s.shape[1] // LANES):
                d_el = d_el + pr[:, c * LANES:(c + 1) * LANES]
            o = o + jnp.dot(pr.astype(bf16), vget(g), preferred_element_type=f32)
        den = jnp.exp2(sink - m) + jnp.sum(d_el, axis=-1, keepdims=True)
        o = o / den
        att_ref[:, 2 * g * LANES:(2 * g + 1) * LANES] = o[:TQ] + pltpu.roll(o[2 * TQ:3 * TQ], HEAD_DIM, axis=1)
        att_ref[:, (2 * g + 1) * LANES:(2 * g + 2) * LANES] = o[TQ:2 * TQ] + pltpu.roll(o[3 * TQ:], HEAD_DIM, axis=1)


def _q_tiles(q_ref, qg, rope=None):
    lane = lax.broadcasted_iota(i32, (TQ, LANES), 1)
    lane_lo = lane < HEAD_DIM
    tiles = []
    for j in range(N_HEADS // 2):
        qt = _head_norm(q_ref[:, j * LANES:(j + 1) * LANES], qg, lane_lo)
        if rope is not None:
            qt = _rope(qt, rope[0], rope[1], lane)
        tiles.append(qt * (HEAD_DIM ** -0.5 * LOG2E))
    return tiles


def _lane_lo_fn(nrows):
    return lax.broadcasted_iota(i32, (nrows, LANES), 1) < HEAD_DIM


def _ctx_attn_kernel(sink_ref, q_ref, k_ref, v_ref, qg_ref, kg_ref, att_ref, newk_ref, newv_ref, ks_ref, vs_ref):
    @pl.when(pl.program_id(1) == 0)
    def _():
        _prep_kv(k_ref, v_ref, ks_ref, vs_ref, kg_ref[...], _lane_lo_fn, None, newk_ref, newv_ref)

    tiles = _q_tiles(q_ref, qg_ref[...])
    seg = [(lambda g: ks_ref[g], lambda g: vs_ref[g], None)]
    _attend(sink_ref, tiles, seg, att_ref)


def ctx_attention(u, p):
    nq = N_CTX // TQ
    kvw = N_KV * HEAD_DIM
    return pl.pallas_call(
        _ctx_attn_kernel,
        out_shape=(_sds((T_CTX, D)), _sds((N_CTX_B, N_CTX, kvw)), _sds((N_CTX_B, N_CTX, kvw))),
        grid=(N_CTX_B, nq),
        in_specs=[pl.BlockSpec(memory_space=pltpu.SMEM),
                  pl.BlockSpec((TQ, D), lambda b, i: (b * nq + i, COL_UQ // D)),
                  pl.BlockSpec((N_CTX, kvw), lambda b, i: (b, COL_UK // kvw)),
                  pl.BlockSpec((N_CTX, kvw), lambda b, i: (b, COL_UV // kvw)),
                  pl.BlockSpec((1, LANES), lambda b, i: (0, 0)),
                  pl.BlockSpec((1, LANES), lambda b, i: (0, 0))],
        out_specs=(pl.BlockSpec((TQ, D), lambda b, i: (b * nq + i, 0)),
                   pl.BlockSpec((1, N_CTX, kvw), lambda b, i: (b, 0, 0)),
                   pl.BlockSpec((1, N_CTX, kvw), lambda b, i: (b, 0, 0))),
        scratch_shapes=[pltpu.VMEM((N_KV, N_CTX, LANES), bf16), pltpu.VMEM((N_KV, N_CTX, LANES), bf16)],
        compiler_params=_cparams(("parallel", "arbitrary")), name="ctx_attention")(
            p["attn_sink"], u, u, u, p["q_norm_g2"], p["k_norm_g2"])


def _lat_attn_kernel(sink_ref, q_ref, k_ref, v_ref, ck_ref, cv_ref, cosq_ref, sinq_ref, cosk_ref, sink_tab_ref,
                     qg_ref, kg_ref, att_ref, ks_ref, vs_ref, kc_ref, vc_ref):
    i = pl.program_id(1)
    nq = N_LAT // TQ

    @pl.when(i == 0)
    def _():
        _prep_kv(k_ref, v_ref, ks_ref, vs_ref, kg_ref[...], _lane_lo_fn, (cosk_ref, sink_tab_ref))
        _prep_cache(ck_ref, cv_ref, kc_ref, vc_ref, _lane_lo_fn)

    tiles = _q_tiles(q_ref, qg_ref[...], (cosq_ref[...], sinq_ref[...]))
    r = lax.broadcasted_iota(i32, (TQ, TQ), 0)
    c = lax.broadcasted_iota(i32, (TQ, TQ), 1)
    prev0 = pl.multiple_of(jnp.maximum(i - 1, 0) * TQ, TQ)
    cur0 = pl.multiple_of(i * TQ, TQ)
    next0 = pl.multiple_of(jnp.minimum(i + 1, nq - 1) * TQ, TQ)
    mask_prev = jnp.logical_and(c >= r, i > 0)
    mask_next = jnp.logical_and(c <= r, i < nq - 1)

    def seg(r0, mask):
        return (lambda g: ks_ref[g, pl.ds(r0, TQ), :], lambda g: vs_ref[g, pl.ds(r0, TQ), :], mask)

    segments = [seg(prev0, mask_prev), seg(cur0, None), seg(next0, mask_next),
                (lambda g: kc_ref[g], lambda g: vc_ref[g], None)]
    _attend(sink_ref, tiles, segments, att_ref)


def lat_attention(u, p, cache_k, cache_v, rope_cos, rope_sin):
    nq = N_LAT // TQ
    kvw = N_KV * HEAD_DIM
    l = p["layer"]
    qblk0 = T_CTX // TQ
    kblk0 = T_CTX // N_LAT
    call = pl.pallas_call(
        _lat_attn_kernel, out_shape=_sds((T_LAT, D)), grid=(N_LAT_B, nq),
        in_specs=[pl.BlockSpec(memory_space=pltpu.SMEM),
                  pl.BlockSpec((TQ, D), lambda b, i: (qblk0 + b * nq + i, COL_UQ // D)),
                  pl.BlockSpec((N_LAT, kvw), lambda b, i: (kblk0 + b, COL_UK // kvw)),
                  pl.BlockSpec((N_LAT, kvw), lambda b, i: (kblk0 + b, COL_UV // kvw)),
                  pl.BlockSpec((1, 1, PAST, kvw), lambda b, i: (b, l, 0, 0)),
                  pl.BlockSpec((1, 1, PAST, kvw), lambda b, i: (b, l, 0, 0)),
                  pl.BlockSpec((TQ, LANES), lambda b, i: (i, 0)),
                  pl.BlockSpec((TQ, LANES), lambda b, i: (i, 0)),
                  pl.BlockSpec((N_LAT, LANES), lambda b, i: (0, 0)),
                  pl.BlockSpec((N_LAT, LANES), lambda b, i: (0, 0)),
                  pl.BlockSpec((1, LANES), lambda b, i: (0, 0)),
                  pl.BlockSpec((1, LANES), lambda b, i: (0, 0))],
        out_specs=pl.BlockSpec((TQ, D), lambda b, i: (b * nq + i, 0)),
        scratch_shapes=[pltpu.VMEM((N_KV, N_LAT, LANES), bf16), pltpu.VMEM((N_KV, N_LAT, LANES), bf16),
                        pltpu.VMEM((N_KV, PAST, LANES), bf16), pltpu.VMEM((N_KV, PAST, LANES), bf16)],
        compiler_params=_cparams(("parallel", "arbitrary")), name="lat_attention")
    return call(p["attn_sink"], u, u, u, cache_k, cache_v, rope_cos, rope_sin, rope_cos, rope_sin,
                p["q_norm_g2"], p["k_norm_g2"])


def rope_tables():
    pos = jnp.arange(N_LAT)
    rows = (pos // GRID_W).astype(f32)
    cols = (pos % GRID_W).astype(f32)
    half = HEAD_DIM // 2
    freqs = ROPE_BASE ** (-jnp.arange(0, half, 2, dtype=f32) / half)
    lane = jnp.arange(LANES)
    within = lane % half
    fidx = within % (half // 2)
    use_cols = (lane % HEAD_DIM) >= half
    ang = jnp.where(use_cols[None, :], cols[:, None], rows[:, None]) * freqs[fidx][None, :]
    sign = jnp.where(within < half // 2, -1.0, 1.0)
    return jnp.cos(ang), jnp.sin(ang) * sign[None, :]


def _merge_kernel(h_ref, xn_ref, lru_c, lru_l, conv_c, conv_l, att_c, att_l, pool_c, pool_l, mod_ref, lng_ref, lnb_ref,
                  wg_ref, bg_ref, wb_ref, wo_ref, o_ref):
    is_ctx = pl.program_id(0) < T_CTX // TM
    pick = lambda c_ref, l_ref: jnp.where(is_ctx, c_ref[...], l_ref[...])
    xn = xn_ref[...]
    hc = pick(conv_c, conv_l)
    mu = jnp.mean(hc, axis=-1, keepdims=True)
    xc = hc - mu
    var = jnp.mean(xc * xc, axis=-1, keepdims=True)
    y = xc * lax.rsqrt(var + EPS) * lng_ref[...] + lnb_ref[...]
    conv = y * _sigmoid(y)
    merged = jnp.zeros((TM, D), f32)
    for j, br in enumerate((pick(lru_c, lru_l), conv, pick(att_c, att_l), pick(pool_c, pool_l))):
        gate = _sigmoid(jnp.dot(xn, wg_ref[:, j * D:(j + 1) * D], preferred_element_type=f32)
                        + bg_ref[:, j * D:(j + 1) * D])
        proj = jnp.dot(br.astype(bf16), wb_ref[j], preferred_element_type=f32)
        merged = merged + gate * proj
    out = jnp.dot(merged.astype(bf16), wo_ref[...], preferred_element_type=f32)
    o_ref[...] = h_ref[...] + mod_ref[0][2:3, :] * out


def merge(h, xn, lru, conv, att, pool, mod_tiles, p):
    nt = h.shape[0] // TM
    n_ctx_tiles = T_CTX // TM
    tok = pl.BlockSpec((TM, D), lambda i: (i, 0))
    tok_c = pl.BlockSpec((TM, D), lambda i: (jnp.minimum(i, n_ctx_tiles - 1), 0))
    tok_l = pl.BlockSpec((TM, D), lambda i: (jnp.maximum(i - n_ctx_tiles, 0), 0))
    const = lambda shape: pl.BlockSpec(shape, lambda i: (0,) * len(shape), pipeline_mode=pl.Buffered(1))
    return pl.pallas_call(
        _merge_kernel, out_shape=_sds(h.shape), grid=(nt,),
        in_specs=[tok, tok, tok_c, tok_l, tok_c, tok_l, tok_c, tok_l, tok_c, tok_l,
                  pl.BlockSpec((1, 6, D), lambda i: (i, 0, 0)),
                  const((1, D)), const((1, D)),
                  const((D, 4 * D)), const((1, 4 * D)), const((4, D, D)), const((D, D))],
        out_specs=tok,
        compiler_params=_cparams(("parallel",)), name="merge")(
            h, xn, *lru, *conv, *att, *pool, mod_tiles, p["conv_ln_g"].reshape(1, D), p["conv_ln_b"].reshape(1, D),
            p["w_gate_bf"], p["b_gate"].reshape(1, 4 * D), p["w_branch_bf"], p["w_out_bf"])


def _router_kernel(h_ref, g_ref, mod_ref, rw_ref, rb_ref, xn_ref, topi_ref, topw_ref, rank_ref, cnt_ref, carry_ref):
    step = pl.program_id(0)

    @pl.when(step == 0)
    def _():
        carry_ref[...] = jnp.zeros_like(carry_ref)

    m = mod_ref[0]
    xn = _rms_mod(h_ref[...], g_ref[...], m[3:4, :], m[4:5, :])
    xn_ref[...] = xn.reshape(xn_ref.shape)
    logits = jnp.dot(xn, rw_ref[...], preferred_element_type=f32, precision=lax.Precision.HIGHEST) + rb_ref[...]
    lane = lax.broadcasted_iota(i32, (TM, LANES), 1).astype(f32)
    lg = jnp.where(lane < N_EXPERTS, logits, NEG_INF)
    vals, idxs = [], []
    for _ in range(TOP_K):
        mx = jnp.max(lg, axis=-1, keepdims=True)
        idx = jnp.min(jnp.where(lg == mx, lane, float(LANES)), axis=-1, keepdims=True)
        vals.append(mx)
        idxs.append(idx)
        lg = jnp.where(lane == idx, -3e38, lg)
    exps = [jnp.exp(v - vals[0]) for v in vals]
    den = exps[0] + exps[1] + exps[2] + exps[3]
    cnt = jnp.zeros((TM, LANES), f32)
    for idx in idxs:
        cnt = cnt + jnp.where(lane == idx, 1.0, 0.0)
    rr = lax.broadcasted_iota(i32, (TM, TM), 0)
    cc = lax.broadcasted_iota(i32, (TM, TM), 1)
    tri = jnp.where(rr > cc, 1.0, 0.0).astype(bf16)
    before = jnp.dot(tri, cnt.astype(bf16), preferred_element_type=f32) + carry_ref[0:1, :]
    topi = jnp.zeros((TM, LANES), f32)
    topw = jnp.zeros((TM, LANES), f32)
    rank = jnp.zeros((TM, LANES), f32)
    for k in range(TOP_K):
        rk = jnp.sum(jnp.where(lane == idxs[k], before, 0.0), axis=-1, keepdims=True)
        topi = jnp.where(lane == k, idxs[k], topi)
        topw = jnp.where(lane == k, exps[k] / den, topw)
        rank = jnp.where(lane == k, rk, rank)
    topi_ref[...] = topi.astype(i32)
    topw_ref[...] = topw
    rank_ref[...] = rank.astype(i32)
    total = carry_ref[0:1, :] + jnp.sum(cnt, axis=0, keepdims=True)
    carry_ref[...] = jnp.broadcast_to(total, carry_ref.shape)
    cnt_ref[...] = jnp.broadcast_to(total, cnt_ref.shape).astype(i32)


def router(h, mod_tiles, p):
    nt = h.shape[0] // TM
    tok = lambda w, dt=f32: pl.BlockSpec((TM, w), lambda i: (i, 0))
    t = h.shape[0]
    return pl.pallas_call(
        _router_kernel,
        out_shape=(_sds((t, D // LANES, LANES)), _sds((t, LANES), i32), _sds((t, LANES)), _sds((t, LANES), i32),
                   _sds((SUBLANES, LANES), i32)),
        grid=(nt,),
        in_specs=[tok(D), pl.BlockSpec((1, D), lambda i: (0, 0)), pl.BlockSpec((1, 6, D), lambda i: (i, 0, 0)),
                  pl.BlockSpec((D, LANES), lambda i: (0, 0)), pl.BlockSpec((1, LANES), lambda i: (0, 0))],
        out_specs=(pl.BlockSpec((TM, D // LANES, LANES), lambda i: (i, 0, 0)), tok(LANES), tok(LANES), tok(LANES),
                   pl.BlockSpec((SUBLANES, LANES), lambda i: (0, 0))),
        scratch_shapes=[pltpu.VMEM((SUBLANES, LANES), f32)],
        compiler_params=_cparams(("arbitrary",)), name="router")(
            h, p["norm2_g"].reshape(1, D), mod_tiles, p["router_w_pad"], p["router_b_pad"])


def _dest_kernel(topi_ref, rank_ref, pstart_ref, o_ref):
    lane = lax.broadcasted_iota(i32, (TM, LANES), 1).astype(f32)
    topi = topi_ref[...].astype(f32)
    pstart = pstart_ref[...].astype(f32)
    dest = rank_ref[...].astype(f32)
    for k in range(TOP_K):
        e = jnp.sum(jnp.where(lane == k, topi, 0.0), axis=-1, keepdims=True)
        ps = jnp.sum(jnp.where(lane == e, pstart, 0.0), axis=-1, keepdims=True)
        dest = dest + jnp.where(lane == k, ps, 0.0)
    o_ref[...] = dest.astype(i32)


def dest_rows(topi, rank, pstart):
    t = topi.shape[0]
    tok = pl.BlockSpec((TM, LANES), lambda i: (i, 0))
    return pl.pallas_call(
        _dest_kernel, out_shape=_sds((t, LANES), i32), grid=(t // TM,),
        in_specs=[tok, tok, pl.BlockSpec((1, LANES), lambda i: (0, 0))], out_specs=tok,
        compiler_params=_cparams(("parallel",)), name="dest_rows")(topi, rank, pstart)


ROW3 = (D // LANES, LANES)
assert ROW3[0] == SUBLANES


def _row_copy(src_ref, s, dst_ref, d, sem):
    return pltpu.make_async_copy(src_ref.at[pl.ds(s, 1)], dst_ref.at[pl.ds(d, 1)], sem)


def _dispatch_kernel(zflag_ref, dest_ref, x_ref, xb_ref, zero_ref, sem):
    @pl.when(pl.program_id(0) == 0)
    def _():
        zero_ref[...] = jnp.zeros_like(zero_ref)

        def zcopy(j):
            r0 = pl.multiple_of(j * MOE_BM, MOE_BM)
            return pltpu.make_async_copy(zero_ref, xb_ref.at[pl.ds(r0, MOE_BM)], sem)

        def zstart(j, c):
            @pl.when(zflag_ref[j] != 0)
            def _():
                zcopy(j).start()
            return c

        def zwait(j, c):
            @pl.when(zflag_ref[j] != 0)
            def _():
                zcopy(j).wait()
            return c

        lax.fori_loop(0, MOE_NBLK, zstart, 0)
        lax.fori_loop(0, MOE_NBLK, zwait, 0)

    def start(t, c):
        for k in range(TOP_K):
            _row_copy(x_ref, t, xb_ref, dest_ref[t * TOP_K + k], sem).start(priority=k % 2)
        return c

    lax.fori_loop(0, TM, start, 0)

    def wait(t, c):
        for k in range(TOP_K):
            _row_copy(x_ref, 0, xb_ref, 0, sem).wait()
        return c

    lax.fori_loop(0, TM, wait, 0)


def dispatch(zflag, dest_flat, xn3):
    t = xn3.shape[0]
    grid_spec = pltpu.PrefetchScalarGridSpec(
        num_scalar_prefetch=1, grid=(t // TM,),
        in_specs=[pl.BlockSpec((TM * TOP_K,), lambda i, zb: (i,), memory_space=pltpu.SMEM),
                  pl.BlockSpec((TM,) + ROW3, lambda i, zb: (i, 0, 0))],
        out_specs=pl.BlockSpec(memory_space=pl.ANY),
        scratch_shapes=[pltpu.VMEM((MOE_BM,) + ROW3, f32), pltpu.SemaphoreType.DMA(())])
    return pl.pallas_call(
        _dispatch_kernel, out_shape=_sds((MOE_CAP,) + ROW3), grid_spec=grid_spec,
        compiler_params=_cparams(("arbitrary",)), name="moe_dispatch")(zflag, dest_flat, xn3)


def _expert_kernel(be_ref, nused_ref, first_ref, wslot_ref, nxt_ref, xb_ref, wgu_hbm, bgu_ref, wdn_hbm, bdn_ref,
                   yb_ref, wgu_f, wdn_f, wsem, wgu_bf, wdn_bf, *, layer):
    i = pl.program_id(0)
    e = be_ref[i]

    def weight_copies(ex, s):
        return (pltpu.make_async_copy(wgu_hbm.at[layer, ex], wgu_f.at[s], wsem.at[s]),
                pltpu.make_async_copy(wdn_hbm.at[layer, ex], wdn_f.at[s], wsem.at[s]))

    @pl.when(first_ref[i] != 0)
    def _():
        s = wslot_ref[i]

        @pl.when(i == 0)
        def _():
            for cp in weight_copies(e, s):
                cp.start()

        for cp in weight_copies(e, s):
            cp.wait()
        wgu_bf[...] = wgu_f[s].astype(bf16)
        wdn_bf[...] = wdn_f[s].astype(bf16)

        @pl.when(nxt_ref[i] >= 0)
        def _():
            for cp in weight_copies(nxt_ref[i], 1 - s):
                cp.start()

    @pl.when(i < nused_ref[0])
    def _():
        x = xb_ref[...].reshape(MOE_BM, D).astype(bf16)
        h = jnp.dot(x, wgu_bf[...], preferred_element_type=f32) + bgu_ref[0, 0]
        gate = jnp.minimum(h[:, :D], SWIGLU_LIMIT)
        lin = jnp.clip(h[:, D:], -SWIGLU_LIMIT, SWIGLU_LIMIT)
        act = (lin + 1.0) * (gate * _sigmoid(SWIGLU_ALPHA * gate))
        y = jnp.dot(act.astype(bf16), wdn_bf[...], preferred_element_type=f32) + bdn_ref[0, 0]
        yb_ref[...] = y.reshape((MOE_BM,) + ROW3)

    @pl.when(i >= nused_ref[0])
    def _():
        yb_ref[...] = jnp.zeros_like(yb_ref)


def experts(block_e, nused, first, wslot, nxt, xb, p):
    l = p["layer"]
    grid_spec = pltpu.PrefetchScalarGridSpec(
        num_scalar_prefetch=5, grid=(MOE_NBLK,),
        in_specs=[pl.BlockSpec((MOE_BM,) + ROW3,
                               lambda i, be, nu, *_: (jnp.maximum(jnp.minimum(i, nu[0] - 1), 0), 0, 0)),
                  pl.BlockSpec(memory_space=pl.ANY),
                  pl.BlockSpec((1, 1, 1, 2 * D), lambda i, be, *_: (l, be[i], 0, 0)),
                  pl.BlockSpec(memory_space=pl.ANY),
                  pl.BlockSpec((1, 1, 1, D), lambda i, be, *_: (l, be[i], 0, 0))],
        out_specs=pl.BlockSpec((MOE_BM,) + ROW3, lambda i, *_: (i, 0, 0)),
        scratch_shapes=[pltpu.VMEM((2, D, 2 * D), f32), pltpu.VMEM((2, D, D), f32), pltpu.SemaphoreType.DMA((2,)),
                        pltpu.VMEM((D, 2 * D), bf16), pltpu.VMEM((D, D), bf16)])
    return pl.pallas_call(
        functools.partial(_expert_kernel, layer=l), out_shape=_sds((MOE_CAP,) + ROW3), grid_spec=grid_spec,
        compiler_params=_cparams(("arbitrary",)), name="moe_experts")(
            block_e, nused, first, wslot, nxt, xb, p["exp_w_gu"],
            p["exp_b_gu"].reshape(DEPTH, N_EXPERTS, 1, 2 * D), p["exp_w_down"],
            p["exp_b_down"].reshape(DEPTH, N_EXPERTS, 1, D))


def _combine_kernel(dest_ref, yb_ref, topw_ref, h_ref, mod_ref, o_ref, buf, sem):
    def start(t, c):
        for k in range(TOP_K):
            pltpu.make_async_copy(yb_ref.at[pl.ds(dest_ref[t * TOP_K + k], 1)], buf.at[k, pl.ds(t, 1)],
                                  sem).start(priority=k % 2)
        return c

    lax.fori_loop(0, TM, start, 0)

    def wait(t, c):
        for k in range(TOP_K):
            pltpu.make_async_copy(yb_ref.at[pl.ds(0, 1)], buf.at[k, pl.ds(0, 1)], sem).wait()
        return c

    lax.fori_loop(0, TM, wait, 0)
    w = topw_ref[...]
    y = jnp.zeros((TM, D), f32)
    for k in range(TOP_K):
        y = y + buf[k].reshape(TM, D) * w[:, k:k + 1]
    o_ref[...] = h_ref[...] + mod_ref[0][5:6, :] * y


def combine(dest_flat, yb, topw, h, mod_tiles):
    t = h.shape[0]
    tok = pl.BlockSpec((TM, D), lambda i: (i, 0))
    return pl.pallas_call(
        _combine_kernel, out_shape=_sds(h.shape), grid=(t // TM,),
        in_specs=[pl.BlockSpec((TM * TOP_K,), lambda i: (i,), memory_space=pltpu.SMEM),
                  pl.BlockSpec(memory_space=pl.ANY),
                  pl.BlockSpec((TM, LANES), lambda i: (i, 0)), tok,
                  pl.BlockSpec((1, 6, D), lambda i: (i, 0, 0))],
        out_specs=tok,
        scratch_shapes=[pltpu.VMEM((TOP_K, TM) + ROW3, f32), pltpu.SemaphoreType.DMA(())],
        compiler_params=_cparams(("arbitrary",)), name="moe_combine")(dest_flat, yb, topw, h, mod_tiles)


def moe_layer(h, mod_tiles, p):
    xn3, topi, topw, rank, counts = router(h, mod_tiles, p)
    cnt = counts[0, :N_EXPERTS]
    padded = (cnt + MOE_BM - 1) // MOE_BM * MOE_BM
    pend = jnp.cumsum(padded)
    pstart = jnp.zeros((1, LANES), i32).at[0, :N_EXPERTS].set(pend - padded)
    blk_row0 = jnp.arange(MOE_NBLK, dtype=i32) * MOE_BM
    block_e = jnp.minimum(jnp.sum((pend[None, :] <= blk_row0[:, None]).astype(i32), axis=1), N_EXPERTS - 1)
    nused = (pend[-1:] // MOE_BM).astype(i32)
    blk = jnp.arange(MOE_NBLK, dtype=i32)
    last_of_expert = jnp.any((padded > 0)[None, :] & (blk[:, None] == (pend // MOE_BM - 1)[None, :]), axis=1)
    zflag = (last_of_expert | (blk >= nused[0])).astype(i32)
    dest = dest_rows(topi, rank, pstart)
    dest_flat = dest[:, :TOP_K].reshape(-1)
    xb = dispatch(zflag, dest_flat, xn3)
    first = ((blk < nused[0]) & (block_e != jnp.concatenate([jnp.full((1,), -1, i32), block_e[:-1]]))).astype(i32)
    wslot = (jnp.maximum(jnp.cumsum(first) - 1, 0) % 2).astype(i32)
    eidx = jnp.arange(N_EXPERTS, dtype=i32)
    later = jnp.where((padded > 0)[None, :] & (eidx[None, :] > eidx[:, None]), eidx[None, :], N_EXPERTS)
    nxt_tab = jnp.min(later, axis=1)
    nxt = jnp.where(nxt_tab == N_EXPERTS, -1, nxt_tab)[block_e].astype(i32)
    yb = experts(block_e, nused, first, wslot, nxt, xb, p)
    return combine(dest_flat, yb, topw, h, mod_tiles)


def _block_diag(w):
    per = CW // LRU_BW
    w5 = w.reshape(2, D // CW, per, LRU_BW, LRU_BW)
    eye = jnp.eye(per, dtype=w.dtype)
    bd = w5[:, :, :, :, None, :] * eye[None, None, :, None, :, None]
    return bd.reshape(2, D // CW, CW, CW).astype(bf16)


def _layer_params(l, a):
    tile2 = lambda g: jnp.tile(g, 2).reshape(1, LANES)
    whole = ("exp_w_gu", "exp_b_gu", "exp_w_down", "exp_b_down", "w_in", "w_gate", "w_branch", "w_out")
    p = {k: v[l] for k, v in a.items() if k not in whole}
    p["layer"] = l
    for k in ("exp_w_gu", "exp_b_gu", "exp_w_down", "exp_b_down"):
        p[k] = a[k]
    p["w_in_bf"] = cast_bf16(a["w_in"], l)
    p["w_gate_bf"] = cast_bf16(a["w_gate"], l)
    p["w_branch_bf"] = cast_bf16(a["w_branch"].reshape(DEPTH, 4 * D, D), l).reshape(4, D, D)
    p["w_out_bf"] = cast_bf16(a["w_out"], l)
    p["pool_w_bf"] = cast_bf16(a["pool_w"].reshape(DEPTH, D, CW), l).reshape(4, CW, CW)
    p["lru_wr_bd"] = _block_diag(p["lru_w_r"])
    p["lru_wi_bd"] = _block_diag(p["lru_w_i"])
    p["q_norm_g2"] = tile2(p["q_norm_g"])
    p["k_norm_g2"] = tile2(p["k_norm_g"])
    p["router_w_pad"] = jnp.pad(p["router_w"], ((0, 0), (0, LANES - N_EXPERTS)))
    p["router_b_pad"] = jnp.pad(p["router_b"], (0, LANES - N_EXPERTS)).reshape(1, LANES)
    return p


def kernel(x_prompt, x_sample, cache_k, cache_v, state_lru, c, c_ctx, norm1_g, norm2_g, w_mod, b_mod, w_in,
           lru_conv_w, lru_conv_b, lru_w_r, lru_b_r, lru_w_i, lru_b_i, lru_lambda, conv_dw_w, conv_dw_b,
           conv_ln_g, conv_ln_b, q_norm_g, k_norm_g, attn_sink, pool_w, pool_scale, w_branch, w_gate, b_gate,
           w_out, router_w, router_b, exp_w_gu, exp_b_gu, exp_w_down, exp_b_down):
    weights = dict(norm1_g=norm1_g, norm2_g=norm2_g, w_in=w_in, lru_conv_w=lru_conv_w, lru_conv_b=lru_conv_b,
                   lru_w_r=lru_w_r, lru_b_r=lru_b_r, lru_w_i=lru_w_i, lru_b_i=lru_b_i, lru_lambda=lru_lambda,
                   conv_dw_w=conv_dw_w, conv_dw_b=conv_dw_b, conv_ln_g=conv_ln_g, conv_ln_b=conv_ln_b,
                   q_norm_g=q_norm_g, k_norm_g=k_norm_g, attn_sink=attn_sink, pool_w=pool_w, pool_scale=pool_scale,
                   w_branch=w_branch, w_gate=w_gate, b_gate=b_gate, w_out=w_out, router_w=router_w,
                   router_b=router_b, exp_w_gu=exp_w_gu, exp_b_gu=exp_b_gu, exp_w_down=exp_w_down,
                   exp_b_down=exp_b_down)
    kvw = N_KV * HEAD_DIM
    n_cond = 2 * SUBLANES
    cond = jnp.concatenate([c_ctx[None, :], c, jnp.zeros((n_cond - 1 - N_LAT_B, D), f32)], axis=0)
    mod = modulation(cond, w_mod, b_mod)
    tile_start = jnp.arange(T // TM) * TM
    tile_row = jnp.where(tile_start < T_CTX, 0, 1 + (tile_start - T_CTX) // N_LAT)
    h = jnp.concatenate([x_prompt.reshape(T_CTX, D), x_sample.reshape(T_LAT, D)], axis=0)
    ck = cache_k.reshape(N_LAT_B, DEPTH, PAST, kvw)
    cv = cache_v.reshape(N_LAT_B, DEPTH, PAST, kvw)
    rope_cos, rope_sin = rope_tables()
    zero_state = jnp.zeros((N_CTX_B, 2, D), f32)
    new_k, new_v, new_s = [], [], []
    for l in range(DEPTH):
        p = _layer_params(l, weights)
        mod_tiles = mod[l][tile_row].reshape(T // TM, 6, D)
        xn = norm1(h, p["norm1_g"], mod_tiles)
        u = in_proj(xn, p["w_in_bf"])
        lat_blk0 = T_CTX // N_LAT
        lru_c, st_c = lru_branch(u, p, zero_state, N_CTX_B, N_CTX, 0)
        lru_l, _ = lru_branch(u, p, state_lru[:, l], N_LAT_B, N_LAT, lat_blk0)
        conv = (conf_branch(u, p, N_CTX_B, N_CTX, 0), conf_branch(u, p, N_LAT_B, N_LAT, lat_blk0))
        pool = (pool_branch(u, p, N_CTX_B, N_CTX, 0), pool_branch(u, p, N_LAT_B, N_LAT, lat_blk0))
        att_c, k_l, v_l = ctx_attention(u, p)
        att_l = lat_attention(u, p, ck, cv, rope_cos, rope_sin)
        h = merge(h, xn, (lru_c, lru_l), conv, (att_c, att_l), pool, mod_tiles, p)
        h = moe_layer(h, mod_tiles, p)
        new_k.append(k_l.reshape(N_CTX_B, N_CTX, N_KV, HEAD_DIM))
        new_v.append(v_l.reshape(N_CTX_B, N_CTX, N_KV, HEAD_DIM))
        new_s.append(st_c)
    y_prompt = h[:T_CTX].reshape(N_CTX_B, N_CTX, D)
    y_sample = h[T_CTX:].reshape(N_LAT_B, N_LAT, D)
    return (y_prompt, y_sample, jnp.stack(new_k, axis=1), jnp.stack(new_v, axis=1), jnp.stack(new_s, axis=1))
```

```python
import functools

import jax
import jax.numpy as jnp
from jax import lax
from jax.experimental import pallas as pl
from jax.experimental.pallas import tpu as pltpu

f32 = jnp.float32
bf16 = jnp.bfloat16
i32 = jnp.int32

D = 1024
N_CTX_B, N_CTX = 16, 256
N_LAT_B, N_LAT = 8, 1024
T_CTX = N_CTX_B * N_CTX
T_LAT = N_LAT_B * N_LAT
T = T_CTX + T_LAT
DEPTH = 2
PAST = 512
GRID_W = 64
IN_W = 6656
COL_UX, COL_UY, COL_UC, COL_UQ, COL_UK, COL_UV, COL_UP = 0, 1024, 2048, 4096, 5120, 5376, 5632
LRU_BW = 64
LRU_CONV = 4
LRU_C = 8.0
CONV_K = 31
N_HEADS, N_KV, HEAD_DIM = 16, 4, 64
POOL_SIZES = (2, 4, 8, 16)
N_EXPERTS, TOP_K = 32, 4
SWIGLU_LIMIT, SWIGLU_ALPHA = 7.0, 1.702
EPS = 1e-6
NEG_INF = -1e30
ROPE_BASE = 10000.0
LOG2E = 1.4426950408889634

LANES = 128
SUBLANES = 8
VMEM_LIMIT = 56 * 1024 * 1024

CW = 256
RC = 128
TQ = 128
TM = 256
MOE_BM = 256
N_ASSIGN = T * TOP_K
MOE_NBLK = N_ASSIGN // MOE_BM + N_EXPERTS
MOE_CAP = MOE_NBLK * MOE_BM


def _sds(shape, dt=f32):
    return jax.ShapeDtypeStruct(shape, dt)


def _cparams(sem, vmem=VMEM_LIMIT):
    return pltpu.CompilerParams(dimension_semantics=sem, vmem_limit_bytes=vmem)


def _sigmoid(x):
    return 0.5 * jnp.tanh(0.5 * x) + 0.5


def _log1p(z):
    u = 1.0 + z
    d = u - 1.0
    return jnp.where(d == 0.0, z, jnp.log(u) * (z / jnp.where(d == 0.0, 1.0, d)))


def _cast_kernel(x_ref, o_ref):
    o_ref[...] = x_ref[...].astype(o_ref.dtype)


def cast_bf16(w, l):
    _, r, c = w.shape
    tr = 256
    return pl.pallas_call(
        _cast_kernel, out_shape=_sds((r, c), bf16), grid=(r // tr,),
        in_specs=[pl.BlockSpec((None, tr, c), lambda i: (l, i, 0))],
        out_specs=pl.BlockSpec((tr, c), lambda i: (i, 0)),
        compiler_params=_cparams(("parallel",)), name="cast_bf16")(w)


def _mod_kernel(c_ref, w_ref, b_ref, o_ref):
    x = c_ref[...]
    s = x * _sigmoid(x)
    o_ref[0] = jnp.dot(s.astype(bf16), w_ref[0].astype(bf16), preferred_element_type=f32) + b_ref[0]


def modulation(cond, w_mod, b_mod):
    r = cond.shape[0]
    tn = 1536
    return pl.pallas_call(
        _mod_kernel, out_shape=_sds((DEPTH, r, 6 * D)), grid=(DEPTH, 6 * D // tn),
        in_specs=[pl.BlockSpec((r, D), lambda l, j: (0, 0)),
                  pl.BlockSpec((1, D, tn), lambda l, j: (l, 0, j)),
                  pl.BlockSpec((1, 1, tn), lambda l, j: (l, 0, j))],
        out_specs=pl.BlockSpec((1, r, tn), lambda l, j: (l, 0, j)),
        compiler_params=_cparams(("parallel", "parallel")), name="modulation")(
            cond, w_mod, b_mod.reshape(DEPTH, 1, 6 * D))


def _rms_mod(x, g, shift, scale):
    ms = jnp.mean(x * x, axis=-1, keepdims=True)
    return (x * lax.rsqrt(ms + EPS) * g) * (1.0 + scale) + shift


def _norm1_kernel(h_ref, g_ref, mod_ref, o_ref):
    m = mod_ref[0]
    o_ref[...] = _rms_mod(h_ref[...], g_ref[...], m[0:1, :], m[1:2, :]).astype(bf16)


def norm1(h, g, mod_tiles):
    nt = h.shape[0] // TM
    return pl.pallas_call(
        _norm1_kernel, out_shape=_sds(h.shape, bf16), grid=(nt,),
        in_specs=[pl.BlockSpec((TM, D), lambda i: (i, 0)),
                  pl.BlockSpec((1, D), lambda i: (0, 0)),
                  pl.BlockSpec((1, 6, D), lambda i: (i, 0, 0))],
        out_specs=pl.BlockSpec((TM, D), lambda i: (i, 0)),
        compiler_params=_cparams(("parallel",)), name="norm1")(h, g.reshape(1, D), mod_tiles)


def _matmul_kernel(x_ref, w_ref, o_ref):
    o_ref[...] = jnp.dot(x_ref[...], w_ref[...], preferred_element_type=f32)


def in_proj(xn, w):
    t = xn.shape[0]
    tm, tn = 1024, 1664
    return pl.pallas_call(
        _matmul_kernel, out_shape=_sds((t, IN_W)), grid=(t // tm, IN_W // tn),
        in_specs=[pl.BlockSpec((tm, D), lambda i, j: (i, 0)),
                  pl.BlockSpec((D, tn), lambda i, j: (0, j))],
        out_specs=pl.BlockSpec((tm, tn), lambda i, j: (i, j)),
        compiler_params=_cparams(("parallel", "parallel")), name="in_proj")(xn, w)


def _gelu_tanh(x):
    return 0.5 * x * (1.0 + jnp.tanh(0.7978845608028654 * (x + 0.044715 * (x * x * x))))


def _lru_kernel(ux_ref, uy_ref, cw_ref, cb_ref, wr_ref, wi_ref, br_ref, bi_ref, lam_ref, h0_ref,
                out_ref, st_ref, xp_ref, a_ref, b_ref, hs_ref, *, n):
    pad = SUBLANES
    xp_ref[pl.ds(0, pad), :] = jnp.zeros((pad, CW), f32)
    xp_ref[pl.ds(n + pad, pad), :] = jnp.zeros((pad, CW), f32)
    xp_ref[pl.ds(pad, n), :] = ux_ref[...]
    left = LRU_CONV // 2
    lam = lam_ref[...]
    neg_c_sp = -LRU_C * (jnp.maximum(-lam, 0.0) + _log1p(jnp.exp(-jnp.abs(lam))))

    def coef_body(i, c):
        r0 = pl.multiple_of(i * RC, RC)
        w = xp_ref[pl.ds(r0, RC + 2 * pad), :]
        xc = jnp.zeros((RC, CW), f32) + cb_ref[...]
        for k in range(LRU_CONV):
            off = pad - left + k
            xc = xc + cw_ref[k:k + 1, :] * w[off:off + RC, :]
        xcb = xc.astype(bf16)
        for d in range(2):
            r = _sigmoid(jnp.dot(xcb, wr_ref[d, 0], preferred_element_type=f32) + br_ref[d:d + 1, :])
            g = _sigmoid(jnp.dot(xcb, wi_ref[d, 0], preferred_element_type=f32) + bi_ref[d:d + 1, :])
            log_a = r * neg_c_sp[d:d + 1, :]
            a = jnp.exp(log_a)
            th = jnp.tanh(log_a)
            one_m_a2 = (-2.0 * th) / (1.0 - th)
            a_ref[d, pl.ds(r0, RC), :] = a
            b_ref[d, pl.ds(r0, RC), :] = jnp.sqrt(one_m_a2) * (g * xc)
        return c

    lax.fori_loop(0, n // RC, coef_body, 0)

    row = lax.broadcasted_iota(i32, (SUBLANES, CW), 0)
    nchunk = n // SUBLANES

    def scan_body(j, carry):
        hf, hb = carry
        rf = pl.multiple_of(j * SUBLANES, SUBLANES)
        a = a_ref[0, pl.ds(rf, SUBLANES), :]
        b = b_ref[0, pl.ds(rf, SUBLANES), :]
        for sh in (1, 2, 4):
            a_s = jnp.where(row >= sh, pltpu.roll(a, sh, axis=0), 1.0)
            b_s = jnp.where(row >= sh, pltpu.roll(b, sh, axis=0), 0.0)
            b = a * b_s + b
            a = a * a_s
        h = a * hf + b
        hs_ref[0, pl.ds(rf, SUBLANES), :] = h
        hf = h[SUBLANES - 1:SUBLANES, :]
        rb = pl.multiple_of((nchunk - 1 - j) * SUBLANES, SUBLANES)
        a = a_ref[1, pl.ds(rb, SUBLANES), :]
        b = b_ref[1, pl.ds(rb, SUBLANES), :]
        for sh in (1, 2, 4):
            keep = row < SUBLANES - sh
            a_s = jnp.where(keep, pltpu.roll(a, SUBLANES - sh, axis=0), 1.0)
            b_s = jnp.where(keep, pltpu.roll(b, SUBLANES - sh, axis=0), 0.0)
            b = a * b_s + b
            a = a * a_s
        h = a * hb + b
        hs_ref[1, pl.ds(rb, SUBLANES), :] = h
        hb = h[0:1, :]
        return hf, hb

    h0 = h0_ref[0]
    hf, hb = lax.fori_loop(0, nchunk, scan_body, (h0[0:1, :], h0[1:2, :]), unroll=4)
    st_ref[0, 0:1, :] = hf
    st_ref[0, 1:2, :] = hb

    def out_body(i, c):
        r0 = pl.multiple_of(i * RC, RC)
        hsum = hs_ref[0, pl.ds(r0, RC), :] + hs_ref[1, pl.ds(r0, RC), :]
        out_ref[pl.ds(r0, RC), :] = hsum * _gelu_tanh(uy_ref[pl.ds(r0, RC), :])
        return c

    lax.fori_loop(0, n // RC, out_body, 0)


def lru_branch(u, p, h0, nb, n, row_blk0):
    nct = D // CW
    cu = COL_UY // CW
    kern = functools.partial(_lru_kernel, n=n)
    vec = lambda a: pl.BlockSpec((a, CW), lambda b, c: (0, c))
    call = pl.pallas_call(
        kern, out_shape=(_sds((nb * n, D)), _sds((nb, 2, D))), grid=(nb, nct),
        in_specs=[pl.BlockSpec((n, CW), lambda b, c: (row_blk0 + b, c)),
                  pl.BlockSpec((n, CW), lambda b, c: (row_blk0 + b, cu + c)),
                  vec(LRU_CONV), vec(1),
                  pl.BlockSpec((2, 1, CW, CW), lambda b, c: (0, c, 0, 0)),
                  pl.BlockSpec((2, 1, CW, CW), lambda b, c: (0, c, 0, 0)),
                  vec(2), vec(2), vec(2),
                  pl.BlockSpec((1, 2, CW), lambda b, c: (b, 0, c))],
        out_specs=(pl.BlockSpec((n, CW), lambda b, c: (b, c)),
                   pl.BlockSpec((1, 2, CW), lambda b, c: (b, 0, c))),
        scratch_shapes=[pltpu.VMEM((n + 2 * SUBLANES, CW), f32),
                        pltpu.VMEM((2, n, CW), f32), pltpu.VMEM((2, n, CW), f32), pltpu.VMEM((2, n, CW), f32)],
        compiler_params=_cparams(("parallel", "parallel")), name="lru_branch")
    return call(u, u, p["lru_conv_w"], p["lru_conv_b"].reshape(1, D), p["lru_wr_bd"], p["lru_wi_bd"],
                p["lru_b_r"], p["lru_b_i"], p["lru_lambda"], h0)


def _conf_kernel(ua_ref, ug_ref, w_ref, b_ref, o_ref, gp_ref, *, n):
    pad = 2 * SUBLANES
    left = CONV_K // 2
    gp_ref[pl.ds(0, pad), :] = jnp.zeros((pad, CW), f32)
    gp_ref[pl.ds(n + pad, pad), :] = jnp.zeros((pad, CW), f32)
    gp_ref[pl.ds(pad, n), :] = ua_ref[...] * _sigmoid(ug_ref[...])

    wn = RC + 2 * pad

    def body(i, c):
        r0 = pl.multiple_of(i * RC, RC)
        w = gp_ref[pl.ds(r0, wn), :]
        rolled = [w] + [pltpu.roll(w, wn - m, axis=0) for m in range(1, SUBLANES)]
        acc = jnp.zeros((RC, CW), f32) + b_ref[...]
        for k in range(CONV_K):
            off = pad - left + k
            q, m = off // SUBLANES, off % SUBLANES
            acc = acc + w_ref[k:k + 1, :] * rolled[m][q * SUBLANES:q * SUBLANES + RC, :]
        o_ref[pl.ds(r0, RC), :] = acc
        return c

    lax.fori_loop(0, n // RC, body, 0)


def conf_branch(u, p, nb, n, row_blk0):
    nct = D // CW
    ca, cg = COL_UC // CW, (COL_UC + D) // CW
    kern = functools.partial(_conf_kernel, n=n)
    call = pl.pallas_call(
        kern, out_shape=_sds((nb * n, D)), grid=(nb, nct),
        in_specs=[pl.BlockSpec((n, CW), lambda b, c: (row_blk0 + b, ca + c)),
                  pl.BlockSpec((n, CW), lambda b, c: (row_blk0 + b, cg + c)),
                  pl.BlockSpec((CONV_K, CW), lambda b, c: (0, c)),
                  pl.BlockSpec((1, CW), lambda b, c: (0, c))],
        out_specs=pl.BlockSpec((n, CW), lambda b, c: (b, c)),
        scratch_shapes=[pltpu.VMEM((n + 4 * SUBLANES, CW), f32)],
        compiler_params=_cparams(("parallel", "parallel")), name="conf_branch")
    return call(u, u, p["conv_dw_w"], p["conv_dw_b"].reshape(1, D))


def _pool_kernel(up_ref, w_ref, s_ref, o_ref, xp_ref, *, n):
    pad = SUBLANES
    gi = pl.program_id(1)
    half = jnp.left_shift(1, gi)
    xp_ref[pl.ds(0, pad), :] = jnp.zeros((pad, CW), f32)
    xp_ref[pl.ds(n + pad, pad), :] = jnp.zeros((pad, CW), f32)
    xp_ref[pl.ds(pad, n), :] = up_ref[...]
    wn = RC + 2 * pad

    def body(i, c):
        r0 = pl.multiple_of(i * RC, RC)
        w = xp_ref[pl.ds(r0, wn), :]
        s2 = w + pltpu.roll(w, 1, axis=0)
        s4 = pltpu.roll(s2, 1, axis=0) + pltpu.roll(s2, wn - 1, axis=0)
        s8 = pltpu.roll(s4, 2, axis=0) + pltpu.roll(s4, wn - 2, axis=0)
        s16 = pltpu.roll(s8, 4, axis=0) + pltpu.roll(s8, wn - 4, axis=0)
        s = jnp.where(gi == 0, s2, jnp.where(gi == 1, s4, jnp.where(gi == 2, s8, s16)))[pad:pad + RC, :]
        t = r0 + lax.broadcasted_iota(i32, (RC, CW), 0)
        cnt = (jnp.minimum(t + half, n) - jnp.maximum(t - half, 0)).astype(f32)
        pooled = s / cnt - w[pad:pad + RC, :]
        o_ref[pl.ds(r0, RC), :] = jnp.dot(pooled.astype(bf16), w_ref[0], preferred_element_type=f32) * s_ref[...]
        return c

    lax.fori_loop(0, n // RC, body, 0)


def pool_branch(u, p, nb, n, row_blk0):
    assert POOL_SIZES == (2, 4, 8, 16) and D // len(POOL_SIZES) == CW
    cp = COL_UP // CW
    kern = functools.partial(_pool_kernel, n=n)
    call = pl.pallas_call(
        kern, out_shape=_sds((nb * n, D)), grid=(nb, len(POOL_SIZES)),
        in_specs=[pl.BlockSpec((n, CW), lambda b, c: (row_blk0 + b, cp + c)),
                  pl.BlockSpec((1, CW, CW), lambda b, c: (c, 0, 0)),
                  pl.BlockSpec((1, CW), lambda b, c: (0, c))],
        out_specs=pl.BlockSpec((n, CW), lambda b, c: (b, c)),
        scratch_shapes=[pltpu.VMEM((n + 2 * SUBLANES, CW), f32)],
        compiler_params=_cparams(("parallel", "parallel")), name="pool_branch")
    return call(u, p["pool_w_bf"], p["pool_scale"].reshape(1, D))


def _head_norm(blk, g128, lane_lo):
    sq = blk * blk
    s_lo = jnp.sum(jnp.where(lane_lo, sq, 0.0), axis=-1, keepdims=True)
    s_hi = jnp.sum(jnp.where(lane_lo, 0.0, sq), axis=-1, keepdims=True)
    r = jnp.where(lane_lo, lax.rsqrt(s_lo * (1.0 / HEAD_DIM) + EPS), lax.rsqrt(s_hi * (1.0 / HEAD_DIM) + EPS))
    return blk * r * g128


def _rope(blk, cos, sin_signed, lane):
    partner = jnp.where((lane % 32) < 16, pltpu.roll(blk, LANES - 16, axis=1), pltpu.roll(blk, 16, axis=1))
    return blk * cos + partner * sin_signed


def _prep_kv(k_ref, v_ref, ks_ref, vs_ref, kg, lane_lo, rope=None, newk_ref=None, newv_ref=None):
    nk = k_ref.shape[0]
    lane = lax.broadcasted_iota(i32, (nk, LANES), 1)
    for tj in range(N_KV * HEAD_DIM // LANES):
        kt = _head_norm(k_ref[:, tj * LANES:(tj + 1) * LANES], kg, lane_lo(nk))
        if rope is not None:
            kt = _rope(kt, rope[0][...], rope[1][...], lane)
        vt = v_ref[:, tj * LANES:(tj + 1) * LANES]
        if newk_ref is not None:
            newk_ref[0, :, tj * LANES:(tj + 1) * LANES] = kt
            newv_ref[0, :, tj * LANES:(tj + 1) * LANES] = vt
        _store_low_half(kt, vt, ks_ref, vs_ref, tj, lane_lo(nk))


def _store_low_half(kt, vt, ks_ref, vs_ref, tj, lane_lo):
    ks_ref[2 * tj] = jnp.where(lane_lo, kt, 0.0).astype(bf16)
    vs_ref[2 * tj] = jnp.where(lane_lo, vt, 0.0).astype(bf16)
    ks_ref[2 * tj + 1] = jnp.where(lane_lo, pltpu.roll(kt, HEAD_DIM, axis=1), 0.0).astype(bf16)
    vs_ref[2 * tj + 1] = jnp.where(lane_lo, pltpu.roll(vt, HEAD_DIM, axis=1), 0.0).astype(bf16)


def _prep_cache(ck_ref, cv_ref, kc_ref, vc_ref, lane_lo):
    nk = ck_ref.shape[2]
    for tj in range(N_KV * HEAD_DIM // LANES):
        kt = ck_ref[0, 0, :, tj * LANES:(tj + 1) * LANES]
        vt = cv_ref[0, 0, :, tj * LANES:(tj + 1) * LANES]
        _store_low_half(kt, vt, kc_ref, vc_ref, tj, lane_lo(nk))


def _qk(qm, k):
    return lax.dot_general(qm, k, (((1,), (1,)), ((), ())), preferred_element_type=f32)


def _attend(sink_ref, q_tiles, segments, att_ref):
    per_kv = N_HEADS // N_KV
    assert per_kv == 4
    rows = per_kv * TQ
    row = lax.broadcasted_iota(i32, (rows, 1), 0)
    masks = [None if mk is None else jnp.concatenate([mk] * per_kv, axis=0) for _, _, mk in segments]
    for g in range(N_KV):
        t0, t1 = q_tiles[2 * g], q_tiles[2 * g + 1]
        qm = jnp.concatenate([t0, t1, pltpu.roll(t0, HEAD_DIM, axis=1), pltpu.roll(t1, HEAD_DIM, axis=1)],
                             axis=0).astype(bf16)
        hd = [per_kv * g, per_kv * g + 2, per_kv * g + 1, per_kv * g + 3]
        sink = jnp.where(row < TQ, sink_ref[hd[0]],
                         jnp.where(row < 2 * TQ, sink_ref[hd[1]],
                                   jnp.where(row < 3 * TQ, sink_ref[hd[2]], sink_ref[hd[3]]))) * LOG2E
        scores = []
        m_el = None
        for (kget, _, _), mask in zip(segments, masks):
            s = _qk(qm, kget(g))
            if mask is not None:
                s = jnp.where(mask, s, NEG_INF)
            for c in range(s.shape[1] // LANES):
                t = s[:, c * LANES:(c + 1) * LANES]
                m_el = t if m_el is None else jnp.maximum(m_el, t)
            scores.append(s)
        m = jnp.maximum(jnp.max(m_el, axis=-1, keepdims=True), sink)
        d_el = jnp.zeros((rows, LANES), f32)
        o = jnp.zeros((rows, LANES), f32)
        for s, (_, vget, _) in zip(scores, segments):
            pr = jnp.exp2(s - m)
            for c in range(s.shape[1] // LANES):
                d_el = d_el + pr[:, c * LANES:(c + 1) * LANES]
            o = o + jnp.dot(pr.astype(bf16), vget(g), preferred_element_type=f32)
        den = jnp.exp2(sink - m) + jnp.sum(d_el, axis=-1, keepdims=True)
        o = o / den
        att_ref[:, 2 * g * LANES:(2 * g + 1) * LANES] = o[:TQ] + pltpu.roll(o[2 * TQ:3 * TQ], HEAD_DIM, axis=1)
        att_ref[:, (2 * g + 1) * LANES:(2 * g + 2) * LANES] = o[TQ:2 * TQ] + pltpu.roll(o[3 * TQ:], HEAD_DIM, axis=1)


def _q_tiles(q_ref, qg, rope=None):
    lane = lax.broadcasted_iota(i32, (TQ, LANES), 1)
    lane_lo = lane < HEAD_DIM
    tiles = []
    for j in range(N_HEADS // 2):
        qt = _head_norm(q_ref[:, j * LANES:(j + 1) * LANES], qg, lane_lo)
        if rope is not None:
            qt = _rope(qt, rope[0], rope[1], lane)
        tiles.append(qt * (HEAD_DIM ** -0.5 * LOG2E))
    return tiles


def _lane_lo_fn(nrows):
    return lax.broadcasted_iota(i32, (nrows, LANES), 1) < HEAD_DIM


def _ctx_attn_kernel(sink_ref, q_ref, k_ref, v_ref, qg_ref, kg_ref, att_ref, newk_ref, newv_ref, ks_ref, vs_ref):
    @pl.when(pl.program_id(1) == 0)
    def _():
        _prep_kv(k_ref, v_ref, ks_ref, vs_ref, kg_ref[...], _lane_lo_fn, None, newk_ref, newv_ref)

    tiles = _q_tiles(q_ref, qg_ref[...])
    seg = [(lambda g: ks_ref[g], lambda g: vs_ref[g], None)]
    _attend(sink_ref, tiles, seg, att_ref)


def ctx_attention(u, p):
    nq = N_CTX // TQ
    kvw = N_KV * HEAD_DIM
    return pl.pallas_call(
        _ctx_attn_kernel,
        out_shape=(_sds((T_CTX, D)), _sds((N_CTX_B, N_CTX, kvw)), _sds((N_CTX_B, N_CTX, kvw))),
        grid=(N_CTX_B, nq),
        in_specs=[pl.BlockSpec(memory_space=pltpu.SMEM),
                  pl.BlockSpec((TQ, D), lambda b, i: (b * nq + i, COL_UQ // D)),
                  pl.BlockSpec((N_CTX, kvw), lambda b, i: (b, COL_UK // kvw)),
                  pl.BlockSpec((N_CTX, kvw), lambda b, i: (b, COL_UV // kvw)),
                  pl.BlockSpec((1, LANES), lambda b, i: (0, 0)),
                  pl.BlockSpec((1, LANES), lambda b, i: (0, 0))],
        out_specs=(pl.BlockSpec((TQ, D), lambda b, i: (b * nq + i, 0)),
                   pl.BlockSpec((1, N_CTX, kvw), lambda b, i: (b, 0, 0)),
                   pl.BlockSpec((1, N_CTX, kvw), lambda b, i: (b, 0, 0))),
        scratch_shapes=[pltpu.VMEM((N_KV, N_CTX, LANES), bf16), pltpu.VMEM((N_KV, N_CTX, LANES), bf16)],
        compiler_params=_cparams(("parallel", "arbitrary")), name="ctx_attention")(
            p["attn_sink"], u, u, u, p["q_norm_g2"], p["k_norm_g2"])


def _lat_attn_kernel(sink_ref, q_ref, k_ref, v_ref, ck_ref, cv_ref, cosq_ref, sinq_ref, cosk_ref, sink_tab_ref,
                     qg_ref, kg_ref, att_ref, ks_ref, vs_ref, kc_ref, vc_ref):
    i = pl.program_id(1)
    nq = N_LAT // TQ

    @pl.when(i == 0)
    def _():
        _prep_kv(k_ref, v_ref, ks_ref, vs_ref, kg_ref[...], _lane_lo_fn, (cosk_ref, sink_tab_ref))
        _prep_cache(ck_ref, cv_ref, kc_ref, vc_ref, _lane_lo_fn)

    tiles = _q_tiles(q_ref, qg_ref[...], (cosq_ref[...], sinq_ref[...]))
    r = lax.broadcasted_iota(i32, (TQ, TQ), 0)
    c = lax.broadcasted_iota(i32, (TQ, TQ), 1)
    prev0 = pl.multiple_of(jnp.maximum(i - 1, 0) * TQ, TQ)
    cur0 = pl.multiple_of(i * TQ, TQ)
    next0 = pl.multiple_of(jnp.minimum(i + 1, nq - 1) * TQ, TQ)
    mask_prev = jnp.logical_and(c >= r, i > 0)
    mask_next = jnp.logical_and(c <= r, i < nq - 1)

    def seg(r0, mask):
        return (lambda g: ks_ref[g, pl.ds(r0, TQ), :], lambda g: vs_ref[g, pl.ds(r0, TQ), :], mask)

    segments = [seg(prev0, mask_prev), seg(cur0, None), seg(next0, mask_next),
                (lambda g: kc_ref[g], lambda g: vc_ref[g], None)]
    _attend(sink_ref, tiles, segments, att_ref)


def lat_attention(u, p, cache_k, cache_v, rope_cos, rope_sin):
    nq = N_LAT // TQ
    kvw = N_KV * HEAD_DIM
    l = p["layer"]
    qblk0 = T_CTX // TQ
    kblk0 = T_CTX // N_LAT
    call = pl.pallas_call(
        _lat_attn_kernel, out_shape=_sds((T_LAT, D)), grid=(N_LAT_B, nq),
        in_specs=[pl.BlockSpec(memory_space=pltpu.SMEM),
                  pl.BlockSpec((TQ, D), lambda b, i: (qblk0 + b * nq + i, COL_UQ // D)),
                  pl.BlockSpec((N_LAT, kvw), lambda b, i: (kblk0 + b, COL_UK // kvw)),
                  pl.BlockSpec((N_LAT, kvw), lambda b, i: (kblk0 + b, COL_UV // kvw)),
                  pl.BlockSpec((1, 1, PAST, kvw), lambda b, i: (b, l, 0, 0)),
                  pl.BlockSpec((1, 1, PAST, kvw), lambda b, i: (b, l, 0, 0)),
                  pl.BlockSpec((TQ, LANES), lambda b, i: (i, 0)),
                  pl.BlockSpec((TQ, LANES), lambda b, i: (i, 0)),
                  pl.BlockSpec((N_LAT, LANES), lambda b, i: (0, 0)),
                  pl.BlockSpec((N_LAT, LANES), lambda b, i: (0, 0)),
                  pl.BlockSpec((1, LANES), lambda b, i: (0, 0)),
                  pl.BlockSpec((1, LANES), lambda b, i: (0, 0))],
        out_specs=pl.BlockSpec((TQ, D), lambda b, i: (b * nq + i, 0)),
        scratch_shapes=[pltpu.VMEM((N_KV, N_LAT, LANES), bf16), pltpu.VMEM((N_KV, N_LAT, LANES), bf16),
                        pltpu.VMEM((N_KV, PAST, LANES), bf16), pltpu.VMEM((N_KV, PAST, LANES), bf16)],
        compiler_params=_cparams(("parallel", "arbitrary")), name="lat_attention")
    return call(p["attn_sink"], u, u, u, cache_k, cache_v, rope_cos, rope_sin, rope_cos, rope_sin,
                p["q_norm_g2"], p["k_norm_g2"])


def rope_tables():
    pos = jnp.arange(N_LAT)
    rows = (pos // GRID_W).astype(f32)
    cols = (pos % GRID_W).astype(f32)
    half = HEAD_DIM // 2
    freqs = ROPE_BASE ** (-jnp.arange(0, half, 2, dtype=f32) / half)
    lane = jnp.arange(LANES)
    within = lane % half
    fidx = within % (half // 2)
    use_cols = (lane % HEAD_DIM) >= half
    ang = jnp.where(use_cols[None, :], cols[:, None], rows[:, None]) * freqs[fidx][None, :]
    sign = jnp.where(within < half // 2, -1.0, 1.0)
    return jnp.cos(ang), jnp.sin(ang) * sign[None, :]


def _merge_kernel(h_ref, xn_ref, lru_c, lru_l, conv_c, conv_l, att_c, att_l, pool_c, pool_l, mod_ref, lng_ref, lnb_ref,
                  wg_ref, bg_ref, wb_ref, wo_ref, o_ref):
    is_ctx = pl.program_id(0) < T_CTX // TM
    pick = lambda c_ref, l_ref: jnp.where(is_ctx, c_ref[...], l_ref[...])
    xn = xn_ref[...]
    hc = pick(conv_c, conv_l)
    mu = jnp.mean(hc, axis=-1, keepdims=True)
    xc = hc - mu
    var = jnp.mean(xc * xc, axis=-1, keepdims=True)
    y = xc * lax.rsqrt(var + EPS) * lng_ref[...] + lnb_ref[...]
    conv = y * _sigmoid(y)
    merged = jnp.zeros((TM, D), f32)
    for j, br in enumerate((pick(lru_c, lru_l), conv, pick(att_c, att_l), pick(pool_c, pool_l))):
        gate = _sigmoid(jnp.dot(xn, wg_ref[:, j * D:(j + 1) * D], preferred_element_type=f32)
                        + bg_ref[:, j * D:(j + 1) * D])
        proj = jnp.dot(br.astype(bf16), wb_ref[j], preferred_element_type=f32)
        merged = merged + gate * proj
    out = jnp.dot(merged.astype(bf16), wo_ref[...], preferred_element_type=f32)
    o_ref[...] = h_ref[...] + mod_ref[0][2:3, :] * out


def merge(h, xn, lru, conv, att, pool, mod_tiles, p):
    nt = h.shape[0] // TM
    n_ctx_tiles = T_CTX // TM
    tok = pl.BlockSpec((TM, D), lambda i: (i, 0))
    tok_c = pl.BlockSpec((TM, D), lambda i: (jnp.minimum(i, n_ctx_tiles - 1), 0))
    tok_l = pl.BlockSpec((TM, D), lambda i: (jnp.maximum(i - n_ctx_tiles, 0), 0))
    const = lambda shape: pl.BlockSpec(shape, lambda i: (0,) * len(shape), pipeline_mode=pl.Buffered(1))
    return pl.pallas_call(
        _merge_kernel, out_shape=_sds(h.shape), grid=(nt,),
        in_specs=[tok, tok, tok_c, tok_l, tok_c, tok_l, tok_c, tok_l, tok_c, tok_l,
                  pl.BlockSpec((1, 6, D), lambda i: (i, 0, 0)),
                  const((1, D)), const((1, D)),
                  const((D, 4 * D)), const((1, 4 * D)), const((4, D, D)), const((D, D))],
        out_specs=tok,
        compiler_params=_cparams(("parallel",)), name="merge")(
            h, xn, *lru, *conv, *att, *pool, mod_tiles, p["conv_ln_g"].reshape(1, D), p["conv_ln_b"].reshape(1, D),
            p["w_gate_bf"], p["b_gate"].reshape(1, 4 * D), p["w_branch_bf"], p["w_out_bf"])


def _router_kernel(h_ref, g_ref, mod_ref, rw_ref, rb_ref, xn_ref, topi_ref, topw_ref, rank_ref, cnt_ref, carry_ref):
    step = pl.program_id(0)

    @pl.when(step == 0)
    def _():
        carry_ref[...] = jnp.zeros_like(carry_ref)

    m = mod_ref[0]
    xn = _rms_mod(h_ref[...], g_ref[...], m[3:4, :], m[4:5, :])
    xn_ref[...] = xn.reshape(xn_ref.shape)
    logits = jnp.dot(xn, rw_ref[...], preferred_element_type=f32, precision=lax.Precision.HIGHEST) + rb_ref[...]
    lane = lax.broadcasted_iota(i32, (TM, LANES), 1).astype(f32)
    lg = jnp.where(lane < N_EXPERTS, logits, NEG_INF)
    vals, idxs = [], []
    for _ in range(TOP_K):
        mx = jnp.max(lg, axis=-1, keepdims=True)
        idx = jnp.min(jnp.where(lg == mx, lane, float(LANES)), axis=-1, keepdims=True)
        vals.append(mx)
        idxs.append(idx)
        lg = jnp.where(lane == idx, -3e38, lg)
    exps = [jnp.exp(v - vals[0]) for v in vals]
    den = exps[0] + exps[1] + exps[2] + exps[3]
    cnt = jnp.zeros((TM, LANES), f32)
    for idx in idxs:
        cnt = cnt + jnp.where(lane == idx, 1.0, 0.0)
    rr = lax.broadcasted_iota(i32, (TM, TM), 0)
    cc = lax.broadcasted_iota(i32, (TM, TM), 1)
    tri = jnp.where(rr > cc, 1.0, 0.0).astype(bf16)
    before = jnp.dot(tri, cnt.astype(bf16), preferred_element_type=f32) + carry_ref[0:1, :]
    topi = jnp.zeros((TM, LANES), f32)
    topw = jnp.zeros((TM, LANES), f32)
    rank = jnp.zeros((TM, LANES), f32)
    for k in range(TOP_K):
        rk = jnp.sum(jnp.where(lane == idxs[k], before, 0.0), axis=-1, keepdims=True)
        topi = jnp.where(lane == k, idxs[k], topi)
        topw = jnp.where(lane == k, exps[k] / den, topw)
        rank = jnp.where(lane == k, rk, rank)
    topi_ref[...] = topi.astype(i32)
    topw_ref[...] = topw
    rank_ref[...] = rank.astype(i32)
    total = carry_ref[0:1, :] + jnp.sum(cnt, axis=0, keepdims=True)
    carry_ref[...] = jnp.broadcast_to(total, carry_ref.shape)
    cnt_ref[...] = jnp.broadcast_to(total, cnt_ref.shape).astype(i32)


def router(h, mod_tiles, p):
    nt = h.shape[0] // TM
    tok = lambda w, dt=f32: pl.BlockSpec((TM, w), lambda i: (i, 0))
    t = h.shape[0]
    return pl.pallas_call(
        _router_kernel,
        out_shape=(_sds((t, D // LANES, LANES)), _sds((t, LANES), i32), _sds((t, LANES)), _sds((t, LANES), i32),
                   _sds((SUBLANES, LANES), i32)),
        grid=(nt,),
        in_specs=[tok(D), pl.BlockSpec((1, D), lambda i: (0, 0)), pl.BlockSpec((1, 6, D), lambda i: (i, 0, 0)),
                  pl.BlockSpec((D, LANES), lambda i: (0, 0)), pl.BlockSpec((1, LANES), lambda i: (0, 0))],
        out_specs=(pl.BlockSpec((TM, D // LANES, LANES), lambda i: (i, 0, 0)), tok(LANES), tok(LANES), tok(LANES),
                   pl.BlockSpec((SUBLANES, LANES), lambda i: (0, 0))),
        scratch_shapes=[pltpu.VMEM((SUBLANES, LANES), f32)],
        compiler_params=_cparams(("arbitrary",)), name="router")(
            h, p["norm2_g"].reshape(1, D), mod_tiles, p["router_w_pad"], p["router_b_pad"])


def _dest_kernel(topi_ref, rank_ref, pstart_ref, o_ref):
    lane = lax.broadcasted_iota(i32, (TM, LANES), 1).astype(f32)
    topi = topi_ref[...].astype(f32)
    pstart = pstart_ref[...].astype(f32)
    dest = rank_ref[...].astype(f32)
    for k in range(TOP_K):
        e = jnp.sum(jnp.where(lane == k, topi, 0.0), axis=-1, keepdims=True)
        ps = jnp.sum(jnp.where(lane == e, pstart, 0.0), axis=-1, keepdims=True)
        dest = dest + jnp.where(lane == k, ps, 0.0)
    o_ref[...] = dest.astype(i32)


def dest_rows(topi, rank, pstart):
    t = topi.shape[0]
    tok = pl.BlockSpec((TM, LANES), lambda i: (i, 0))
    return pl.pallas_call(
        _dest_kernel, out_shape=_sds((t, LANES), i32), grid=(t // TM,),
        in_specs=[tok, tok, pl.BlockSpec((1, LANES), lambda i: (0, 0))], out_specs=tok,
        compiler_params=_cparams(("parallel",)), name="dest_rows")(topi, rank, pstart)


ROW3 = (D // LANES, LANES)
assert ROW3[0] == SUBLANES


def _row_copy(src_ref, s, dst_ref, d, sem):
    return pltpu.make_async_copy(src_ref.at[pl.ds(s, 1)], dst_ref.at[pl.ds(d, 1)], sem)


def _dispatch_kernel(zflag_ref, dest_ref, x_ref, xb_ref, zero_ref, sem):
    @pl.when(pl.program_id(0) == 0)
    def _():
        zero_ref[...] = jnp.zeros_like(zero_ref)

        def zcopy(j):
            r0 = pl.multiple_of(j * MOE_BM, MOE_BM)
            return pltpu.make_async_copy(zero_ref, xb_ref.at[pl.ds(r0, MOE_BM)], sem)

        def zstart(j, c):
            @pl.when(zflag_ref[j] != 0)
            def _():
                zcopy(j).start()
            return c

        def zwait(j, c):
            @pl.when(zflag_ref[j] != 0)
            def _():
                zcopy(j).wait()
            return c

        lax.fori_loop(0, MOE_NBLK, zstart, 0)
        lax.fori_loop(0, MOE_NBLK, zwait, 0)

    def start(t, c):
        for k in range(TOP_K):
            _row_copy(x_ref, t, xb_ref, dest_ref[t * TOP_K + k], sem).start(priority=k % 2)
        return c

    lax.fori_loop(0, TM, start, 0)

    def wait(t, c):
        for k in range(TOP_K):
            _row_copy(x_ref, 0, xb_ref, 0, sem).wait()
        return c

    lax.fori_loop(0, TM, wait, 0)


def dispatch(zflag, dest_flat, xn3):
    t = xn3.shape[0]
    grid_spec = pltpu.PrefetchScalarGridSpec(
        num_scalar_prefetch=1, grid=(t // TM,),
        in_specs=[pl.BlockSpec((TM * TOP_K,), lambda i, zb: (i,), memory_space=pltpu.SMEM),
                  pl.BlockSpec((TM,) + ROW3, lambda i, zb: (i, 0, 0))],
        out_specs=pl.BlockSpec(memory_space=pl.ANY),
        scratch_shapes=[pltpu.VMEM((MOE_BM,) + ROW3, f32), pltpu.SemaphoreType.DMA(())])
    return pl.pallas_call(
        _dispatch_kernel, out_shape=_sds((MOE_CAP,) + ROW3), grid_spec=grid_spec,
        compiler_params=_cparams(("arbitrary",)), name="moe_dispatch")(zflag, dest_flat, xn3)


def _expert_kernel(be_ref, nused_ref, first_ref, wslot_ref, nxt_ref, xb_ref, wgu_hbm, bgu_ref, wdn_hbm, bdn_ref,
                   yb_ref, wgu_f, wdn_f, wsem, wgu_bf, wdn_bf, *, layer):
    i = pl.program_id(0)
    e = be_ref[i]

    def weight_copies(ex, s):
        return (pltpu.make_async_copy(wgu_hbm.at[layer, ex], wgu_f.at[s], wsem.at[s]),
                pltpu.make_async_copy(wdn_hbm.at[layer, ex], wdn_f.at[s], wsem.at[s]))

    @pl.when(first_ref[i] != 0)
    def _():
        s = wslot_ref[i]

        @pl.when(i == 0)
        def _():
            for cp in weight_copies(e, s):
                cp.start()

        for cp in weight_copies(e, s):
            cp.wait()
        wgu_bf[...] = wgu_f[s].astype(bf16)
        wdn_bf[...] = wdn_f[s].astype(bf16)

        @pl.when(nxt_ref[i] >= 0)
        def _():
            for cp in weight_copies(nxt_ref[i], 1 - s):
                cp.start()

    @pl.when(i < nused_ref[0])
    def _():
        x = xb_ref[...].reshape(MOE_BM, D).astype(bf16)
        h = jnp.dot(x, wgu_bf[...], preferred_element_type=f32) + bgu_ref[0, 0]
        gate = jnp.minimum(h[:, :D], SWIGLU_LIMIT)
        lin = jnp.clip(h[:, D:], -SWIGLU_LIMIT, SWIGLU_LIMIT)
        act = (lin + 1.0) * (gate * _sigmoid(SWIGLU_ALPHA * gate))
        y = jnp.dot(act.astype(bf16), wdn_bf[...], preferred_element_type=f32) + bdn_ref[0, 0]
        yb_ref[...] = y.reshape((MOE_BM,) + ROW3)

    @pl.when(i >= nused_ref[0])
    def _():
        yb_ref[...] = jnp.zeros_like(yb_ref)


def experts(block_e, nused, first, wslot, nxt, xb, p):
    l = p["layer"]
    grid_spec = pltpu.PrefetchScalarGridSpec(
        num_scalar_prefetch=5, grid=(MOE_NBLK,),
        in_specs=[pl.BlockSpec((MOE_BM,) + ROW3,
                               lambda i, be, nu, *_: (jnp.maximum(jnp.minimum(i, nu[0] - 1), 0), 0, 0)),
                  pl.BlockSpec(memory_space=pl.ANY),
                  pl.BlockSpec((1, 1, 1, 2 * D), lambda i, be, *_: (l, be[i], 0, 0)),
                  pl.BlockSpec(memory_space=pl.ANY),
                  pl.BlockSpec((1, 1, 1, D), lambda i, be, *_: (l, be[i], 0, 0))],
        out_specs=pl.BlockSpec((MOE_BM,) + ROW3, lambda i, *_: (i, 0, 0)),
        scratch_shapes=[pltpu.VMEM((2, D, 2 * D), f32), pltpu.VMEM((2, D, D), f32), pltpu.SemaphoreType.DMA((2,)),
                        pltpu.VMEM((D, 2 * D), bf16), pltpu.VMEM((D, D), bf16)])
    return pl.pallas_call(
        functools.partial(_expert_kernel, layer=l), out_shape=_sds((MOE_CAP,) + ROW3), grid_spec=grid_spec,
        compiler_params=_cparams(("arbitrary",)), name="moe_experts")(
            block_e, nused, first, wslot, nxt, xb, p["exp_w_gu"],
            p["exp_b_gu"].reshape(DEPTH, N_EXPERTS, 1, 2 * D), p["exp_w_down"],
            p["exp_b_down"].reshape(DEPTH, N_EXPERTS, 1, D))


def _combine_kernel(dest_ref, dest_nxt_ref, yb_ref, topw_ref, h_ref, mod_ref, o_ref, buf, sem):
    i = pl.program_id(0)
    slot = lax.rem(i, 2)

    def issue(d_ref, s):
        def start(t, c):
            for k in range(TOP_K):
                pltpu.make_async_copy(yb_ref.at[pl.ds(d_ref[t * TOP_K + k], 1)], buf.at[s, pl.ds(k * TM + t, 1)],
                                      sem.at[s]).start(priority=k % 2)
            return c

        lax.fori_loop(0, TM, start, 0)

    @pl.when(i == 0)
    def _():
        issue(dest_ref, 0)

    @pl.when(i + 1 < pl.num_programs(0))
    def _():
        issue(dest_nxt_ref, 1 - slot)

    pltpu.make_async_copy(yb_ref.at[pl.ds(0, TOP_K * TM)], buf.at[slot], sem.at[slot]).wait()
    w = topw_ref[...]
    y = jnp.zeros((TM, D), f32)
    for k in range(TOP_K):
        y = y + buf[slot, pl.ds(k * TM, TM)].reshape(TM, D) * w[:, k:k + 1]
    o_ref[...] = h_ref[...] + mod_ref[0][5:6, :] * y


def combine(dest_flat, yb, topw, h, mod_tiles):
    t = h.shape[0]
    nt = t // TM
    tok = pl.BlockSpec((TM, D), lambda i: (i, 0))
    return pl.pallas_call(
        _combine_kernel, out_shape=_sds(h.shape), grid=(nt,),
        in_specs=[pl.BlockSpec((TM * TOP_K,), lambda i: (i,), memory_space=pltpu.SMEM),
                  pl.BlockSpec((TM * TOP_K,), lambda i: (jnp.minimum(i + 1, nt - 1),), memory_space=pltpu.SMEM),
                  pl.BlockSpec(memory_space=pl.ANY),
                  pl.BlockSpec((TM, LANES), lambda i: (i, 0)), tok,
                  pl.BlockSpec((1, 6, D), lambda i: (i, 0, 0))],
        out_specs=tok,
        scratch_shapes=[pltpu.VMEM((2, TOP_K * TM) + ROW3, f32), pltpu.SemaphoreType.DMA((2,))],
        compiler_params=_cparams(("arbitrary",)), name="moe_combine")(
            dest_flat, dest_flat, yb, topw, h, mod_tiles)


def moe_layer(h, mod_tiles, p):
    xn3, topi, topw, rank, counts = router(h, mod_tiles, p)
    cnt = counts[0, :N_EXPERTS]
    padded = (cnt + MOE_BM - 1) // MOE_BM * MOE_BM
    pend = jnp.cumsum(padded)
    pstart = jnp.zeros((1, LANES), i32).at[0, :N_EXPERTS].set(pend - padded)
    blk_row0 = jnp.arange(MOE_NBLK, dtype=i32) * MOE_BM
    block_e = jnp.minimum(jnp.sum((pend[None, :] <= blk_row0[:, None]).astype(i32), axis=1), N_EXPERTS - 1)
    nused = (pend[-1:] // MOE_BM).astype(i32)
    blk = jnp.arange(MOE_NBLK, dtype=i32)
    last_of_expert = jnp.any((padded > 0)[None, :] & (blk[:, None] == (pend // MOE_BM - 1)[None, :]), axis=1)
    zflag = (last_of_expert | (blk >= nused[0])).astype(i32)
    dest = dest_rows(topi, rank, pstart)
    dest_flat = dest[:, :TOP_K].reshape(-1)
    xb = dispatch(zflag, dest_flat, xn3)
    first = ((blk < nused[0]) & (block_e != jnp.concatenate([jnp.full((1,), -1, i32), block_e[:-1]]))).astype(i32)
    wslot = (jnp.maximum(jnp.cumsum(first) - 1, 0) % 2).astype(i32)
    eidx = jnp.arange(N_EXPERTS, dtype=i32)
    later = jnp.where((padded > 0)[None, :] & (eidx[None, :] > eidx[:, None]), eidx[None, :], N_EXPERTS)
    nxt_tab = jnp.min(later, axis=1)
    nxt = jnp.where(nxt_tab == N_EXPERTS, -1, nxt_tab)[block_e].astype(i32)
    yb = experts(block_e, nused, first, wslot, nxt, xb, p)
    return combine(dest_flat, yb, topw, h, mod_tiles)


def _block_diag(w):
    per = CW // LRU_BW
    w5 = w.reshape(2, D // CW, per, LRU_BW, LRU_BW)
    eye = jnp.eye(per, dtype=w.dtype)
    bd = w5[:, :, :, :, None, :] * eye[None, None, :, None, :, None]
    return bd.reshape(2, D // CW, CW, CW).astype(bf16)


def _layer_params(l, a):
    tile2 = lambda g: jnp.tile(g, 2).reshape(1, LANES)
    whole = ("exp_w_gu", "exp_b_gu", "exp_w_down", "exp_b_down", "w_in", "w_gate", "w_branch", "w_out")
    p = {k: v[l] for k, v in a.items() if k not in whole}
    p["layer"] = l
    for k in ("exp_w_gu", "exp_b_gu", "exp_w_down", "exp_b_down"):
        p[k] = a[k]
    p["w_in_bf"] = cast_bf16(a["w_in"], l)
    p["w_gate_bf"] = cast_bf16(a["w_gate"], l)
    p["w_branch_bf"] = cast_bf16(a["w_branch"].reshape(DEPTH, 4 * D, D), l).reshape(4, D, D)
    p["w_out_bf"] = cast_bf16(a["w_out"], l)
    p["pool_w_bf"] = cast_bf16(a["pool_w"].reshape(DEPTH, D, CW), l).reshape(4, CW, CW)
    p["lru_wr_bd"] = _block_diag(p["lru_w_r"])
    p["lru_wi_bd"] = _block_diag(p["lru_w_i"])
    p["q_norm_g2"] = tile2(p["q_norm_g"])
    p["k_norm_g2"] = tile2(p["k_norm_g"])
    p["router_w_pad"] = jnp.pad(p["router_w"], ((0, 0), (0, LANES - N_EXPERTS)))
    p["router_b_pad"] = jnp.pad(p["router_b"], (0, LANES - N_EXPERTS)).reshape(1, LANES)
    return p


def kernel(x_prompt, x_sample, cache_k, cache_v, state_lru, c, c_ctx, norm1_g, norm2_g, w_mod, b_mod, w_in,
           lru_conv_w, lru_conv_b, lru_w_r, lru_b_r, lru_w_i, lru_b_i, lru_lambda, conv_dw_w, conv_dw_b,
           conv_ln_g, conv_ln_b, q_norm_g, k_norm_g, attn_sink, pool_w, pool_scale, w_branch, w_gate, b_gate,
           w_out, router_w, router_b, exp_w_gu, exp_b_gu, exp_w_down, exp_b_down):
    weights = dict(norm1_g=norm1_g, norm2_g=norm2_g, w_in=w_in, lru_conv_w=lru_conv_w, lru_conv_b=lru_conv_b,
                   lru_w_r=lru_w_r, lru_b_r=lru_b_r, lru_w_i=lru_w_i, lru_b_i=lru_b_i, lru_lambda=lru_lambda,
                   conv_dw_w=conv_dw_w, conv_dw_b=conv_dw_b, conv_ln_g=conv_ln_g, conv_ln_b=conv_ln_b,
                   q_norm_g=q_norm_g, k_norm_g=k_norm_g, attn_sink=attn_sink, pool_w=pool_w, pool_scale=pool_scale,
                   w_branch=w_branch, w_gate=w_gate, b_gate=b_gate, w_out=w_out, router_w=router_w,
                   router_b=router_b, exp_w_gu=exp_w_gu, exp_b_gu=exp_b_gu, exp_w_down=exp_w_down,
                   exp_b_down=exp_b_down)
    kvw = N_KV * HEAD_DIM
    n_cond = 2 * SUBLANES
    cond = jnp.concatenate([c_ctx[None, :], c, jnp.zeros((n_cond - 1 - N_LAT_B, D), f32)], axis=0)
    mod = modulation(cond, w_mod, b_mod)
    tile_start = jnp.arange(T // TM) * TM
    tile_row = jnp.where(tile_start < T_CTX, 0, 1 + (tile_start - T_CTX) // N_LAT)
    h = jnp.concatenate([x_prompt.reshape(T_CTX, D), x_sample.reshape(T_LAT, D)], axis=0)
    ck = cache_k.reshape(N_LAT_B, DEPTH, PAST, kvw)
    cv = cache_v.reshape(N_LAT_B, DEPTH, PAST, kvw)
    rope_cos, rope_sin = rope_tables()
    zero_state = jnp.zeros((N_CTX_B, 2, D), f32)
    new_k, new_v, new_s = [], [], []
    for l in range(DEPTH):
        p = _layer_params(l, weights)
        mod_tiles = mod[l][tile_row].reshape(T // TM, 6, D)
        xn = norm1(h, p["norm1_g"], mod_tiles)
        u = in_proj(xn, p["w_in_bf"])
        lat_blk0 = T_CTX // N_LAT
        lru_c, st_c = lru_branch(u, p, zero_state, N_CTX_B, N_CTX, 0)
        lru_l, _ = lru_branch(u, p, state_lru[:, l], N_LAT_B, N_LAT, lat_blk0)
        conv = (conf_branch(u, p, N_CTX_B, N_CTX, 0), conf_branch(u, p, N_LAT_B, N_LAT, lat_blk0))
        pool = (pool_branch(u, p, N_CTX_B, N_CTX, 0), pool_branch(u, p, N_LAT_B, N_LAT, lat_blk0))
        att_c, k_l, v_l = ctx_attention(u, p)
        att_l = lat_attention(u, p, ck, cv, rope_cos, rope_sin)
        h = merge(h, xn, (lru_c, lru_l), conv, (att_c, att_l), pool, mod_tiles, p)
        h = moe_layer(h, mod_tiles, p)
        new_k.append(k_l.reshape(N_CTX_B, N_CTX, N_KV, HEAD_DIM))
        new_v.append(v_l.reshape(N_CTX_B, N_CTX, N_KV, HEAD_DIM))
        new_s.append(st_c)
    y_prompt = h[:T_CTX].reshape(N_CTX_B, N_CTX, D)
    y_sample = h[T_CTX:].reshape(N_LAT_B, N_LAT, D)
    return (y_prompt, y_sample, jnp.stack(new_k, axis=1), jnp.stack(new_v, axis=1), jnp.stack(new_s, axis=1))
```

```python
import functools

import jax
import jax.numpy as jnp
from jax import lax
from jax.experimental import pallas as pl
from jax.experimental.pallas import tpu as pltpu

f32 = jnp.float32
bf16 = jnp.bfloat16
i32 = jnp.int32

D = 1024
N_CTX_B, N_CTX = 16, 256
N_LAT_B, N_LAT = 8, 1024
T_CTX = N_CTX_B * N_CTX
T_LAT = N_LAT_B * N_LAT
T = T_CTX + T_LAT
DEPTH = 2
PAST = 512
GRID_W = 64
IN_W = 6656
COL_UX, COL_UY, COL_UC, COL_UQ, COL_UK, COL_UV, COL_UP = 0, 1024, 2048, 4096, 5120, 5376, 5632
LRU_BW = 64
LRU_CONV = 4
LRU_C = 8.0
CONV_K = 31
N_HEADS, N_KV, HEAD_DIM = 16, 4, 64
POOL_SIZES = (2, 4, 8, 16)
N_EXPERTS, TOP_K = 32, 4
SWIGLU_LIMIT, SWIGLU_ALPHA = 7.0, 1.702
EPS = 1e-6
NEG_INF = -1e30
ROPE_BASE = 10000.0
LOG2E = 1.4426950408889634

LANES = 128
SUBLANES = 8
VMEM_LIMIT = 56 * 1024 * 1024

CW = 256
RC = 128
TQ = 128
TM = 256
MOE_BM = 256
N_ASSIGN = T * TOP_K
MOE_NBLK = N_ASSIGN // MOE_BM + N_EXPERTS
MOE_CAP = MOE_NBLK * MOE_BM


def _sds(shape, dt=f32):
    return jax.ShapeDtypeStruct(shape, dt)


def _cparams(sem, vmem=VMEM_LIMIT):
    return pltpu.CompilerParams(dimension_semantics=sem, vmem_limit_bytes=vmem)


def _sigmoid(x):
    return 0.5 * jnp.tanh(0.5 * x) + 0.5


def _log1p(z):
    u = 1.0 + z
    d = u - 1.0
    return jnp.where(d == 0.0, z, jnp.log(u) * (z / jnp.where(d == 0.0, 1.0, d)))


def _cast_kernel(x_ref, o_ref):
    o_ref[...] = x_ref[...].astype(o_ref.dtype)


def cast_bf16(w, l):
    _, r, c = w.shape
    tr = 256
    return pl.pallas_call(
        _cast_kernel, out_shape=_sds((r, c), bf16), grid=(r // tr,),
        in_specs=[pl.BlockSpec((None, tr, c), lambda i: (l, i, 0))],
        out_specs=pl.BlockSpec((tr, c), lambda i: (i, 0)),
        compiler_params=_cparams(("parallel",)), name="cast_bf16")(w)


def _mod_kernel(c_ref, w_ref, b_ref, o_ref):
    x = c_ref[...]
    s = x * _sigmoid(x)
    o_ref[0] = jnp.dot(s.astype(bf16), w_ref[0].astype(bf16), preferred_element_type=f32) + b_ref[0]


def modulation(cond, w_mod, b_mod):
    r = cond.shape[0]
    tn = 1536
    return pl.pallas_call(
        _mod_kernel, out_shape=_sds((DEPTH, r, 6 * D)), grid=(DEPTH, 6 * D // tn),
        in_specs=[pl.BlockSpec((r, D), lambda l, j: (0, 0)),
                  pl.BlockSpec((1, D, tn), lambda l, j: (l, 0, j)),
                  pl.BlockSpec((1, 1, tn), lambda l, j: (l, 0, j))],
        out_specs=pl.BlockSpec((1, r, tn), lambda l, j: (l, 0, j)),
        compiler_params=_cparams(("parallel", "parallel")), name="modulation")(
            cond, w_mod, b_mod.reshape(DEPTH, 1, 6 * D))


def _rms_mod(x, g, shift, scale):
    ms = jnp.mean(x * x, axis=-1, keepdims=True)
    return (x * lax.rsqrt(ms + EPS) * g) * (1.0 + scale) + shift


def _norm1_kernel(h_ref, g_ref, mod_ref, o_ref):
    m = mod_ref[0]
    o_ref[...] = _rms_mod(h_ref[...], g_ref[...], m[0:1, :], m[1:2, :]).astype(bf16)


def norm1(h, g, mod_tiles):
    nt = h.shape[0] // TM
    return pl.pallas_call(
        _norm1_kernel, out_shape=_sds(h.shape, bf16), grid=(nt,),
        in_specs=[pl.BlockSpec((TM, D), lambda i: (i, 0)),
                  pl.BlockSpec((1, D), lambda i: (0, 0)),
                  pl.BlockSpec((1, 6, D), lambda i: (i, 0, 0))],
        out_specs=pl.BlockSpec((TM, D), lambda i: (i, 0)),
        compiler_params=_cparams(("parallel",)), name="norm1")(h, g.reshape(1, D), mod_tiles)


def _matmul_kernel(x_ref, w_ref, o_ref):
    o_ref[...] = jnp.dot(x_ref[...], w_ref[...], preferred_element_type=f32)


def in_proj(xn, w):
    t = xn.shape[0]
    tm, tn = 1024, 1664
    return pl.pallas_call(
        _matmul_kernel, out_shape=_sds((t, IN_W)), grid=(t // tm, IN_W // tn),
        in_specs=[pl.BlockSpec((tm, D), lambda i, j: (i, 0)),
                  pl.BlockSpec((D, tn), lambda i, j: (0, j))],
        out_specs=pl.BlockSpec((tm, tn), lambda i, j: (i, j)),
        compiler_params=_cparams(("parallel", "parallel")), name="in_proj")(xn, w)


def _gelu_tanh(x):
    return 0.5 * x * (1.0 + jnp.tanh(0.7978845608028654 * (x + 0.044715 * (x * x * x))))


def _lru_kernel(ux_ref, uy_ref, cw_ref, cb_ref, wr_ref, wi_ref, br_ref, bi_ref, lam_ref, h0_ref,
                out_ref, st_ref, xp_ref, a_ref, b_ref, hs_ref, *, n):
    pad = SUBLANES
    xp_ref[pl.ds(0, pad), :] = jnp.zeros((pad, CW), f32)
    xp_ref[pl.ds(n + pad, pad), :] = jnp.zeros((pad, CW), f32)
    xp_ref[pl.ds(pad, n), :] = ux_ref[...]
    left = LRU_CONV // 2
    lam = lam_ref[...]
    neg_c_sp = -LRU_C * (jnp.maximum(-lam, 0.0) + _log1p(jnp.exp(-jnp.abs(lam))))

    def coef_body(i, c):
        r0 = pl.multiple_of(i * RC, RC)
        w = xp_ref[pl.ds(r0, RC + 2 * pad), :]
        xc = jnp.zeros((RC, CW), f32) + cb_ref[...]
        for k in range(LRU_CONV):
            off = pad - left + k
            xc = xc + cw_ref[k:k + 1, :] * w[off:off + RC, :]
        xcb = xc.astype(bf16)
        for d in range(2):
            r = _sigmoid(jnp.dot(xcb, wr_ref[d, 0], preferred_element_type=f32) + br_ref[d:d + 1, :])
            g = _sigmoid(jnp.dot(xcb, wi_ref[d, 0], preferred_element_type=f32) + bi_ref[d:d + 1, :])
            log_a = r * neg_c_sp[d:d + 1, :]
            a = jnp.exp(log_a)
            th = jnp.tanh(log_a)
            one_m_a2 = (-2.0 * th) / (1.0 - th)
            a_ref[d, pl.ds(r0, RC), :] = a
            b_ref[d, pl.ds(r0, RC), :] = jnp.sqrt(one_m_a2) * (g * xc)
        return c

    lax.fori_loop(0, n // RC, coef_body, 0)

    row = lax.broadcasted_iota(i32, (SUBLANES, CW), 0)
    nchunk = n // SUBLANES

    def scan_body(j, carry):
        hf, hb = carry
        rf = pl.multiple_of(j * SUBLANES, SUBLANES)
        a = a_ref[0, pl.ds(rf, SUBLANES), :]
        b = b_ref[0, pl.ds(rf, SUBLANES), :]
        for sh in (1, 2, 4):
            a_s = jnp.where(row >= sh, pltpu.roll(a, sh, axis=0), 1.0)
            b_s = jnp.where(row >= sh, pltpu.roll(b, sh, axis=0), 0.0)
            b = a * b_s + b
            a = a * a_s
        h = a * hf + b
        hs_ref[0, pl.ds(rf, SUBLANES), :] = h
        hf = h[SUBLANES - 1:SUBLANES, :]
        rb = pl.multiple_of((nchunk - 1 - j) * SUBLANES, SUBLANES)
        a = a_ref[1, pl.ds(rb, SUBLANES), :]
        b = b_ref[1, pl.ds(rb, SUBLANES), :]
        for sh in (1, 2, 4):
            keep = row < SUBLANES - sh
            a_s = jnp.where(keep, pltpu.roll(a, SUBLANES - sh, axis=0), 1.0)
            b_s = jnp.where(keep, pltpu.roll(b, SUBLANES - sh, axis=0), 0.0)
            b = a * b_s + b
            a = a * a_s
        h = a * hb + b
        hs_ref[1, pl.ds(rb, SUBLANES), :] = h
        hb = h[0:1, :]
        return hf, hb

    h0 = h0_ref[0]
    hf, hb = lax.fori_loop(0, nchunk, scan_body, (h0[0:1, :], h0[1:2, :]), unroll=4)
    st_ref[0, 0:1, :] = hf
    st_ref[0, 1:2, :] = hb

    def out_body(i, c):
        r0 = pl.multiple_of(i * RC, RC)
        hsum = hs_ref[0, pl.ds(r0, RC), :] + hs_ref[1, pl.ds(r0, RC), :]
        out_ref[pl.ds(r0, RC), :] = hsum * _gelu_tanh(uy_ref[pl.ds(r0, RC), :])
        return c

    lax.fori_loop(0, n // RC, out_body, 0)


def lru_branch(u, p, h0, nb, n, row_blk0):
    nct = D // CW
    cu = COL_UY // CW
    kern = functools.partial(_lru_kernel, n=n)
    vec = lambda a: pl.BlockSpec((a, CW), lambda b, c: (0, c))
    call = pl.pallas_call(
        kern, out_shape=(_sds((nb * n, D)), _sds((nb, 2, D))), grid=(nb, nct),
        in_specs=[pl.BlockSpec((n, CW), lambda b, c: (row_blk0 + b, c)),
                  pl.BlockSpec((n, CW), lambda b, c: (row_blk0 + b, cu + c)),
                  vec(LRU_CONV), vec(1),
                  pl.BlockSpec((2, 1, CW, CW), lambda b, c: (0, c, 0, 0)),
                  pl.BlockSpec((2, 1, CW, CW), lambda b, c: (0, c, 0, 0)),
                  vec(2), vec(2), vec(2),
                  pl.BlockSpec((1, 2, CW), lambda b, c: (b, 0, c))],
        out_specs=(pl.BlockSpec((n, CW), lambda b, c: (b, c)),
                   pl.BlockSpec((1, 2, CW), lambda b, c: (b, 0, c))),
        scratch_shapes=[pltpu.VMEM((n + 2 * SUBLANES, CW), f32),
                        pltpu.VMEM((2, n, CW), f32), pltpu.VMEM((2, n, CW), f32), pltpu.VMEM((2, n, CW), f32)],
        compiler_params=_cparams(("parallel", "parallel")), name="lru_branch")
    return call(u, u, p["lru_conv_w"], p["lru_conv_b"].reshape(1, D), p["lru_wr_bd"], p["lru_wi_bd"],
                p["lru_b_r"], p["lru_b_i"], p["lru_lambda"], h0)


def _conf_kernel(ua_ref, ug_ref, w_ref, b_ref, o_ref, gp_ref, *, n):
    pad = 2 * SUBLANES
    left = CONV_K // 2
    gp_ref[pl.ds(0, pad), :] = jnp.zeros((pad, CW), f32)
    gp_ref[pl.ds(n + pad, pad), :] = jnp.zeros((pad, CW), f32)
    gp_ref[pl.ds(pad, n), :] = ua_ref[...] * _sigmoid(ug_ref[...])

    wn = RC + 2 * pad

    def body(i, c):
        r0 = pl.multiple_of(i * RC, RC)
        w = gp_ref[pl.ds(r0, wn), :]
        rolled = [w] + [pltpu.roll(w, wn - m, axis=0) for m in range(1, SUBLANES)]
        acc = jnp.zeros((RC, CW), f32) + b_ref[...]
        for k in range(CONV_K):
            off = pad - left + k
            q, m = off // SUBLANES, off % SUBLANES
            acc = acc + w_ref[k:k + 1, :] * rolled[m][q * SUBLANES:q * SUBLANES + RC, :]
        o_ref[pl.ds(r0, RC), :] = acc
        return c

    lax.fori_loop(0, n // RC, body, 0)


def conf_branch(u, p, nb, n, row_blk0):
    nct = D // CW
    ca, cg = COL_UC // CW, (COL_UC + D) // CW
    kern = functools.partial(_conf_kernel, n=n)
    call = pl.pallas_call(
        kern, out_shape=_sds((nb * n, D)), grid=(nb, nct),
        in_specs=[pl.BlockSpec((n, CW), lambda b, c: (row_blk0 + b, ca + c)),
                  pl.BlockSpec((n, CW), lambda b, c: (row_blk0 + b, cg + c)),
                  pl.BlockSpec((CONV_K, CW), lambda b, c: (0, c)),
                  pl.BlockSpec((1, CW), lambda b, c: (0, c))],
        out_specs=pl.BlockSpec((n, CW), lambda b, c: (b, c)),
        scratch_shapes=[pltpu.VMEM((n + 4 * SUBLANES, CW), f32)],
        compiler_params=_cparams(("parallel", "parallel")), name="conf_branch")
    return call(u, u, p["conv_dw_w"], p["conv_dw_b"].reshape(1, D))


def _pool_kernel(up_ref, w_ref, s_ref, o_ref, xp_ref, *, n):
    pad = SUBLANES
    gi = pl.program_id(1)
    half = jnp.left_shift(1, gi)
    xp_ref[pl.ds(0, pad), :] = jnp.zeros((pad, CW), f32)
    xp_ref[pl.ds(n + pad, pad), :] = jnp.zeros((pad, CW), f32)
    xp_ref[pl.ds(pad, n), :] = up_ref[...]
    wn = RC + 2 * pad

    def body(i, c):
        r0 = pl.multiple_of(i * RC, RC)
        w = xp_ref[pl.ds(r0, wn), :]
        s2 = w + pltpu.roll(w, 1, axis=0)
        s4 = pltpu.roll(s2, 1, axis=0) + pltpu.roll(s2, wn - 1, axis=0)
        s8 = pltpu.roll(s4, 2, axis=0) + pltpu.roll(s4, wn - 2, axis=0)
        s16 = pltpu.roll(s8, 4, axis=0) + pltpu.roll(s8, wn - 4, axis=0)
        s = jnp.where(gi == 0, s2, jnp.where(gi == 1, s4, jnp.where(gi == 2, s8, s16)))[pad:pad + RC, :]
        t = r0 + lax.broadcasted_iota(i32, (RC, CW), 0)
        cnt = (jnp.minimum(t + half, n) - jnp.maximum(t - half, 0)).astype(f32)
        pooled = s / cnt - w[pad:pad + RC, :]
        o_ref[pl.ds(r0, RC), :] = jnp.dot(pooled.astype(bf16), w_ref[0], preferred_element_type=f32) * s_ref[...]
        return c

    lax.fori_loop(0, n // RC, body, 0)


def pool_branch(u, p, nb, n, row_blk0):
    assert POOL_SIZES == (2, 4, 8, 16) and D // len(POOL_SIZES) == CW
    cp = COL_UP // CW
    kern = functools.partial(_pool_kernel, n=n)
    call = pl.pallas_call(
        kern, out_shape=_sds((nb * n, D)), grid=(nb, len(POOL_SIZES)),
        in_specs=[pl.BlockSpec((n, CW), lambda b, c: (row_blk0 + b, cp + c)),
                  pl.BlockSpec((1, CW, CW), lambda b, c: (c, 0, 0)),
                  pl.BlockSpec((1, CW), lambda b, c: (0, c))],
        out_specs=pl.BlockSpec((n, CW), lambda b, c: (b, c)),
        scratch_shapes=[pltpu.VMEM((n + 2 * SUBLANES, CW), f32)],
        compiler_params=_cparams(("parallel", "parallel")), name="pool_branch")
    return call(u, p["pool_w_bf"], p["pool_scale"].reshape(1, D))


def _head_norm(blk, g128, lane_lo):
    sq = blk * blk
    s_lo = jnp.sum(jnp.where(lane_lo, sq, 0.0), axis=-1, keepdims=True)
    s_hi = jnp.sum(jnp.where(lane_lo, 0.0, sq), axis=-1, keepdims=True)
    r = jnp.where(lane_lo, lax.rsqrt(s_lo * (1.0 / HEAD_DIM) + EPS), lax.rsqrt(s_hi * (1.0 / HEAD_DIM) + EPS))
    return blk * r * g128


def _rope(blk, cos, sin_signed, lane):
    partner = jnp.where((lane % 32) < 16, pltpu.roll(blk, LANES - 16, axis=1), pltpu.roll(blk, 16, axis=1))
    return blk * cos + partner * sin_signed


def _prep_kv(k_ref, v_ref, ks_ref, vs_ref, kg, lane_lo, rope=None, newk_ref=None, newv_ref=None):
    nk = k_ref.shape[0]
    lane = lax.broadcasted_iota(i32, (nk, LANES), 1)
    for tj in range(N_KV * HEAD_DIM // LANES):
        kt = _head_norm(k_ref[:, tj * LANES:(tj + 1) * LANES], kg, lane_lo(nk))
        if rope is not None:
            kt = _rope(kt, rope[0][...], rope[1][...], lane)
        vt = v_ref[:, tj * LANES:(tj + 1) * LANES]
        if newk_ref is not None:
            newk_ref[0, :, tj * LANES:(tj + 1) * LANES] = kt
            newv_ref[0, :, tj * LANES:(tj + 1) * LANES] = vt
        _store_low_half(kt, vt, ks_ref, vs_ref, tj, lane_lo(nk))


def _store_low_half(kt, vt, ks_ref, vs_ref, tj, lane_lo):
    ks_ref[2 * tj] = jnp.where(lane_lo, kt, 0.0).astype(bf16)
    vs_ref[2 * tj] = jnp.where(lane_lo, vt, 0.0).astype(bf16)
    ks_ref[2 * tj + 1] = jnp.where(lane_lo, pltpu.roll(kt, HEAD_DIM, axis=1), 0.0).astype(bf16)
    vs_ref[2 * tj + 1] = jnp.where(lane_lo, pltpu.roll(vt, HEAD_DIM, axis=1), 0.0).astype(bf16)


def _prep_cache(ck_ref, cv_ref, kc_ref, vc_ref, lane_lo):
    nk = ck_ref.shape[2]
    for tj in range(N_KV * HEAD_DIM // LANES):
        kt = ck_ref[0, 0, :, tj * LANES:(tj + 1) * LANES]
        vt = cv_ref[0, 0, :, tj * LANES:(tj + 1) * LANES]
        _store_low_half(kt, vt, kc_ref, vc_ref, tj, lane_lo(nk))


def _qk(qm, k):
    return lax.dot_general(qm, k, (((1,), (1,)), ((), ())), preferred_element_type=f32)


def _attend(sink_ref, q_tiles, segments, att_ref):
    per_kv = N_HEADS // N_KV
    assert per_kv == 4
    rows = per_kv * TQ
    row = lax.broadcasted_iota(i32, (rows, 1), 0)
    masks = [None if mk is None else jnp.concatenate([mk] * per_kv, axis=0) for _, _, mk in segments]
    for g in range(N_KV):
        t0, t1 = q_tiles[2 * g], q_tiles[2 * g + 1]
        qm = jnp.concatenate([t0, t1, pltpu.roll(t0, HEAD_DIM, axis=1), pltpu.roll(t1, HEAD_DIM, axis=1)],
                             axis=0).astype(bf16)
        hd = [per_kv * g, per_kv * g + 2, per_kv * g + 1, per_kv * g + 3]
        sink = jnp.where(row < TQ, sink_ref[hd[0]],
                         jnp.where(row < 2 * TQ, sink_ref[hd[1]],
                                   jnp.where(row < 3 * TQ, sink_ref[hd[2]], sink_ref[hd[3]]))) * LOG2E
        scores = []
        m_el = None
        for (kget, _, _), mask in zip(segments, masks):
            s = _qk(qm, kget(g))
            if mask is not None:
                s = jnp.where(mask, s, NEG_INF)
            for c in range(s.shape[1] // LANES):
                t = s[:, c * LANES:(c + 1) * LANES]
                m_el = t if m_el is None else jnp.maximum(m_el, t)
            scores.append(s)
        m = jnp.maximum(jnp.max(m_el, axis=-1, keepdims=True), sink)
        d_el = jnp.zeros((rows, LANES), f32)
        o = jnp.zeros((rows, LANES), f32)
        for s, (_, vget, _) in zip(scores, segments):
            pr = jnp.exp2(s - m)
            for c in range(s.shape[1] // LANES):
                d_el = d_el + pr[:, c * LANES:(c + 1) * LANES]
            o = o + jnp.dot(pr.astype(bf16), vget(g), preferred_element_type=f32)
        den = jnp.exp2(sink - m) + jnp.sum(d_el, axis=-1, keepdims=True)
        o = o / den
        att_ref[:, 2 * g * LANES:(2 * g + 1) * LANES] = o[:TQ] + pltpu.roll(o[2 * TQ:3 * TQ], HEAD_DIM, axis=1)
        att_ref[:, (2 * g + 1) * LANES:(2 * g + 2) * LANES] = o[TQ:2 * TQ] + pltpu.roll(o[3 * TQ:], HEAD_DIM, axis=1)


def _q_tiles(q_ref, qg, rope=None):
    lane = lax.broadcasted_iota(i32, (TQ, LANES), 1)
    lane_lo = lane < HEAD_DIM
    tiles = []
    for j in range(N_HEADS // 2):
        qt = _head_norm(q_ref[:, j * LANES:(j + 1) * LANES], qg, lane_lo)
        if rope is not None:
            qt = _rope(qt, rope[0], rope[1], lane)
        tiles.append(qt * (HEAD_DIM ** -0.5 * LOG2E))
    return tiles


def _lane_lo_fn(nrows):
    return lax.broadcasted_iota(i32, (nrows, LANES), 1) < HEAD_DIM


def _ctx_attn_kernel(sink_ref, q_ref, k_ref, v_ref, qg_ref, kg_ref, att_ref, newk_ref, newv_ref, ks_ref, vs_ref):
    @pl.when(pl.program_id(1) == 0)
    def _():
        _prep_kv(k_ref, v_ref, ks_ref, vs_ref, kg_ref[...], _lane_lo_fn, None, newk_ref, newv_ref)

    tiles = _q_tiles(q_ref, qg_ref[...])
    seg = [(lambda g: ks_ref[g], lambda g: vs_ref[g], None)]
    _attend(sink_ref, tiles, seg, att_ref)


def ctx_attention(u, p):
    nq = N_CTX // TQ
    kvw = N_KV * HEAD_DIM
    return pl.pallas_call(
        _ctx_attn_kernel,
        out_shape=(_sds((T_CTX, D)), _sds((N_CTX_B, N_CTX, kvw)), _sds((N_CTX_B, N_CTX, kvw))),
        grid=(N_CTX_B, nq),
        in_specs=[pl.BlockSpec(memory_space=pltpu.SMEM),
                  pl.BlockSpec((TQ, D), lambda b, i: (b * nq + i, COL_UQ // D)),
                  pl.BlockSpec((N_CTX, kvw), lambda b, i: (b, COL_UK // kvw)),
                  pl.BlockSpec((N_CTX, kvw), lambda b, i: (b, COL_UV // kvw)),
                  pl.BlockSpec((1, LANES), lambda b, i: (0, 0)),
                  pl.BlockSpec((1, LANES), lambda b, i: (0, 0))],
        out_specs=(pl.BlockSpec((TQ, D), lambda b, i: (b * nq + i, 0)),
                   pl.BlockSpec((1, N_CTX, kvw), lambda b, i: (b, 0, 0)),
                   pl.BlockSpec((1, N_CTX, kvw), lambda b, i: (b, 0, 0))),
        scratch_shapes=[pltpu.VMEM((N_KV, N_CTX, LANES), bf16), pltpu.VMEM((N_KV, N_CTX, LANES), bf16)],
        compiler_params=_cparams(("parallel", "arbitrary")), name="ctx_attention")(
            p["attn_sink"], u, u, u, p["q_norm_g2"], p["k_norm_g2"])


def _lat_attn_kernel(sink_ref, q_ref, k_ref, v_ref, ck_ref, cv_ref, cosq_ref, sinq_ref, cosk_ref, sink_tab_ref,
                     qg_ref, kg_ref, att_ref, ks_ref, vs_ref, kc_ref, vc_ref):
    i = pl.program_id(1)
    nq = N_LAT // TQ

    @pl.when(i == 0)
    def _():
        _prep_kv(k_ref, v_ref, ks_ref, vs_ref, kg_ref[...], _lane_lo_fn, (cosk_ref, sink_tab_ref))
        _prep_cache(ck_ref, cv_ref, kc_ref, vc_ref, _lane_lo_fn)

    tiles = _q_tiles(q_ref, qg_ref[...], (cosq_ref[...], sinq_ref[...]))
    r = lax.broadcasted_iota(i32, (TQ, TQ), 0)
    c = lax.broadcasted_iota(i32, (TQ, TQ), 1)
    prev0 = pl.multiple_of(jnp.maximum(i - 1, 0) * TQ, TQ)
    cur0 = pl.multiple_of(i * TQ, TQ)
    next0 = pl.multiple_of(jnp.minimum(i + 1, nq - 1) * TQ, TQ)
    mask_prev = jnp.logical_and(c >= r, i > 0)
    mask_next = jnp.logical_and(c <= r, i < nq - 1)

    def seg(r0, mask):
        return (lambda g: ks_ref[g, pl.ds(r0, TQ), :], lambda g: vs_ref[g, pl.ds(r0, TQ), :], mask)

    segments = [seg(prev0, mask_prev), seg(cur0, None), seg(next0, mask_next),
                (lambda g: kc_ref[g], lambda g: vc_ref[g], None)]
    _attend(sink_ref, tiles, segments, att_ref)


def lat_attention(u, p, cache_k, cache_v, rope_cos, rope_sin):
    nq = N_LAT // TQ
    kvw = N_KV * HEAD_DIM
    l = p["layer"]
    qblk0 = T_CTX // TQ
    kblk0 = T_CTX // N_LAT
    call = pl.pallas_call(
        _lat_attn_kernel, out_shape=_sds((T_LAT, D)), grid=(N_LAT_B, nq),
        in_specs=[pl.BlockSpec(memory_space=pltpu.SMEM),
                  pl.BlockSpec((TQ, D), lambda b, i: (qblk0 + b * nq + i, COL_UQ // D)),
                  pl.BlockSpec((N_LAT, kvw), lambda b, i: (kblk0 + b, COL_UK // kvw)),
                  pl.BlockSpec((N_LAT, kvw), lambda b, i: (kblk0 + b, COL_UV // kvw)),
                  pl.BlockSpec((1, 1, PAST, kvw), lambda b, i: (b, l, 0, 0)),
                  pl.BlockSpec((1, 1, PAST, kvw), lambda b, i: (b, l, 0, 0)),
                  pl.BlockSpec((TQ, LANES), lambda b, i: (i, 0)),
                  pl.BlockSpec((TQ, LANES), lambda b, i: (i, 0)),
                  pl.BlockSpec((N_LAT, LANES), lambda b, i: (0, 0)),
                  pl.BlockSpec((N_LAT, LANES), lambda b, i: (0, 0)),
                  pl.BlockSpec((1, LANES), lambda b, i: (0, 0)),
                  pl.BlockSpec((1, LANES), lambda b, i: (0, 0))],
        out_specs=pl.BlockSpec((TQ, D), lambda b, i: (b * nq + i, 0)),
        scratch_shapes=[pltpu.VMEM((N_KV, N_LAT, LANES), bf16), pltpu.VMEM((N_KV, N_LAT, LANES), bf16),
                        pltpu.VMEM((N_KV, PAST, LANES), bf16), pltpu.VMEM((N_KV, PAST, LANES), bf16)],
        compiler_params=_cparams(("parallel", "arbitrary")), name="lat_attention")
    return call(p["attn_sink"], u, u, u, cache_k, cache_v, rope_cos, rope_sin, rope_cos, rope_sin,
                p["q_norm_g2"], p["k_norm_g2"])


def rope_tables():
    pos = jnp.arange(N_LAT)
    rows = (pos // GRID_W).astype(f32)
    cols = (pos % GRID_W).astype(f32)
    half = HEAD_DIM // 2
    freqs = ROPE_BASE ** (-jnp.arange(0, half, 2, dtype=f32) / half)
    lane = jnp.arange(LANES)
    within = lane % half
    fidx = within % (half // 2)
    use_cols = (lane % HEAD_DIM) >= half
    ang = jnp.where(use_cols[None, :], cols[:, None], rows[:, None]) * freqs[fidx][None, :]
    sign = jnp.where(within < half // 2, -1.0, 1.0)
    return jnp.cos(ang), jnp.sin(ang) * sign[None, :]


def _merge_kernel(h_ref, xn_ref, lru_c, lru_l, conv_c, conv_l, att_c, att_l, pool_c, pool_l, mod_ref, lng_ref, lnb_ref,
                  wg_ref, bg_ref, wb_ref, wo_ref, o_ref):
    is_ctx = pl.program_id(0) < T_CTX // TM
    pick = lambda c_ref, l_ref: jnp.where(is_ctx, c_ref[...], l_ref[...])
    xn = xn_ref[...]
    hc = pick(conv_c, conv_l)
    mu = jnp.mean(hc, axis=-1, keepdims=True)
    xc = hc - mu
    var = jnp.mean(xc * xc, axis=-1, keepdims=True)
    y = xc * lax.rsqrt(var + EPS) * lng_ref[...] + lnb_ref[...]
    conv = y * _sigmoid(y)
    merged = jnp.zeros((TM, D), f32)
    for j, br in enumerate((pick(lru_c, lru_l), conv, pick(att_c, att_l), pick(pool_c, pool_l))):
        gate = _sigmoid(jnp.dot(xn, wg_ref[:, j * D:(j + 1) * D], preferred_element_type=f32)
                        + bg_ref[:, j * D:(j + 1) * D])
        proj = jnp.dot(br.astype(bf16), wb_ref[j], preferred_element_type=f32)
        merged = merged + gate * proj
    out = jnp.dot(merged.astype(bf16), wo_ref[...], preferred_element_type=f32)
    o_ref[...] = h_ref[...] + mod_ref[0][2:3, :] * out


def merge(h, xn, lru, conv, att, pool, mod_tiles, p):
    nt = h.shape[0] // TM
    n_ctx_tiles = T_CTX // TM
    tok = pl.BlockSpec((TM, D), lambda i: (i, 0))
    tok_c = pl.BlockSpec((TM, D), lambda i: (jnp.minimum(i, n_ctx_tiles - 1), 0))
    tok_l = pl.BlockSpec((TM, D), lambda i: (jnp.maximum(i - n_ctx_tiles, 0), 0))
    const = lambda shape: pl.BlockSpec(shape, lambda i: (0,) * len(shape), pipeline_mode=pl.Buffered(1))
    return pl.pallas_call(
        _merge_kernel, out_shape=_sds(h.shape), grid=(nt,),
        in_specs=[tok, tok, tok_c, tok_l, tok_c, tok_l, tok_c, tok_l, tok_c, tok_l,
                  pl.BlockSpec((1, 6, D), lambda i: (i, 0, 0)),
                  const((1, D)), const((1, D)),
                  const((D, 4 * D)), const((1, 4 * D)), const((4, D, D)), const((D, D))],
        out_specs=tok,
        compiler_params=_cparams(("parallel",)), name="merge")(
            h, xn, *lru, *conv, *att, *pool, mod_tiles, p["conv_ln_g"].reshape(1, D), p["conv_ln_b"].reshape(1, D),
            p["w_gate_bf"], p["b_gate"].reshape(1, 4 * D), p["w_branch_bf"], p["w_out_bf"])


def _router_kernel(h_ref, g_ref, mod_ref, rw_ref, rb_ref, xn_ref, topi_ref, topw_ref, rank_ref, cnt_ref, carry_ref):
    step = pl.program_id(0)

    @pl.when(step == 0)
    def _():
        carry_ref[...] = jnp.zeros_like(carry_ref)

    m = mod_ref[0]
    xn = _rms_mod(h_ref[...], g_ref[...], m[3:4, :], m[4:5, :])
    xn_ref[...] = xn.reshape(xn_ref.shape)
    logits = jnp.dot(xn, rw_ref[...], preferred_element_type=f32, precision=lax.Precision.HIGHEST) + rb_ref[...]
    lane = lax.broadcasted_iota(i32, (TM, LANES), 1).astype(f32)
    lg = jnp.where(lane < N_EXPERTS, logits, NEG_INF)
    vals, idxs = [], []
    for _ in range(TOP_K):
        mx = jnp.max(lg, axis=-1, keepdims=True)
        idx = jnp.min(jnp.where(lg == mx, lane, float(LANES)), axis=-1, keepdims=True)
        vals.append(mx)
        idxs.append(idx)
        lg = jnp.where(lane == idx, -3e38, lg)
    exps = [jnp.exp(v - vals[0]) for v in vals]
    den = exps[0] + exps[1] + exps[2] + exps[3]
    cnt = jnp.zeros((TM, LANES), f32)
    for idx in idxs:
        cnt = cnt + jnp.where(lane == idx, 1.0, 0.0)
    rr = lax.broadcasted_iota(i32, (TM, TM), 0)
    cc = lax.broadcasted_iota(i32, (TM, TM), 1)
    tri = jnp.where(rr > cc, 1.0, 0.0).astype(bf16)
    before = jnp.dot(tri, cnt.astype(bf16), preferred_element_type=f32) + carry_ref[0:1, :]
    topi = jnp.zeros((TM, LANES), f32)
    topw = jnp.zeros((TM, LANES), f32)
    rank = jnp.zeros((TM, LANES), f32)
    for k in range(TOP_K):
        rk = jnp.sum(jnp.where(lane == idxs[k], before, 0.0), axis=-1, keepdims=True)
        topi = jnp.where(lane == k, idxs[k], topi)
        topw = jnp.where(lane == k, exps[k] / den, topw)
        rank = jnp.where(lane == k, rk, rank)
    topi_ref[...] = topi.astype(i32)
    topw_ref[...] = topw
    rank_ref[...] = rank.astype(i32)
    total = carry_ref[0:1, :] + jnp.sum(cnt, axis=0, keepdims=True)
    carry_ref[...] = jnp.broadcast_to(total, carry_ref.shape)
    cnt_ref[...] = jnp.broadcast_to(total, cnt_ref.shape).astype(i32)


def router(h, mod_tiles, p):
    nt = h.shape[0] // TM
    tok = lambda w, dt=f32: pl.BlockSpec((TM, w), lambda i: (i, 0))
    t = h.shape[0]
    return pl.pallas_call(
        _router_kernel,
        out_shape=(_sds((t, D // LANES, LANES)), _sds((t, LANES), i32), _sds((t, LANES)), _sds((t, LANES), i32),
                   _sds((SUBLANES, LANES), i32)),
        grid=(nt,),
        in_specs=[tok(D), pl.BlockSpec((1, D), lambda i: (0, 0)), pl.BlockSpec((1, 6, D), lambda i: (i, 0, 0)),
                  pl.BlockSpec((D, LANES), lambda i: (0, 0)), pl.BlockSpec((1, LANES), lambda i: (0, 0))],
        out_specs=(pl.BlockSpec((TM, D // LANES, LANES), lambda i: (i, 0, 0)), tok(LANES), tok(LANES), tok(LANES),
                   pl.BlockSpec((SUBLANES, LANES), lambda i: (0, 0))),
        scratch_shapes=[pltpu.VMEM((SUBLANES, LANES), f32)],
        compiler_params=_cparams(("arbitrary",)), name="router")(
            h, p["norm2_g"].reshape(1, D), mod_tiles, p["router_w_pad"], p["router_b_pad"])


def _dest_kernel(topi_ref, rank_ref, pstart_ref, o_ref):
    lane = lax.broadcasted_iota(i32, (TM, LANES), 1).astype(f32)
    topi = topi_ref[...].astype(f32)
    pstart = pstart_ref[...].astype(f32)
    dest = rank_ref[...].astype(f32)
    for k in range(TOP_K):
        e = jnp.sum(jnp.where(lane == k, topi, 0.0), axis=-1, keepdims=True)
        ps = jnp.sum(jnp.where(lane == e, pstart, 0.0), axis=-1, keepdims=True)
        dest = dest + jnp.where(lane == k, ps, 0.0)
    o_ref[...] = dest.astype(i32)


def dest_rows(topi, rank, pstart):
    t = topi.shape[0]
    tok = pl.BlockSpec((TM, LANES), lambda i: (i, 0))
    return pl.pallas_call(
        _dest_kernel, out_shape=_sds((t, LANES), i32), grid=(t // TM,),
        in_specs=[tok, tok, pl.BlockSpec((1, LANES), lambda i: (0, 0))], out_specs=tok,
        compiler_params=_cparams(("parallel",)), name="dest_rows")(topi, rank, pstart)


ROW3 = (D // LANES, LANES)
assert ROW3[0] == SUBLANES


def _row_copy(src_ref, s, dst_ref, d, sem):
    return pltpu.make_async_copy(src_ref.at[pl.ds(s, 1)], dst_ref.at[pl.ds(d, 1)], sem)


def _dispatch_kernel(zflag_ref, dest_ref, x_ref, xb_ref, zero_ref, sem):
    @pl.when(pl.program_id(0) == 0)
    def _():
        zero_ref[...] = jnp.zeros_like(zero_ref)

        def zcopy(j):
            r0 = pl.multiple_of(j * MOE_BM, MOE_BM)
            return pltpu.make_async_copy(zero_ref, xb_ref.at[pl.ds(r0, MOE_BM)], sem)

        def zstart(j, c):
            @pl.when(zflag_ref[j] != 0)
            def _():
                zcopy(j).start()
            return c

        def zwait(j, c):
            @pl.when(zflag_ref[j] != 0)
            def _():
                zcopy(j).wait()
            return c

        lax.fori_loop(0, MOE_NBLK, zstart, 0)
        lax.fori_loop(0, MOE_NBLK, zwait, 0)

    def start(t, c):
        for k in range(TOP_K):
            _row_copy(x_ref, t, xb_ref, dest_ref[t * TOP_K + k], sem).start(priority=k % 2)
        return c

    lax.fori_loop(0, TM, start, 0)
    assert MOE_BM == TM
    for _ in range(TOP_K):
        pltpu.make_async_copy(zero_ref, xb_ref.at[pl.ds(0, MOE_BM)], sem).wait()


def dispatch(zflag, dest_flat, xn3):
    t = xn3.shape[0]
    grid_spec = pltpu.PrefetchScalarGridSpec(
        num_scalar_prefetch=1, grid=(t // TM,),
        in_specs=[pl.BlockSpec((TM * TOP_K,), lambda i, zb: (i,), memory_space=pltpu.SMEM),
                  pl.BlockSpec((TM,) + ROW3, lambda i, zb: (i, 0, 0))],
        out_specs=pl.BlockSpec(memory_space=pl.ANY),
        scratch_shapes=[pltpu.VMEM((MOE_BM,) + ROW3, f32), pltpu.SemaphoreType.DMA(())])
    return pl.pallas_call(
        _dispatch_kernel, out_shape=_sds((MOE_CAP,) + ROW3), grid_spec=grid_spec,
        compiler_params=_cparams(("arbitrary",)), name="moe_dispatch")(zflag, dest_flat, xn3)


def _expert_kernel(be_ref, nused_ref, first_ref, wslot_ref, nxt_ref, xb_ref, wgu_hbm, bgu_ref, wdn_hbm, bdn_ref,
                   yb_ref, wgu_f, wdn_f, wsem, wgu_bf, wdn_bf, *, layer):
    i = pl.program_id(0)
    e = be_ref[i]

    def weight_copies(ex, s):
        return (pltpu.make_async_copy(wgu_hbm.at[layer, ex], wgu_f.at[s], wsem.at[s]),
                pltpu.make_async_copy(wdn_hbm.at[layer, ex], wdn_f.at[s], wsem.at[s]))

    @pl.when(first_ref[i] != 0)
    def _():
        s = wslot_ref[i]

        @pl.when(i == 0)
        def _():
            for cp in weight_copies(e, s):
                cp.start()

        for cp in weight_copies(e, s):
            cp.wait()
        wgu_bf[...] = wgu_f[s].astype(bf16)
        wdn_bf[...] = wdn_f[s].astype(bf16)

        @pl.when(nxt_ref[i] >= 0)
        def _():
            for cp in weight_copies(nxt_ref[i], 1 - s):
                cp.start()

    @pl.when(i < nused_ref[0])
    def _():
        x = xb_ref[...].reshape(MOE_BM, D).astype(bf16)
        h = jnp.dot(x, wgu_bf[...], preferred_element_type=f32) + bgu_ref[0, 0]
        gate = jnp.minimum(h[:, :D], SWIGLU_LIMIT)
        lin = jnp.clip(h[:, D:], -SWIGLU_LIMIT, SWIGLU_LIMIT)
        act = (lin + 1.0) * (gate * _sigmoid(SWIGLU_ALPHA * gate))
        y = jnp.dot(act.astype(bf16), wdn_bf[...], preferred_element_type=f32) + bdn_ref[0, 0]
        yb_ref[...] = y.reshape((MOE_BM,) + ROW3)

    @pl.when(i >= nused_ref[0])
    def _():
        yb_ref[...] = jnp.zeros_like(yb_ref)


def experts(block_e, nused, first, wslot, nxt, xb, p):
    l = p["layer"]
    grid_spec = pltpu.PrefetchScalarGridSpec(
        num_scalar_prefetch=5, grid=(MOE_NBLK,),
        in_specs=[pl.BlockSpec((MOE_BM,) + ROW3,
                               lambda i, be, nu, *_: (jnp.maximum(jnp.minimum(i, nu[0] - 1), 0), 0, 0)),
                  pl.BlockSpec(memory_space=pl.ANY),
                  pl.BlockSpec((1, 1, 1, 2 * D), lambda i, be, *_: (l, be[i], 0, 0)),
                  pl.BlockSpec(memory_space=pl.ANY),
                  pl.BlockSpec((1, 1, 1, D), lambda i, be, *_: (l, be[i], 0, 0))],
        out_specs=pl.BlockSpec((MOE_BM,) + ROW3, lambda i, *_: (i, 0, 0)),
        scratch_shapes=[pltpu.VMEM((2, D, 2 * D), f32), pltpu.VMEM((2, D, D), f32), pltpu.SemaphoreType.DMA((2,)),
                        pltpu.VMEM((D, 2 * D), bf16), pltpu.VMEM((D, D), bf16)])
    return pl.pallas_call(
        functools.partial(_expert_kernel, layer=l), out_shape=_sds((MOE_CAP,) + ROW3), grid_spec=grid_spec,
        compiler_params=_cparams(("arbitrary",)), name="moe_experts")(
            block_e, nused, first, wslot, nxt, xb, p["exp_w_gu"],
            p["exp_b_gu"].reshape(DEPTH, N_EXPERTS, 1, 2 * D), p["exp_w_down"],
            p["exp_b_down"].reshape(DEPTH, N_EXPERTS, 1, D))


def _combine_kernel(dest_ref, dest_nxt_ref, yb_ref, topw_ref, h_ref, mod_ref, o_ref, buf, sem):
    i = pl.program_id(0)
    slot = lax.rem(i, 2)

    def issue(d_ref, s):
        def start(t, c):
            for k in range(TOP_K):
                pltpu.make_async_copy(yb_ref.at[pl.ds(d_ref[t * TOP_K + k], 1)], buf.at[s, pl.ds(k * TM + t, 1)],
                                      sem.at[s]).start(priority=k % 2)
            return c

        lax.fori_loop(0, TM, start, 0)

    @pl.when(i == 0)
    def _():
        issue(dest_ref, 0)

    @pl.when(i + 1 < pl.num_programs(0))
    def _():
        issue(dest_nxt_ref, 1 - slot)

    pltpu.make_async_copy(yb_ref.at[pl.ds(0, TOP_K * TM)], buf.at[slot], sem.at[slot]).wait()
    w = topw_ref[...]
    y = jnp.zeros((TM, D), f32)
    for k in range(TOP_K):
        y = y + buf[slot, pl.ds(k * TM, TM)].reshape(TM, D) * w[:, k:k + 1]
    o_ref[...] = h_ref[...] + mod_ref[0][5:6, :] * y


def combine(dest_flat, yb, topw, h, mod_tiles):
    t = h.shape[0]
    nt = t // TM
    tok = pl.BlockSpec((TM, D), lambda i: (i, 0))
    return pl.pallas_call(
        _combine_kernel, out_shape=_sds(h.shape), grid=(nt,),
        in_specs=[pl.BlockSpec((TM * TOP_K,), lambda i: (i,), memory_space=pltpu.SMEM),
                  pl.BlockSpec((TM * TOP_K,), lambda i: (jnp.minimum(i + 1, nt - 1),), memory_space=pltpu.SMEM),
                  pl.BlockSpec(memory_space=pl.ANY),
                  pl.BlockSpec((TM, LANES), lambda i: (i, 0)), tok,
                  pl.BlockSpec((1, 6, D), lambda i: (i, 0, 0))],
        out_specs=tok,
        scratch_shapes=[pltpu.VMEM((2, TOP_K * TM) + ROW3, f32), pltpu.SemaphoreType.DMA((2,))],
        compiler_params=_cparams(("arbitrary",)), name="moe_combine")(
            dest_flat, dest_flat, yb, topw, h, mod_tiles)


def moe_layer(h, mod_tiles, p):
    xn3, topi, topw, rank, counts = router(h, mod_tiles, p)
    cnt = counts[0, :N_EXPERTS]
    padded = (cnt + MOE_BM - 1) // MOE_BM * MOE_BM
    pend = jnp.cumsum(padded)
    pstart = jnp.zeros((1, LANES), i32).at[0, :N_EXPERTS].set(pend - padded)
    blk_row0 = jnp.arange(MOE_NBLK, dtype=i32) * MOE_BM
    block_e = jnp.minimum(jnp.sum((pend[None, :] <= blk_row0[:, None]).astype(i32), axis=1), N_EXPERTS - 1)
    nused = (pend[-1:] // MOE_BM).astype(i32)
    blk = jnp.arange(MOE_NBLK, dtype=i32)
    last_of_expert = jnp.any((padded > 0)[None, :] & (blk[:, None] == (pend // MOE_BM - 1)[None, :]), axis=1)
    zflag = (last_of_expert | (blk >= nused[0])).astype(i32)
    dest = dest_rows(topi, rank, pstart)
    dest_flat = dest[:, :TOP_K].reshape(-1)
    xb = dispatch(zflag, dest_flat, xn3)
    first = ((blk < nused[0]) & (block_e != jnp.concatenate([jnp.full((1,), -1, i32), block_e[:-1]]))).astype(i32)
    wslot = (jnp.maximum(jnp.cumsum(first) - 1, 0) % 2).astype(i32)
    eidx = jnp.arange(N_EXPERTS, dtype=i32)
    later = jnp.where((padded > 0)[None, :] & (eidx[None, :] > eidx[:, None]), eidx[None, :], N_EXPERTS)
    nxt_tab = jnp.min(later, axis=1)
    nxt = jnp.where(nxt_tab == N_EXPERTS, -1, nxt_tab)[block_e].astype(i32)
    yb = experts(block_e, nused, first, wslot, nxt, xb, p)
    return combine(dest_flat, yb, topw, h, mod_tiles)


def _block_diag(w):
    per = CW // LRU_BW
    w5 = w.reshape(2, D // CW, per, LRU_BW, LRU_BW)
    eye = jnp.eye(per, dtype=w.dtype)
    bd = w5[:, :, :, :, None, :] * eye[None, None, :, None, :, None]
    return bd.reshape(2, D // CW, CW, CW).astype(bf16)


def _layer_params(l, a):
    tile2 = lambda g: jnp.tile(g, 2).reshape(1, LANES)
    whole = ("exp_w_gu", "exp_b_gu", "exp_w_down", "exp_b_down", "w_in", "w_gate", "w_branch", "w_out")
    p = {k: v[l] for k, v in a.items() if k not in whole}
    p["layer"] = l
    for k in ("exp_w_gu", "exp_b_gu", "exp_w_down", "exp_b_down"):
        p[k] = a[k]
    p["w_in_bf"] = cast_bf16(a["w_in"], l)
    p["w_gate_bf"] = cast_bf16(a["w_gate"], l)
    p["w_branch_bf"] = cast_bf16(a["w_branch"].reshape(DEPTH, 4 * D, D), l).reshape(4, D, D)
    p["w_out_bf"] = cast_bf16(a["w_out"], l)
    p["pool_w_bf"] = cast_bf16(a["pool_w"].reshape(DEPTH, D, CW), l).reshape(4, CW, CW)
    p["lru_wr_bd"] = _block_diag(p["lru_w_r"])
    p["lru_wi_bd"] = _block_diag(p["lru_w_i"])
    p["q_norm_g2"] = tile2(p["q_norm_g"])
    p["k_norm_g2"] = tile2(p["k_norm_g"])
    p["router_w_pad"] = jnp.pad(p["router_w"], ((0, 0), (0, LANES - N_EXPERTS)))
    p["router_b_pad"] = jnp.pad(p["router_b"], (0, LANES - N_EXPERTS)).reshape(1, LANES)
    return p


def kernel(x_prompt, x_sample, cache_k, cache_v, state_lru, c, c_ctx, norm1_g, norm2_g, w_mod, b_mod, w_in,
           lru_conv_w, lru_conv_b, lru_w_r, lru_b_r, lru_w_i, lru_b_i, lru_lambda, conv_dw_w, conv_dw_b,
           conv_ln_g, conv_ln_b, q_norm_g, k_norm_g, attn_sink, pool_w, pool_scale, w_branch, w_gate, b_gate,
           w_out, router_w, router_b, exp_w_gu, exp_b_gu, exp_w_down, exp_b_down):
    weights = dict(norm1_g=norm1_g, norm2_g=norm2_g, w_in=w_in, lru_conv_w=lru_conv_w, lru_conv_b=lru_conv_b,
                   lru_w_r=lru_w_r, lru_b_r=lru_b_r, lru_w_i=lru_w_i, lru_b_i=lru_b_i, lru_lambda=lru_lambda,
                   conv_dw_w=conv_dw_w, conv_dw_b=conv_dw_b, conv_ln_g=conv_ln_g, conv_ln_b=conv_ln_b,
                   q_norm_g=q_norm_g, k_norm_g=k_norm_g, attn_sink=attn_sink, pool_w=pool_w, pool_scale=pool_scale,
                   w_branch=w_branch, w_gate=w_gate, b_gate=b_gate, w_out=w_out, router_w=router_w,
                   router_b=router_b, exp_w_gu=exp_w_gu, exp_b_gu=exp_b_gu, exp_w_down=exp_w_down,
                   exp_b_down=exp_b_down)
    kvw = N_KV * HEAD_DIM
    n_cond = 2 * SUBLANES
    cond = jnp.concatenate([c_ctx[None, :], c, jnp.zeros((n_cond - 1 - N_LAT_B, D), f32)], axis=0)
    mod = modulation(cond, w_mod, b_mod)
    tile_start = jnp.arange(T // TM) * TM
    tile_row = jnp.where(tile_start < T_CTX, 0, 1 + (tile_start - T_CTX) // N_LAT)
    h = jnp.concatenate([x_prompt.reshape(T_CTX, D), x_sample.reshape(T_LAT, D)], axis=0)
    ck = cache_k.reshape(N_LAT_B, DEPTH, PAST, kvw)
    cv = cache_v.reshape(N_LAT_B, DEPTH, PAST, kvw)
    rope_cos, rope_sin = rope_tables()
    zero_state = jnp.zeros((N_CTX_B, 2, D), f32)
    new_k, new_v, new_s = [], [], []
    for l in range(DEPTH):
        p = _layer_params(l, weights)
        mod_tiles = mod[l][tile_row].reshape(T // TM, 6, D)
        xn = norm1(h, p["norm1_g"], mod_tiles)
        u = in_proj(xn, p["w_in_bf"])
        lat_blk0 = T_CTX // N_LAT
        lru_c, st_c = lru_branch(u, p, zero_state, N_CTX_B, N_CTX, 0)
        lru_l, _ = lru_branch(u, p, state_lru[:, l], N_LAT_B, N_LAT, lat_blk0)
        conv = (conf_branch(u, p, N_CTX_B, N_CTX, 0), conf_branch(u, p, N_LAT_B, N_LAT, lat_blk0))
        pool = (pool_branch(u, p, N_CTX_B, N_CTX, 0), pool_branch(u, p, N_LAT_B, N_LAT, lat_blk0))
        att_c, k_l, v_l = ctx_attention(u, p)
        att_l = lat_attention(u, p, ck, cv, rope_cos, rope_sin)
        h = merge(h, xn, (lru_c, lru_l), conv, (att_c, att_l), pool, mod_tiles, p)
        h = moe_layer(h, mod_tiles, p)
        new_k.append(k_l.reshape(N_CTX_B, N_CTX, N_KV, HEAD_DIM))
        new_v.append(v_l.reshape(N_CTX_B, N_CTX, N_KV, HEAD_DIM))
        new_s.append(st_c)
    y_prompt = h[:T_CTX].reshape(N_CTX_B, N_CTX, D)
    y_sample = h[T_CTX:].reshape(N_LAT_B, N_LAT, D)
    return (y_prompt, y_sample, jnp.stack(new_k, axis=1), jnp.stack(new_v, axis=1), jnp.stack(new_s, axis=1))
```

```python
import functools

import jax
import jax.numpy as jnp
from jax import lax
from jax.experimental import pallas as pl
from jax.experimental.pallas import tpu as pltpu

f32 = jnp.float32
bf16 = jnp.bfloat16
i32 = jnp.int32

D = 1024
N_CTX_B, N_CTX = 16, 256
N_LAT_B, N_LAT = 8, 1024
T_CTX = N_CTX_B * N_CTX
T_LAT = N_LAT_B * N_LAT
T = T_CTX + T_LAT
DEPTH = 2
PAST = 512
GRID_W = 64
IN_W = 6656
COL_UX, COL_UY, COL_UC, COL_UQ, COL_UK, COL_UV, COL_UP = 0, 1024, 2048, 4096, 5120, 5376, 5632
LRU_BW = 64
LRU_CONV = 4
LRU_C = 8.0
CONV_K = 31
N_HEADS, N_KV, HEAD_DIM = 16, 4, 64
POOL_SIZES = (2, 4, 8, 16)
N_EXPERTS, TOP_K = 32, 4
SWIGLU_LIMIT, SWIGLU_ALPHA = 7.0, 1.702
EPS = 1e-6
NEG_INF = -1e30
ROPE_BASE = 10000.0
LOG2E = 1.4426950408889634

LANES = 128
SUBLANES = 8
VMEM_LIMIT = 56 * 1024 * 1024

CW = 256
RC = 128
TQ = 128
TM = 256
MOE_BM = 512
N_ASSIGN = T * TOP_K
MOE_NBLK = N_ASSIGN // MOE_BM + N_EXPERTS
MOE_CAP = MOE_NBLK * MOE_BM


def _sds(shape, dt=f32):
    return jax.ShapeDtypeStruct(shape, dt)


def _cparams(sem, vmem=VMEM_LIMIT):
    return pltpu.CompilerParams(dimension_semantics=sem, vmem_limit_bytes=vmem)


def _sigmoid(x):
    return 0.5 * jnp.tanh(0.5 * x) + 0.5


def _log1p(z):
    u = 1.0 + z
    d = u - 1.0
    return jnp.where(d == 0.0, z, jnp.log(u) * (z / jnp.where(d == 0.0, 1.0, d)))


def _cast_kernel(x_ref, o_ref):
    o_ref[...] = x_ref[...].astype(o_ref.dtype)


def cast_bf16(w, l):
    _, r, c = w.shape
    tr = 256
    return pl.pallas_call(
        _cast_kernel, out_shape=_sds((r, c), bf16), grid=(r // tr,),
        in_specs=[pl.BlockSpec((None, tr, c), lambda i: (l, i, 0))],
        out_specs=pl.BlockSpec((tr, c), lambda i: (i, 0)),
        compiler_params=_cparams(("parallel",)), name="cast_bf16")(w)


def _mod_kernel(c_ref, w_ref, b_ref, o_ref):
    x = c_ref[...]
    s = x * _sigmoid(x)
    o_ref[0] = jnp.dot(s.astype(bf16), w_ref[0].astype(bf16), preferred_element_type=f32) + b_ref[0]


def modulation(cond, w_mod, b_mod):
    r = cond.shape[0]
    tn = 1536
    return pl.pallas_call(
        _mod_kernel, out_shape=_sds((DEPTH, r, 6 * D)), grid=(DEPTH, 6 * D // tn),
        in_specs=[pl.BlockSpec((r, D), lambda l, j: (0, 0)),
                  pl.BlockSpec((1, D, tn), lambda l, j: (l, 0, j)),
                  pl.BlockSpec((1, 1, tn), lambda l, j: (l, 0, j))],
        out_specs=pl.BlockSpec((1, r, tn), lambda l, j: (l, 0, j)),
        compiler_params=_cparams(("parallel", "parallel")), name="modulation")(
            cond, w_mod, b_mod.reshape(DEPTH, 1, 6 * D))


def _rms_mod(x, g, shift, scale):
    ms = jnp.mean(x * x, axis=-1, keepdims=True)
    return (x * lax.rsqrt(ms + EPS) * g) * (1.0 + scale) + shift


def _norm1_kernel(h_ref, g_ref, mod_ref, o_ref):
    m = mod_ref[0]
    o_ref[...] = _rms_mod(h_ref[...], g_ref[...], m[0:1, :], m[1:2, :]).astype(bf16)


def norm1(h, g, mod_tiles):
    nt = h.shape[0] // TM
    return pl.pallas_call(
        _norm1_kernel, out_shape=_sds(h.shape, bf16), grid=(nt,),
        in_specs=[pl.BlockSpec((TM, D), lambda i: (i, 0)),
                  pl.BlockSpec((1, D), lambda i: (0, 0)),
                  pl.BlockSpec((1, 6, D), lambda i: (i, 0, 0))],
        out_specs=pl.BlockSpec((TM, D), lambda i: (i, 0)),
        compiler_params=_cparams(("parallel",)), name="norm1")(h, g.reshape(1, D), mod_tiles)


def _matmul_kernel(x_ref, w_ref, o_ref):
    o_ref[...] = jnp.dot(x_ref[...], w_ref[...], preferred_element_type=f32)


def in_proj(xn, w):
    t = xn.shape[0]
    tm, tn = 1024, 1664
    return pl.pallas_call(
        _matmul_kernel, out_shape=_sds((t, IN_W)), grid=(t // tm, IN_W // tn),
        in_specs=[pl.BlockSpec((tm, D), lambda i, j: (i, 0)),
                  pl.BlockSpec((D, tn), lambda i, j: (0, j))],
        out_specs=pl.BlockSpec((tm, tn), lambda i, j: (i, j)),
        compiler_params=_cparams(("parallel", "parallel")), name="in_proj")(xn, w)


def _gelu_tanh(x):
    return 0.5 * x * (1.0 + jnp.tanh(0.7978845608028654 * (x + 0.044715 * (x * x * x))))


def _lru_kernel(ux_ref, uy_ref, cw_ref, cb_ref, wr_ref, wi_ref, br_ref, bi_ref, lam_ref, h0_ref,
                out_ref, st_ref, xp_ref, a_ref, b_ref, hs_ref, *, n):
    pad = SUBLANES
    xp_ref[pl.ds(0, pad), :] = jnp.zeros((pad, CW), f32)
    xp_ref[pl.ds(n + pad, pad), :] = jnp.zeros((pad, CW), f32)
    xp_ref[pl.ds(pad, n), :] = ux_ref[...]
    left = LRU_CONV // 2
    lam = lam_ref[...]
    neg_c_sp = -LRU_C * (jnp.maximum(-lam, 0.0) + _log1p(jnp.exp(-jnp.abs(lam))))

    def coef_body(i, c):
        r0 = pl.multiple_of(i * RC, RC)
        w = xp_ref[pl.ds(r0, RC + 2 * pad), :]
        xc = jnp.zeros((RC, CW), f32) + cb_ref[...]
        for k in range(LRU_CONV):
            off = pad - left + k
            xc = xc + cw_ref[k:k + 1, :] * w[off:off + RC, :]
        xcb = xc.astype(bf16)
        for d in range(2):
            r = _sigmoid(jnp.dot(xcb, wr_ref[d, 0], preferred_element_type=f32) + br_ref[d:d + 1, :])
            g = _sigmoid(jnp.dot(xcb, wi_ref[d, 0], preferred_element_type=f32) + bi_ref[d:d + 1, :])
            log_a = r * neg_c_sp[d:d + 1, :]
            a = jnp.exp(log_a)
            th = jnp.tanh(log_a)
            one_m_a2 = (-2.0 * th) / (1.0 - th)
            a_ref[d, pl.ds(r0, RC), :] = a
            b_ref[d, pl.ds(r0, RC), :] = jnp.sqrt(one_m_a2) * (g * xc)
        return c

    lax.fori_loop(0, n // RC, coef_body, 0)

    row = lax.broadcasted_iota(i32, (SUBLANES, CW), 0)
    nchunk = n // SUBLANES

    def scan_body(j, carry):
        hf, hb = carry
        rf = pl.multiple_of(j * SUBLANES, SUBLANES)
        a = a_ref[0, pl.ds(rf, SUBLANES), :]
        b = b_ref[0, pl.ds(rf, SUBLANES), :]
        for sh in (1, 2, 4):
            a_s = jnp.where(row >= sh, pltpu.roll(a, sh, axis=0), 1.0)
            b_s = jnp.where(row >= sh, pltpu.roll(b, sh, axis=0), 0.0)
            b = a * b_s + b
            a = a * a_s
        h = a * hf + b
        hs_ref[0, pl.ds(rf, SUBLANES), :] = h
        hf = h[SUBLANES - 1:SUBLANES, :]
        rb = pl.multiple_of((nchunk - 1 - j) * SUBLANES, SUBLANES)
        a = a_ref[1, pl.ds(rb, SUBLANES), :]
        b = b_ref[1, pl.ds(rb, SUBLANES), :]
        for sh in (1, 2, 4):
            keep = row < SUBLANES - sh
            a_s = jnp.where(keep, pltpu.roll(a, SUBLANES - sh, axis=0), 1.0)
            b_s = jnp.where(keep, pltpu.roll(b, SUBLANES - sh, axis=0), 0.0)
            b = a * b_s + b
            a = a * a_s
        h = a * hb + b
        hs_ref[1, pl.ds(rb, SUBLANES), :] = h
        hb = h[0:1, :]
        return hf, hb

    h0 = h0_ref[0]
    hf, hb = lax.fori_loop(0, nchunk, scan_body, (h0[0:1, :], h0[1:2, :]), unroll=4)
    st_ref[0, 0:1, :] = hf
    st_ref[0, 1:2, :] = hb

    def out_body(i, c):
        r0 = pl.multiple_of(i * RC, RC)
        hsum = hs_ref[0, pl.ds(r0, RC), :] + hs_ref[1, pl.ds(r0, RC), :]
        out_ref[pl.ds(r0, RC), :] = hsum * _gelu_tanh(uy_ref[pl.ds(r0, RC), :])
        return c

    lax.fori_loop(0, n // RC, out_body, 0)


def lru_branch(u, p, h0, nb, n, row_blk0):
    nct = D // CW
    cu = COL_UY // CW
    kern = functools.partial(_lru_kernel, n=n)
    vec = lambda a: pl.BlockSpec((a, CW), lambda b, c: (0, c))
    call = pl.pallas_call(
        kern, out_shape=(_sds((nb * n, D)), _sds((nb, 2, D))), grid=(nb, nct),
        in_specs=[pl.BlockSpec((n, CW), lambda b, c: (row_blk0 + b, c)),
                  pl.BlockSpec((n, CW), lambda b, c: (row_blk0 + b, cu + c)),
                  vec(LRU_CONV), vec(1),
                  pl.BlockSpec((2, 1, CW, CW), lambda b, c: (0, c, 0, 0)),
                  pl.BlockSpec((2, 1, CW, CW), lambda b, c: (0, c, 0, 0)),
                  vec(2), vec(2), vec(2),
                  pl.BlockSpec((1, 2, CW), lambda b, c: (b, 0, c))],
        out_specs=(pl.BlockSpec((n, CW), lambda b, c: (b, c)),
                   pl.BlockSpec((1, 2, CW), lambda b, c: (b, 0, c))),
        scratch_shapes=[pltpu.VMEM((n + 2 * SUBLANES, CW), f32),
                        pltpu.VMEM((2, n, CW), f32), pltpu.VMEM((2, n, CW), f32), pltpu.VMEM((2, n, CW), f32)],
        compiler_params=_cparams(("parallel", "parallel")), name="lru_branch")
    return call(u, u, p["lru_conv_w"], p["lru_conv_b"].reshape(1, D), p["lru_wr_bd"], p["lru_wi_bd"],
                p["lru_b_r"], p["lru_b_i"], p["lru_lambda"], h0)


def _conf_kernel(ua_ref, ug_ref, w_ref, b_ref, o_ref, gp_ref, *, n):
    pad = 2 * SUBLANES
    left = CONV_K // 2
    gp_ref[pl.ds(0, pad), :] = jnp.zeros((pad, CW), f32)
    gp_ref[pl.ds(n + pad, pad), :] = jnp.zeros((pad, CW), f32)
    gp_ref[pl.ds(pad, n), :] = ua_ref[...] * _sigmoid(ug_ref[...])

    wn = RC + 2 * pad

    def body(i, c):
        r0 = pl.multiple_of(i * RC, RC)
        w = gp_ref[pl.ds(r0, wn), :]
        rolled = [w] + [pltpu.roll(w, wn - m, axis=0) for m in range(1, SUBLANES)]
        acc = jnp.zeros((RC, CW), f32) + b_ref[...]
        for k in range(CONV_K):
            off = pad - left + k
            q, m = off // SUBLANES, off % SUBLANES
            acc = acc + w_ref[k:k + 1, :] * rolled[m][q * SUBLANES:q * SUBLANES + RC, :]
        o_ref[pl.ds(r0, RC), :] = acc
        return c

    lax.fori_loop(0, n // RC, body, 0)


def conf_branch(u, p, nb, n, row_blk0):
    nct = D // CW
    ca, cg = COL_UC // CW, (COL_UC + D) // CW
    kern = functools.partial(_conf_kernel, n=n)
    call = pl.pallas_call(
        kern, out_shape=_sds((nb * n, D)), grid=(nb, nct),
        in_specs=[pl.BlockSpec((n, CW), lambda b, c: (row_blk0 + b, ca + c)),
                  pl.BlockSpec((n, CW), lambda b, c: (row_blk0 + b, cg + c)),
                  pl.BlockSpec((CONV_K, CW), lambda b, c: (0, c)),
                  pl.BlockSpec((1, CW), lambda b, c: (0, c))],
        out_specs=pl.BlockSpec((n, CW), lambda b, c: (b, c)),
        scratch_shapes=[pltpu.VMEM((n + 4 * SUBLANES, CW), f32)],
        compiler_params=_cparams(("parallel", "parallel")), name="conf_branch")
    return call(u, u, p["conv_dw_w"], p["conv_dw_b"].reshape(1, D))


def _pool_kernel(up_ref, w_ref, s_ref, o_ref, xp_ref, *, n):
    pad = SUBLANES
    gi = pl.program_id(1)
    half = jnp.left_shift(1, gi)
    xp_ref[pl.ds(0, pad), :] = jnp.zeros((pad, CW), f32)
    xp_ref[pl.ds(n + pad, pad), :] = jnp.zeros((pad, CW), f32)
    xp_ref[pl.ds(pad, n), :] = up_ref[...]
    wn = RC + 2 * pad

    def body(i, c):
        r0 = pl.multiple_of(i * RC, RC)
        w = xp_ref[pl.ds(r0, wn), :]
        s2 = w + pltpu.roll(w, 1, axis=0)
        s4 = pltpu.roll(s2, 1, axis=0) + pltpu.roll(s2, wn - 1, axis=0)
        s8 = pltpu.roll(s4, 2, axis=0) + pltpu.roll(s4, wn - 2, axis=0)
        s16 = pltpu.roll(s8, 4, axis=0) + pltpu.roll(s8, wn - 4, axis=0)
        s = jnp.where(gi == 0, s2, jnp.where(gi == 1, s4, jnp.where(gi == 2, s8, s16)))[pad:pad + RC, :]
        t = r0 + lax.broadcasted_iota(i32, (RC, CW), 0)
        cnt = (jnp.minimum(t + half, n) - jnp.maximum(t - half, 0)).astype(f32)
        pooled = s / cnt - w[pad:pad + RC, :]
        o_ref[pl.ds(r0, RC), :] = jnp.dot(pooled.astype(bf16), w_ref[0], preferred_element_type=f32) * s_ref[...]
        return c

    lax.fori_loop(0, n // RC, body, 0)


def pool_branch(u, p, nb, n, row_blk0):
    assert POOL_SIZES == (2, 4, 8, 16) and D // len(POOL_SIZES) == CW
    cp = COL_UP // CW
    kern = functools.partial(_pool_kernel, n=n)
    call = pl.pallas_call(
        kern, out_shape=_sds((nb * n, D)), grid=(nb, len(POOL_SIZES)),
        in_specs=[pl.BlockSpec((n, CW), lambda b, c: (row_blk0 + b, cp + c)),
                  pl.BlockSpec((1, CW, CW), lambda b, c: (c, 0, 0)),
                  pl.BlockSpec((1, CW), lambda b, c: (0, c))],
        out_specs=pl.BlockSpec((n, CW), lambda b, c: (b, c)),
        scratch_shapes=[pltpu.VMEM((n + 2 * SUBLANES, CW), f32)],
        compiler_params=_cparams(("parallel", "parallel")), name="pool_branch")
    return call(u, p["pool_w_bf"], p["pool_scale"].reshape(1, D))


def _head_norm(blk, g128, lane_lo):
    sq = blk * blk
    s_lo = jnp.sum(jnp.where(lane_lo, sq, 0.0), axis=-1, keepdims=True)
    s_hi = jnp.sum(jnp.where(lane_lo, 0.0, sq), axis=-1, keepdims=True)
    r = jnp.where(lane_lo, lax.rsqrt(s_lo * (1.0 / HEAD_DIM) + EPS), lax.rsqrt(s_hi * (1.0 / HEAD_DIM) + EPS))
    return blk * r * g128


def _rope(blk, cos, sin_signed, lane):
    partner = jnp.where((lane % 32) < 16, pltpu.roll(blk, LANES - 16, axis=1), pltpu.roll(blk, 16, axis=1))
    return blk * cos + partner * sin_signed


def _prep_kv(k_ref, v_ref, ks_ref, vs_ref, kg, lane_lo, rope=None, newk_ref=None, newv_ref=None):
    nk = k_ref.shape[0]
    lane = lax.broadcasted_iota(i32, (nk, LANES), 1)
    for tj in range(N_KV * HEAD_DIM // LANES):
        kt = _head_norm(k_ref[:, tj * LANES:(tj + 1) * LANES], kg, lane_lo(nk))
        if rope is not None:
            kt = _rope(kt, rope[0][...], rope[1][...], lane)
        vt = v_ref[:, tj * LANES:(tj + 1) * LANES]
        if newk_ref is not None:
            newk_ref[0, :, tj * LANES:(tj + 1) * LANES] = kt
            newv_ref[0, :, tj * LANES:(tj + 1) * LANES] = vt
        _store_low_half(kt, vt, ks_ref, vs_ref, tj, lane_lo(nk))


def _store_low_half(kt, vt, ks_ref, vs_ref, tj, lane_lo):
    ks_ref[2 * tj] = jnp.where(lane_lo, kt, 0.0).astype(bf16)
    vs_ref[2 * tj] = jnp.where(lane_lo, vt, 0.0).astype(bf16)
    ks_ref[2 * tj + 1] = jnp.where(lane_lo, pltpu.roll(kt, HEAD_DIM, axis=1), 0.0).astype(bf16)
    vs_ref[2 * tj + 1] = jnp.where(lane_lo, pltpu.roll(vt, HEAD_DIM, axis=1), 0.0).astype(bf16)


def _prep_cache(ck_ref, cv_ref, kc_ref, vc_ref, lane_lo):
    nk = ck_ref.shape[2]
    for tj in range(N_KV * HEAD_DIM // LANES):
        kt = ck_ref[0, 0, :, tj * LANES:(tj + 1) * LANES]
        vt = cv_ref[0, 0, :, tj * LANES:(tj + 1) * LANES]
        _store_low_half(kt, vt, kc_ref, vc_ref, tj, lane_lo(nk))


def _qk(qm, k):
    return lax.dot_general(qm, k, (((1,), (1,)), ((), ())), preferred_element_type=f32)


def _attend(sink_ref, q_tiles, segments, att_ref):
    per_kv = N_HEADS // N_KV
    assert per_kv == 4
    rows = per_kv * TQ
    row = lax.broadcasted_iota(i32, (rows, 1), 0)
    masks = [None if mk is None else jnp.concatenate([mk] * per_kv, axis=0) for _, _, mk in segments]
    for g in range(N_KV):
        t0, t1 = q_tiles[2 * g], q_tiles[2 * g + 1]
        qm = jnp.concatenate([t0, t1, pltpu.roll(t0, HEAD_DIM, axis=1), pltpu.roll(t1, HEAD_DIM, axis=1)],
                             axis=0).astype(bf16)
        hd = [per_kv * g, per_kv * g + 2, per_kv * g + 1, per_kv * g + 3]
        sink = jnp.where(row < TQ, sink_ref[hd[0]],
                         jnp.where(row < 2 * TQ, sink_ref[hd[1]],
                                   jnp.where(row < 3 * TQ, sink_ref[hd[2]], sink_ref[hd[3]]))) * LOG2E
        scores = []
        m_el = None
        for (kget, _, _), mask in zip(segments, masks):
            s = _qk(qm, kget(g))
            if mask is not None:
                s = jnp.where(mask, s, NEG_INF)
            for c in range(s.shape[1] // LANES):
                t = s[:, c * LANES:(c + 1) * LANES]
                m_el = t if m_el is None else jnp.maximum(m_el, t)
            scores.append(s)
        m = jnp.maximum(jnp.max(m_el, axis=-1, keepdims=True), sink)
        d_el = jnp.zeros((rows, LANES), f32)
        o = jnp.zeros((rows, LANES), f32)
        for s, (_, vget, _) in zip(scores, segments):
            pr = jnp.exp2(s - m)
            for c in range(s.shape[1] // LANES):
                d_el = d_el + pr[:, c * LANES:(c + 1) * LANES]
            o = o + jnp.dot(pr.astype(bf16), vget(g), preferred_element_type=f32)
        den = jnp.exp2(sink - m) + jnp.sum(d_el, axis=-1, keepdims=True)
        o = o / den
        att_ref[:, 2 * g * LANES:(2 * g + 1) * LANES] = o[:TQ] + pltpu.roll(o[2 * TQ:3 * TQ], HEAD_DIM, axis=1)
        att_ref[:, (2 * g + 1) * LANES:(2 * g + 2) * LANES] = o[TQ:2 * TQ] + pltpu.roll(o[3 * TQ:], HEAD_DIM, axis=1)


def _q_tiles(q_ref, qg, rope=None):
    lane = lax.broadcasted_iota(i32, (TQ, LANES), 1)
    lane_lo = lane < HEAD_DIM
    tiles = []
    for j in range(N_HEADS // 2):
        qt = _head_norm(q_ref[:, j * LANES:(j + 1) * LANES], qg, lane_lo)
        if rope is not None:
            qt = _rope(qt, rope[0], rope[1], lane)
        tiles.append(qt * (HEAD_DIM ** -0.5 * LOG2E))
    return tiles


def _lane_lo_fn(nrows):
    return lax.broadcasted_iota(i32, (nrows, LANES), 1) < HEAD_DIM


def _ctx_attn_kernel(sink_ref, q_ref, k_ref, v_ref, qg_ref, kg_ref, att_ref, newk_ref, newv_ref, ks_ref, vs_ref):
    @pl.when(pl.program_id(1) == 0)
    def _():
        _prep_kv(k_ref, v_ref, ks_ref, vs_ref, kg_ref[...], _lane_lo_fn, None, newk_ref, newv_ref)

    tiles = _q_tiles(q_ref, qg_ref[...])
    seg = [(lambda g: ks_ref[g], lambda g: vs_ref[g], None)]
    _attend(sink_ref, tiles, seg, att_ref)


def ctx_attention(u, p):
    nq = N_CTX // TQ
    kvw = N_KV * HEAD_DIM
    return pl.pallas_call(
        _ctx_attn_kernel,
        out_shape=(_sds((T_CTX, D)), _sds((N_CTX_B, N_CTX, kvw)), _sds((N_CTX_B, N_CTX, kvw))),
        grid=(N_CTX_B, nq),
        in_specs=[pl.BlockSpec(memory_space=pltpu.SMEM),
                  pl.BlockSpec((TQ, D), lambda b, i: (b * nq + i, COL_UQ // D)),
                  pl.BlockSpec((N_CTX, kvw), lambda b, i: (b, COL_UK // kvw)),
                  pl.BlockSpec((N_CTX, kvw), lambda b, i: (b, COL_UV // kvw)),
                  pl.BlockSpec((1, LANES), lambda b, i: (0, 0)),
                  pl.BlockSpec((1, LANES), lambda b, i: (0, 0))],
        out_specs=(pl.BlockSpec((TQ, D), lambda b, i: (b * nq + i, 0)),
                   pl.BlockSpec((1, N_CTX, kvw), lambda b, i: (b, 0, 0)),
                   pl.BlockSpec((1, N_CTX, kvw), lambda b, i: (b, 0, 0))),
        scratch_shapes=[pltpu.VMEM((N_KV, N_CTX, LANES), bf16), pltpu.VMEM((N_KV, N_CTX, LANES), bf16)],
        compiler_params=_cparams(("parallel", "arbitrary")), name="ctx_attention")(
            p["attn_sink"], u, u, u, p["q_norm_g2"], p["k_norm_g2"])


def _lat_attn_kernel(sink_ref, q_ref, k_ref, v_ref, ck_ref, cv_ref, cosq_ref, sinq_ref, cosk_ref, sink_tab_ref,
                     qg_ref, kg_ref, att_ref, ks_ref, vs_ref, kc_ref, vc_ref):
    i = pl.program_id(1)
    nq = N_LAT // TQ

    @pl.when(i == 0)
    def _():
        _prep_kv(k_ref, v_ref, ks_ref, vs_ref, kg_ref[...], _lane_lo_fn, (cosk_ref, sink_tab_ref))
        _prep_cache(ck_ref, cv_ref, kc_ref, vc_ref, _lane_lo_fn)

    tiles = _q_tiles(q_ref, qg_ref[...], (cosq_ref[...], sinq_ref[...]))
    r = lax.broadcasted_iota(i32, (TQ, TQ), 0)
    c = lax.broadcasted_iota(i32, (TQ, TQ), 1)
    prev0 = pl.multiple_of(jnp.maximum(i - 1, 0) * TQ, TQ)
    cur0 = pl.multiple_of(i * TQ, TQ)
    next0 = pl.multiple_of(jnp.minimum(i + 1, nq - 1) * TQ, TQ)
    mask_prev = jnp.logical_and(c >= r, i > 0)
    mask_next = jnp.logical_and(c <= r, i < nq - 1)

    def seg(r0, mask):
        return (lambda g: ks_ref[g, pl.ds(r0, TQ), :], lambda g: vs_ref[g, pl.ds(r0, TQ), :], mask)

    segments = [seg(prev0, mask_prev), seg(cur0, None), seg(next0, mask_next),
                (lambda g: kc_ref[g], lambda g: vc_ref[g], None)]
    _attend(sink_ref, tiles, segments, att_ref)


def lat_attention(u, p, cache_k, cache_v, rope_cos, rope_sin):
    nq = N_LAT // TQ
    kvw = N_KV * HEAD_DIM
    l = p["layer"]
    qblk0 = T_CTX // TQ
    kblk0 = T_CTX // N_LAT
    call = pl.pallas_call(
        _lat_attn_kernel, out_shape=_sds((T_LAT, D)), grid=(N_LAT_B, nq),
        in_specs=[pl.BlockSpec(memory_space=pltpu.SMEM),
                  pl.BlockSpec((TQ, D), lambda b, i: (qblk0 + b * nq + i, COL_UQ // D)),
                  pl.BlockSpec((N_LAT, kvw), lambda b, i: (kblk0 + b, COL_UK // kvw)),
                  pl.BlockSpec((N_LAT, kvw), lambda b, i: (kblk0 + b, COL_UV // kvw)),
                  pl.BlockSpec((1, 1, PAST, kvw), lambda b, i: (b, l, 0, 0)),
                  pl.BlockSpec((1, 1, PAST, kvw), lambda b, i: (b, l, 0, 0)),
                  pl.BlockSpec((TQ, LANES), lambda b, i: (i, 0)),
                  pl.BlockSpec((TQ, LANES), lambda b, i: (i, 0)),
                  pl.BlockSpec((N_LAT, LANES), lambda b, i: (0, 0)),
                  pl.BlockSpec((N_LAT, LANES), lambda b, i: (0, 0)),
                  pl.BlockSpec((1, LANES), lambda b, i: (0, 0)),
                  pl.BlockSpec((1, LANES), lambda b, i: (0, 0))],
        out_specs=pl.BlockSpec((TQ, D), lambda b, i: (b * nq + i, 0)),
        scratch_shapes=[pltpu.VMEM((N_KV, N_LAT, LANES), bf16), pltpu.VMEM((N_KV, N_LAT, LANES), bf16),
                        pltpu.VMEM((N_KV, PAST, LANES), bf16), pltpu.VMEM((N_KV, PAST, LANES), bf16)],
        compiler_params=_cparams(("parallel", "arbitrary")), name="lat_attention")
    return call(p["attn_sink"], u, u, u, cache_k, cache_v, rope_cos, rope_sin, rope_cos, rope_sin,
                p["q_norm_g2"], p["k_norm_g2"])


def rope_tables():
    pos = jnp.arange(N_LAT)
    rows = (pos // GRID_W).astype(f32)
    cols = (pos % GRID_W).astype(f32)
    half = HEAD_DIM // 2
    freqs = ROPE_BASE ** (-jnp.arange(0, half, 2, dtype=f32) / half)
    lane = jnp.arange(LANES)
    within = lane % half
    fidx = within % (half // 2)
    use_cols = (lane % HEAD_DIM) >= half
    ang = jnp.where(use_cols[None, :], cols[:, None], rows[:, None]) * freqs[fidx][None, :]
    sign = jnp.where(within < half // 2, -1.0, 1.0)
    return jnp.cos(ang), jnp.sin(ang) * sign[None, :]


def _merge_kernel(h_ref, xn_ref, lru_c, lru_l, conv_c, conv_l, att_c, att_l, pool_c, pool_l, mod_ref, lng_ref, lnb_ref,
                  wg_ref, bg_ref, wb_ref, wo_ref, o_ref):
    is_ctx = pl.program_id(0) < T_CTX // TM
    pick = lambda c_ref, l_ref: jnp.where(is_ctx, c_ref[...], l_ref[...])
    xn = xn_ref[...]
    hc = pick(conv_c, conv_l)
    mu = jnp.mean(hc, axis=-1, keepdims=True)
    xc = hc - mu
    var = jnp.mean(xc * xc, axis=-1, keepdims=True)
    y = xc * lax.rsqrt(var + EPS) * lng_ref[...] + lnb_ref[...]
    conv = y * _sigmoid(y)
    merged = jnp.zeros((TM, D), f32)
    for j, br in enumerate((pick(lru_c, lru_l), conv, pick(att_c, att_l), pick(pool_c, pool_l))):
        gate = _sigmoid(jnp.dot(xn, wg_ref[:, j * D:(j + 1) * D], preferred_element_type=f32)
                        + bg_ref[:, j * D:(j + 1) * D])
        proj = jnp.dot(br.astype(bf16), wb_ref[j], preferred_element_type=f32)
        merged = merged + gate * proj
    out = jnp.dot(merged.astype(bf16), wo_ref[...], preferred_element_type=f32)
    o_ref[...] = h_ref[...] + mod_ref[0][2:3, :] * out


def merge(h, xn, lru, conv, att, pool, mod_tiles, p):
    nt = h.shape[0] // TM
    n_ctx_tiles = T_CTX // TM
    tok = pl.BlockSpec((TM, D), lambda i: (i, 0))
    tok_c = pl.BlockSpec((TM, D), lambda i: (jnp.minimum(i, n_ctx_tiles - 1), 0))
    tok_l = pl.BlockSpec((TM, D), lambda i: (jnp.maximum(i - n_ctx_tiles, 0), 0))
    const = lambda shape: pl.BlockSpec(shape, lambda i: (0,) * len(shape), pipeline_mode=pl.Buffered(1))
    return pl.pallas_call(
        _merge_kernel, out_shape=_sds(h.shape), grid=(nt,),
        in_specs=[tok, tok, tok_c, tok_l, tok_c, tok_l, tok_c, tok_l, tok_c, tok_l,
                  pl.BlockSpec((1, 6, D), lambda i: (i, 0, 0)),
                  const((1, D)), const((1, D)),
                  const((D, 4 * D)), const((1, 4 * D)), const((4, D, D)), const((D, D))],
        out_specs=tok,
        compiler_params=_cparams(("parallel",)), name="merge")(
            h, xn, *lru, *conv, *att, *pool, mod_tiles, p["conv_ln_g"].reshape(1, D), p["conv_ln_b"].reshape(1, D),
            p["w_gate_bf"], p["b_gate"].reshape(1, 4 * D), p["w_branch_bf"], p["w_out_bf"])


def _router_kernel(h_ref, g_ref, mod_ref, rw_ref, rb_ref, xn_ref, topi_ref, topw_ref, rank_ref, cnt_ref, carry_ref):
    step = pl.program_id(0)

    @pl.when(step == 0)
    def _():
        carry_ref[...] = jnp.zeros_like(carry_ref)

    m = mod_ref[0]
    xn = _rms_mod(h_ref[...], g_ref[...], m[3:4, :], m[4:5, :])
    xn_ref[...] = xn.reshape(xn_ref.shape)
    logits = jnp.dot(xn, rw_ref[...], preferred_element_type=f32, precision=lax.Precision.HIGHEST) + rb_ref[...]
    lane = lax.broadcasted_iota(i32, (TM, LANES), 1).astype(f32)
    lg = jnp.where(lane < N_EXPERTS, logits, NEG_INF)
    vals, idxs = [], []
    for _ in range(TOP_K):
        mx = jnp.max(lg, axis=-1, keepdims=True)
        idx = jnp.min(jnp.where(lg == mx, lane, float(LANES)), axis=-1, keepdims=True)
        vals.append(mx)
        idxs.append(idx)
        lg = jnp.where(lane == idx, -3e38, lg)
    exps = [jnp.exp(v - vals[0]) for v in vals]
    den = exps[0] + exps[1] + exps[2] + exps[3]
    cnt = jnp.zeros((TM, LANES), f32)
    for idx in idxs:
        cnt = cnt + jnp.where(lane == idx, 1.0, 0.0)
    rr = lax.broadcasted_iota(i32, (TM, TM), 0)
    cc = lax.broadcasted_iota(i32, (TM, TM), 1)
    tri = jnp.where(rr > cc, 1.0, 0.0).astype(bf16)
    before = jnp.dot(tri, cnt.astype(bf16), preferred_element_type=f32) + carry_ref[0:1, :]
    topi = jnp.zeros((TM, LANES), f32)
    topw = jnp.zeros((TM, LANES), f32)
    rank = jnp.zeros((TM, LANES), f32)
    for k in range(TOP_K):
        rk = jnp.sum(jnp.where(lane == idxs[k], before, 0.0), axis=-1, keepdims=True)
        topi = jnp.where(lane == k, idxs[k], topi)
        topw = jnp.where(lane == k, exps[k] / den, topw)
        rank = jnp.where(lane == k, rk, rank)
    topi_ref[...] = topi.astype(i32)
    topw_ref[...] = topw
    rank_ref[...] = rank.astype(i32)
    total = carry_ref[0:1, :] + jnp.sum(cnt, axis=0, keepdims=True)
    carry_ref[...] = jnp.broadcast_to(total, carry_ref.shape)
    cnt_ref[...] = jnp.broadcast_to(total, cnt_ref.shape).astype(i32)


def router(h, mod_tiles, p):
    nt = h.shape[0] // TM
    tok = lambda w, dt=f32: pl.BlockSpec((TM, w), lambda i: (i, 0))
    t = h.shape[0]
    return pl.pallas_call(
        _router_kernel,
        out_shape=(_sds((t, D // LANES, LANES)), _sds((t, LANES), i32), _sds((t, LANES)), _sds((t, LANES), i32),
                   _sds((SUBLANES, LANES), i32)),
        grid=(nt,),
        in_specs=[tok(D), pl.BlockSpec((1, D), lambda i: (0, 0)), pl.BlockSpec((1, 6, D), lambda i: (i, 0, 0)),
                  pl.BlockSpec((D, LANES), lambda i: (0, 0)), pl.BlockSpec((1, LANES), lambda i: (0, 0))],
        out_specs=(pl.BlockSpec((TM, D // LANES, LANES), lambda i: (i, 0, 0)), tok(LANES), tok(LANES), tok(LANES),
                   pl.BlockSpec((SUBLANES, LANES), lambda i: (0, 0))),
        scratch_shapes=[pltpu.VMEM((SUBLANES, LANES), f32)],
        compiler_params=_cparams(("arbitrary",)), name="router")(
            h, p["norm2_g"].reshape(1, D), mod_tiles, p["router_w_pad"], p["router_b_pad"])


def _dest_kernel(topi_ref, rank_ref, pstart_ref, o_ref):
    lane = lax.broadcasted_iota(i32, (TM, LANES), 1).astype(f32)
    topi = topi_ref[...].astype(f32)
    pstart = pstart_ref[...].astype(f32)
    dest = rank_ref[...].astype(f32)
    for k in range(TOP_K):
        e = jnp.sum(jnp.where(lane == k, topi, 0.0), axis=-1, keepdims=True)
        ps = jnp.sum(jnp.where(lane == e, pstart, 0.0), axis=-1, keepdims=True)
        dest = dest + jnp.where(lane == k, ps, 0.0)
    o_ref[...] = dest.astype(i32)


def dest_rows(topi, rank, pstart):
    t = topi.shape[0]
    tok = pl.BlockSpec((TM, LANES), lambda i: (i, 0))
    return pl.pallas_call(
        _dest_kernel, out_shape=_sds((t, LANES), i32), grid=(t // TM,),
        in_specs=[tok, tok, pl.BlockSpec((1, LANES), lambda i: (0, 0))], out_specs=tok,
        compiler_params=_cparams(("parallel",)), name="dest_rows")(topi, rank, pstart)


ROW3 = (D // LANES, LANES)
assert ROW3[0] == SUBLANES


def _row_copy(src_ref, s, dst_ref, d, sem):
    return pltpu.make_async_copy(src_ref.at[pl.ds(s, 1)], dst_ref.at[pl.ds(d, 1)], sem)


def _dispatch_kernel(zflag_ref, dest_ref, x_ref, xb_ref, zero_ref, sem):
    @pl.when(pl.program_id(0) == 0)
    def _():
        zero_ref[...] = jnp.zeros_like(zero_ref)

        def zcopy(j):
            r0 = pl.multiple_of(j * MOE_BM, MOE_BM)
            return pltpu.make_async_copy(zero_ref, xb_ref.at[pl.ds(r0, MOE_BM)], sem)

        def zstart(j, c):
            @pl.when(zflag_ref[j] != 0)
            def _():
                zcopy(j).start()
            return c

        def zwait(j, c):
            @pl.when(zflag_ref[j] != 0)
            def _():
                zcopy(j).wait()
            return c

        lax.fori_loop(0, MOE_NBLK, zstart, 0)
        lax.fori_loop(0, MOE_NBLK, zwait, 0)

    def start(t, c):
        for k in range(TOP_K):
            _row_copy(x_ref, t, xb_ref, dest_ref[t * TOP_K + k], sem).start(priority=k % 2)
        return c

    lax.fori_loop(0, TM, start, 0)
    assert (TOP_K * TM) % MOE_BM == 0
    for _ in range(TOP_K * TM // MOE_BM):
        pltpu.make_async_copy(zero_ref, xb_ref.at[pl.ds(0, MOE_BM)], sem).wait()


def dispatch(zflag, dest_flat, xn3):
    t = xn3.shape[0]
    grid_spec = pltpu.PrefetchScalarGridSpec(
        num_scalar_prefetch=1, grid=(t // TM,),
        in_specs=[pl.BlockSpec((TM * TOP_K,), lambda i, zb: (i,), memory_space=pltpu.SMEM),
                  pl.BlockSpec((TM,) + ROW3, lambda i, zb: (i, 0, 0))],
        out_specs=pl.BlockSpec(memory_space=pl.ANY),
        scratch_shapes=[pltpu.VMEM((MOE_BM,) + ROW3, f32), pltpu.SemaphoreType.DMA(())])
    return pl.pallas_call(
        _dispatch_kernel, out_shape=_sds((MOE_CAP,) + ROW3), grid_spec=grid_spec,
        compiler_params=_cparams(("arbitrary",)), name="moe_dispatch")(zflag, dest_flat, xn3)


def _expert_kernel(be_ref, nused_ref, first_ref, wslot_ref, nxt_ref, xb_ref, wgu_hbm, bgu_ref, wdn_hbm, bdn_ref,
                   yb_ref, wgu_f, wdn_f, wsem, wgu_bf, wdn_bf, *, layer):
    i = pl.program_id(0)
    e = be_ref[i]

    def weight_copies(ex, s):
        return (pltpu.make_async_copy(wgu_hbm.at[layer, ex], wgu_f.at[s], wsem.at[s]),
                pltpu.make_async_copy(wdn_hbm.at[layer, ex], wdn_f.at[s], wsem.at[s]))

    @pl.when(first_ref[i] != 0)
    def _():
        s = wslot_ref[i]

        @pl.when(i == 0)
        def _():
            for cp in weight_copies(e, s):
                cp.start()

        for cp in weight_copies(e, s):
            cp.wait()
        wgu_bf[...] = wgu_f[s].astype(bf16)
        wdn_bf[...] = wdn_f[s].astype(bf16)

        @pl.when(nxt_ref[i] >= 0)
        def _():
            for cp in weight_copies(nxt_ref[i], 1 - s):
                cp.start()

    @pl.when(i < nused_ref[0])
    def _():
        x = xb_ref[...].reshape(MOE_BM, D).astype(bf16)
        h = jnp.dot(x, wgu_bf[...], preferred_element_type=f32) + bgu_ref[0, 0]
        gate = jnp.minimum(h[:, :D], SWIGLU_LIMIT)
        lin = jnp.clip(h[:, D:], -SWIGLU_LIMIT, SWIGLU_LIMIT)
        act = (lin + 1.0) * (gate * _sigmoid(SWIGLU_ALPHA * gate))
        y = jnp.dot(act.astype(bf16), wdn_bf[...], preferred_element_type=f32) + bdn_ref[0, 0]
        yb_ref[...] = y.reshape((MOE_BM,) + ROW3)

    @pl.when(i >= nused_ref[0])
    def _():
        yb_ref[...] = jnp.zeros_like(yb_ref)


def experts(block_e, nused, first, wslot, nxt, xb, p):
    l = p["layer"]
    grid_spec = pltpu.PrefetchScalarGridSpec(
        num_scalar_prefetch=5, grid=(MOE_NBLK,),
        in_specs=[pl.BlockSpec((MOE_BM,) + ROW3,
                               lambda i, be, nu, *_: (jnp.maximum(jnp.minimum(i, nu[0] - 1), 0), 0, 0)),
                  pl.BlockSpec(memory_space=pl.ANY),
                  pl.BlockSpec((1, 1, 1, 2 * D), lambda i, be, *_: (l, be[i], 0, 0)),
                  pl.BlockSpec(memory_space=pl.ANY),
                  pl.BlockSpec((1, 1, 1, D), lambda i, be, *_: (l, be[i], 0, 0))],
        out_specs=pl.BlockSpec((MOE_BM,) + ROW3, lambda i, *_: (i, 0, 0)),
        scratch_shapes=[pltpu.VMEM((2, D, 2 * D), f32), pltpu.VMEM((2, D, D), f32), pltpu.SemaphoreType.DMA((2,)),
                        pltpu.VMEM((D, 2 * D), bf16), pltpu.VMEM((D, D), bf16)])
    return pl.pallas_call(
        functools.partial(_expert_kernel, layer=l), out_shape=_sds((MOE_CAP,) + ROW3), grid_spec=grid_spec,
        compiler_params=_cparams(("arbitrary",)), name="moe_experts")(
            block_e, nused, first, wslot, nxt, xb, p["exp_w_gu"],
            p["exp_b_gu"].reshape(DEPTH, N_EXPERTS, 1, 2 * D), p["exp_w_down"],
            p["exp_b_down"].reshape(DEPTH, N_EXPERTS, 1, D))


def _combine_kernel(dest_ref, dest_nxt_ref, yb_ref, topw_ref, h_ref, mod_ref, o_ref, buf, sem):
    i = pl.program_id(0)
    slot = lax.rem(i, 2)

    def issue(d_ref, s):
        def start(t, c):
            for k in range(TOP_K):
                pltpu.make_async_copy(yb_ref.at[pl.ds(d_ref[t * TOP_K + k], 1)], buf.at[s, pl.ds(k * TM + t, 1)],
                                      sem.at[s]).start(priority=k % 2)
            return c

        lax.fori_loop(0, TM, start, 0)

    @pl.when(i == 0)
    def _():
        issue(dest_ref, 0)

    @pl.when(i + 1 < pl.num_programs(0))
    def _():
        issue(dest_nxt_ref, 1 - slot)

    pltpu.make_async_copy(yb_ref.at[pl.ds(0, TOP_K * TM)], buf.at[slot], sem.at[slot]).wait()
    w = topw_ref[...]
    y = jnp.zeros((TM, D), f32)
    for k in range(TOP_K):
        y = y + buf[slot, pl.ds(k * TM, TM)].reshape(TM, D) * w[:, k:k + 1]
    o_ref[...] = h_ref[...] + mod_ref[0][5:6, :] * y


def combine(dest_flat, yb, topw, h, mod_tiles):
    t = h.shape[0]
    nt = t // TM
    tok = pl.BlockSpec((TM, D), lambda i: (i, 0))
    return pl.pallas_call(
        _combine_kernel, out_shape=_sds(h.shape), grid=(nt,),
        in_specs=[pl.BlockSpec((TM * TOP_K,), lambda i: (i,), memory_space=pltpu.SMEM),
                  pl.BlockSpec((TM * TOP_K,), lambda i: (jnp.minimum(i + 1, nt - 1),), memory_space=pltpu.SMEM),
                  pl.BlockSpec(memory_space=pl.ANY),
                  pl.BlockSpec((TM, LANES), lambda i: (i, 0)), tok,
                  pl.BlockSpec((1, 6, D), lambda i: (i, 0, 0))],
        out_specs=tok,
        scratch_shapes=[pltpu.VMEM((2, TOP_K * TM) + ROW3, f32), pltpu.SemaphoreType.DMA((2,))],
        compiler_params=_cparams(("arbitrary",)), name="moe_combine")(
            dest_flat, dest_flat, yb, topw, h, mod_tiles)


def moe_layer(h, mod_tiles, p):
    xn3, topi, topw, rank, counts = router(h, mod_tiles, p)
    cnt = counts[0, :N_EXPERTS]
    padded = (cnt + MOE_BM - 1) // MOE_BM * MOE_BM
    pend = jnp.cumsum(padded)
    pstart = jnp.zeros((1, LANES), i32).at[0, :N_EXPERTS].set(pend - padded)
    blk_row0 = jnp.arange(MOE_NBLK, dtype=i32) * MOE_BM
    block_e = jnp.minimum(jnp.sum((pend[None, :] <= blk_row0[:, None]).astype(i32), axis=1), N_EXPERTS - 1)
    nused = (pend[-1:] // MOE_BM).astype(i32)
    blk = jnp.arange(MOE_NBLK, dtype=i32)
    last_of_expert = jnp.any((padded > 0)[None, :] & (blk[:, None] == (pend // MOE_BM - 1)[None, :]), axis=1)
    zflag = (last_of_expert | (blk >= nused[0])).astype(i32)
    dest = dest_rows(topi, rank, pstart)
    dest_flat = dest[:, :TOP_K].reshape(-1)
    xb = dispatch(zflag, dest_flat, xn3)
    first = ((blk < nused[0]) & (block_e != jnp.concatenate([jnp.full((1,), -1, i32), block_e[:-1]]))).astype(i32)
    wslot = (jnp.maximum(jnp.cumsum(first) - 1, 0) % 2).astype(i32)
    eidx = jnp.arange(N_EXPERTS, dtype=i32)
    later = jnp.where((padded > 0)[None, :] & (eidx[None, :] > eidx[:, None]), eidx[None, :], N_EXPERTS)
    nxt_tab = jnp.min(later, axis=1)
    nxt = jnp.where(nxt_tab == N_EXPERTS, -1, nxt_tab)[block_e].astype(i32)
    yb = experts(block_e, nused, first, wslot, nxt, xb, p)
    return combine(dest_flat, yb, topw, h, mod_tiles)


def _block_diag(w):
    per = CW // LRU_BW
    w5 = w.reshape(2, D // CW, per, LRU_BW, LRU_BW)
    eye = jnp.eye(per, dtype=w.dtype)
    bd = w5[:, :, :, :, None, :] * eye[None, None, :, None, :, None]
    return bd.reshape(2, D // CW, CW, CW).astype(bf16)


def _layer_params(l, a):
    tile2 = lambda g: jnp.tile(g, 2).reshape(1, LANES)
    whole = ("exp_w_gu", "exp_b_gu", "exp_w_down", "exp_b_down", "w_in", "w_gate", "w_branch", "w_out")
    p = {k: v[l] for k, v in a.items() if k not in whole}
    p["layer"] = l
    for k in ("exp_w_gu", "exp_b_gu", "exp_w_down", "exp_b_down"):
        p[k] = a[k]
    p["w_in_bf"] = cast_bf16(a["w_in"], l)
    p["w_gate_bf"] = cast_bf16(a["w_gate"], l)
    p["w_branch_bf"] = cast_bf16(a["w_branch"].reshape(DEPTH, 4 * D, D), l).reshape(4, D, D)
    p["w_out_bf"] = cast_bf16(a["w_out"], l)
    p["pool_w_bf"] = cast_bf16(a["pool_w"].reshape(DEPTH, D, CW), l).reshape(4, CW, CW)
    p["lru_wr_bd"] = _block_diag(p["lru_w_r"])
    p["lru_wi_bd"] = _block_diag(p["lru_w_i"])
    p["q_norm_g2"] = tile2(p["q_norm_g"])
    p["k_norm_g2"] = tile2(p["k_norm_g"])
    p["router_w_pad"] = jnp.pad(p["router_w"], ((0, 0), (0, LANES - N_EXPERTS)))
    p["router_b_pad"] = jnp.pad(p["router_b"], (0, LANES - N_EXPERTS)).reshape(1, LANES)
    return p


def kernel(x_prompt, x_sample, cache_k, cache_v, state_lru, c, c_ctx, norm1_g, norm2_g, w_mod, b_mod, w_in,
           lru_conv_w, lru_conv_b, lru_w_r, lru_b_r, lru_w_i, lru_b_i, lru_lambda, conv_dw_w, conv_dw_b,
           conv_ln_g, conv_ln_b, q_norm_g, k_norm_g, attn_sink, pool_w, pool_scale, w_branch, w_gate, b_gate,
           w_out, router_w, router_b, exp_w_gu, exp_b_gu, exp_w_down, exp_b_down):
    weights = dict(norm1_g=norm1_g, norm2_g=norm2_g, w_in=w_in, lru_conv_w=lru_conv_w, lru_conv_b=lru_conv_b,
                   lru_w_r=lru_w_r, lru_b_r=lru_b_r, lru_w_i=lru_w_i, lru_b_i=lru_b_i, lru_lambda=lru_lambda,
                   conv_dw_w=conv_dw_w, conv_dw_b=conv_dw_b, conv_ln_g=conv_ln_g, conv_ln_b=conv_ln_b,
                   q_norm_g=q_norm_g, k_norm_g=k_norm_g, attn_sink=attn_sink, pool_w=pool_w, pool_scale=pool_scale,
                   w_branch=w_branch, w_gate=w_gate, b_gate=b_gate, w_out=w_out, router_w=router_w,
                   router_b=router_b, exp_w_gu=exp_w_gu, exp_b_gu=exp_b_gu, exp_w_down=exp_w_down,
                   exp_b_down=exp_b_down)
    kvw = N_KV * HEAD_DIM
    n_cond = 2 * SUBLANES
    cond = jnp.concatenate([c_ctx[None, :], c, jnp.zeros((n_cond - 1 - N_LAT_B, D), f32)], axis=0)
    mod = modulation(cond, w_mod, b_mod)
    tile_start = jnp.arange(T // TM) * TM
    tile_row = jnp.where(tile_start < T_CTX, 0, 1 + (tile_start - T_CTX) // N_LAT)
    h = jnp.concatenate([x_prompt.reshape(T_CTX, D), x_sample.reshape(T_LAT, D)], axis=0)
    ck = cache_k.reshape(N_LAT_B, DEPTH, PAST, kvw)
    cv = cache_v.reshape(N_LAT_B, DEPTH, PAST, kvw)
    rope_cos, rope_sin = rope_tables()
    zero_state = jnp.zeros((N_CTX_B, 2, D), f32)
    new_k, new_v, new_s = [], [], []
    for l in range(DEPTH):
        p = _layer_params(l, weights)
        mod_tiles = mod[l][tile_row].reshape(T // TM, 6, D)
        xn = norm1(h, p["norm1_g"], mod_tiles)
        u = in_proj(xn, p["w_in_bf"])
        lat_blk0 = T_CTX // N_LAT
        lru_c, st_c = lru_branch(u, p, zero_state, N_CTX_B, N_CTX, 0)
        lru_l, _ = lru_branch(u, p, state_lru[:, l], N_LAT_B, N_LAT, lat_blk0)
        conv = (conf_branch(u, p, N_CTX_B, N_CTX, 0), conf_branch(u, p, N_LAT_B, N_LAT, lat_blk0))
        pool = (pool_branch(u, p, N_CTX_B, N_CTX, 0), pool_branch(u, p, N_LAT_B, N_LAT, lat_blk0))
        att_c, k_l, v_l = ctx_attention(u, p)
        att_l = lat_attention(u, p, ck, cv, rope_cos, rope_sin)
        h = merge(h, xn, (lru_c, lru_l), conv, (att_c, att_l), pool, mod_tiles, p)
        h = moe_layer(h, mod_tiles, p)
        new_k.append(k_l.reshape(N_CTX_B, N_CTX, N_KV, HEAD_DIM))
        new_v.append(v_l.reshape(N_CTX_B, N_CTX, N_KV, HEAD_DIM))
        new_s.append(st_c)
    y_prompt = h[:T_CTX].reshape(N_CTX_B, N_CTX, D)
    y_sample = h[T_CTX:].reshape(N_LAT_B, N_LAT, D)
    return (y_prompt, y_sample, jnp.stack(new_k, axis=1), jnp.stack(new_v, axis=1), jnp.stack(new_s, axis=1))
```
